```python
import math
import jax, jax.numpy as jnp
from jax import lax
import numpy as np

D_MODEL = 1024
BATCH = 8
SEQ = 2048
DEPTH = 2

GRID_W = 64
CTX_LEN = 256
DIFF_HEADS = 4
DIFF_HEAD_DIM = 64
GQA_HEADS = 8
GQA_KV_HEADS = 2
GQA_HEAD_DIM = 64
ROPE_THETA = 10000.0
ROPE_PAIRS_PER_AXIS = GQA_HEAD_DIM // 4
FOURIER_GROUPS = 8
FOURIER_GROUP_DIM = D_MODEL // FOURIER_GROUPS
D_FF = -(-8 * D_MODEL // (3 * 256)) * 256
Q_BLOCK = 128
EPS = 1e-6
N_MOD = 6

A_Q_W = DIFF_HEADS * 2 * DIFF_HEAD_DIM
A_K_W = DIFF_HEADS * 2 * DIFF_HEAD_DIM
A_V_W = DIFF_HEADS * 2 * DIFF_HEAD_DIM
B_Q_W = GQA_HEADS * GQA_HEAD_DIM
B_K_W = GQA_KV_HEADS * GQA_HEAD_DIM
B_V_W = GQA_KV_HEADS * GQA_HEAD_DIM
OFF_AQ = 0
OFF_AK = OFF_AQ + A_Q_W
OFF_AV = OFF_AK + A_K_W
OFF_BQ = OFF_AV + A_V_W
OFF_BK = OFF_BQ + B_Q_W
OFF_BV = OFF_BK + B_K_W
IN_WIDTH = OFF_BV + B_V_W
MIX_WIDTH = DIFF_HEADS * 2 * DIFF_HEAD_DIM + GQA_HEADS * GQA_HEAD_DIM

kernel_name = "hybrid_diffattn_gqa_fnet_dit_block"


def rms_norm(x, g):
    xf = x.astype(jnp.float32)
    y = xf * lax.rsqrt(jnp.mean(xf * xf, axis=-1, keepdims=True) + EPS)
    return (y * g.astype(jnp.float32)).astype(x.dtype)


def modulate(h, shift, scale):
    return h * (1 + scale) + shift


def ada_params(cond, w, b):
    m = jax.nn.silu(cond) @ w + b
    return [t[..., None, :] for t in jnp.split(m, N_MOD, axis=-1)]


def rope_tables(n_tokens):
    rows_count = n_tokens // GRID_W
    row = jnp.repeat(jnp.arange(rows_count, dtype=jnp.int32), GRID_W)
    col = jnp.tile(jnp.arange(GRID_W, dtype=jnp.int32), rows_count)
    inv = ROPE_THETA ** (-jnp.arange(ROPE_PAIRS_PER_AXIS, dtype=jnp.float32) / ROPE_PAIRS_PER_AXIS)
    ang = jnp.stack([row.astype(jnp.float32)[:, None] * inv, col.astype(jnp.float32)[:, None] * inv], axis=1)
    return jnp.cos(ang), jnp.sin(ang)


def apply_rope(x, cos, sin):
    b, s, n, d = x.shape
    xr = x.astype(jnp.float32).reshape(b, s, n, 2, 2, ROPE_PAIRS_PER_AXIS)
    x1, x2 = xr[..., 0, :], xr[..., 1, :]
    cb, sb = cos[None, :, None], sin[None, :, None]
    out = jnp.stack([x1 * cb - x2 * sb, x2 * cb + x1 * sb], axis=-2)
    return out.reshape(b, s, n, d).astype(x.dtype)


def to_blocks(t):
    b, s = t.shape[:2]
    return jnp.moveaxis(t.reshape(b, s // Q_BLOCK, Q_BLOCK, *t.shape[2:]), 1, 0)


def from_blocks(t):
    nb, b, q = t.shape[:3]
    return jnp.moveaxis(t, 0, 1).reshape(b, nb * q, *t.shape[3:])


def diff_attention(q1, q2, k1, k2, v, lam):
    scale = DIFF_HEAD_DIM ** -0.5

    def block(qb):
        b1, b2 = qb
        s1 = jnp.einsum('bqhd,bkhd->bhqk', b1, k1, preferred_element_type=jnp.float32) * scale
        s2 = jnp.einsum('bqhd,bkhd->bhqk', b2, k2, preferred_element_type=jnp.float32) * scale
        w = jax.nn.softmax(s1, axis=-1) - lam * jax.nn.softmax(s2, axis=-1)
        return jnp.einsum('bhqk,bkhe->bqhe', w.astype(v.dtype), v)

    return from_blocks(lax.map(block, (to_blocks(q1), to_blocks(q2))))


def gqa_attention(q, k, v):
    b = q.shape[0]
    grp = GQA_HEADS // GQA_KV_HEADS
    scale = GQA_HEAD_DIM ** -0.5

    def block(qb):
        qg = qb.reshape(b, Q_BLOCK, GQA_KV_HEADS, grp, GQA_HEAD_DIM)
        s = jnp.einsum('bqhgd,bkhd->bhgqk', qg, k, preferred_element_type=jnp.float32) * scale
        p = jax.nn.softmax(s, axis=-1)
        o = jnp.einsum('bhgqk,bkhd->bqhgd', p.astype(v.dtype), v)
        return o.reshape(b, Q_BLOCK, GQA_HEADS, GQA_HEAD_DIM)

    return from_blocks(lax.map(block, to_blocks(q)))


def attention_mixer(x, ctx, cos, sin, mod, mod_ctx, p, layer_idx):
    b, s, _ = x.shape
    n_ctx = ctx.shape[1]
    shift, scale, gate = mod
    shift_c, scale_c = mod_ctx
    h = modulate(rms_norm(x, p['norm_mix']), shift, scale)
    hc = modulate(rms_norm(ctx, p['norm_mix']), shift_c, scale_c)
    w_in = p['w_in']
    proj = h @ w_in
    ctx_a = hc @ w_in[:, OFF_AK:OFF_BQ]
    ctx_b = hc @ w_in[:, OFF_BK:IN_WIDTH]

    q_a = apply_rope(proj[..., OFF_AQ:OFF_AK].reshape(b, s, 2 * DIFF_HEADS, DIFF_HEAD_DIM), cos, sin)
    q_a = q_a.reshape(b, s, DIFF_HEADS, 2, DIFF_HEAD_DIM)
    k_a = apply_rope(proj[..., OFF_AK:OFF_AV].reshape(b, s, 2 * DIFF_HEADS, DIFF_HEAD_DIM), cos, sin)
    k_a = k_a.reshape(b, s, DIFF_HEADS, 2, DIFF_HEAD_DIM)
    v_a = proj[..., OFF_AV:OFF_BQ].reshape(b, s, DIFF_HEADS, 2 * DIFF_HEAD_DIM)
    kc_a = ctx_a[..., :A_K_W].reshape(b, n_ctx, DIFF_HEADS, 2, DIFF_HEAD_DIM)
    vc_a = ctx_a[..., A_K_W:].reshape(b, n_ctx, DIFF_HEADS, 2 * DIFF_HEAD_DIM)
    k1 = jnp.concatenate([kc_a[..., 0, :], k_a[..., 0, :]], axis=1)
    k2 = jnp.concatenate([kc_a[..., 1, :], k_a[..., 1, :]], axis=1)
    v_all_a = jnp.concatenate([vc_a, v_a], axis=1)
    lam_init = 0.8 - 0.6 * math.exp(-0.3 * layer_idx)
    f32 = jnp.float32
    lam = (jnp.exp(jnp.sum(p['lambda_q1'].astype(f32) * p['lambda_k1'].astype(f32)))
           - jnp.exp(jnp.sum(p['lambda_q2'].astype(f32) * p['lambda_k2'].astype(f32))) + lam_init)
    o_a = diff_attention(q_a[..., 0, :], q_a[..., 1, :], k1, k2, v_all_a, lam)
    o_a = (rms_norm(o_a, p['subln']) * (1.0 - lam_init)).reshape(b, s, DIFF_HEADS * 2 * DIFF_HEAD_DIM)

    q_b = rms_norm(proj[..., OFF_BQ:OFF_BK].reshape(b, s, GQA_HEADS, GQA_HEAD_DIM), p['q_norm'])
    q_b = apply_rope(q_b, cos, sin)
    k_b = rms_norm(proj[..., OFF_BK:OFF_BV].reshape(b, s, GQA_KV_HEADS, GQA_HEAD_DIM), p['k_norm'])
    k_b = apply_rope(k_b, cos, sin)
    v_b = proj[..., OFF_BV:IN_WIDTH].reshape(b, s, GQA_KV_HEADS, GQA_HEAD_DIM)
    kc_b = rms_norm(ctx_b[..., :B_K_W].reshape(b, n_ctx, GQA_KV_HEADS, GQA_HEAD_DIM), p['k_norm'])
    vc_b = ctx_b[..., B_K_W:].reshape(b, n_ctx, GQA_KV_HEADS, GQA_HEAD_DIM)
    o_b = gqa_attention(q_b, jnp.concatenate([kc_b, k_b], axis=1), jnp.concatenate([vc_b, v_b], axis=1))
    o_b = o_b.reshape(b, s, GQA_HEADS * GQA_HEAD_DIM)

    y = jnp.concatenate([o_a, o_b], axis=-1) @ p['w_out']
    return x + gate * y


def fourier_mixer(x, mod, p):
    b, s, d = x.shape
    shift, scale, gate = mod
    h = modulate(rms_norm(x, p['norm_mix']), shift, scale)
    hg = h.reshape(b, s, FOURIER_GROUPS, FOURIER_GROUP_DIM).astype(jnp.float32)
    f = jnp.fft.fft2(hg, axes=(1, 3), norm='ortho').real.astype(h.dtype).reshape(b, s, d)
    return x + gate * (f @ p['w_out'])


def swiglu_ffn(x, mod, p):
    shift, scale, gate = mod
    h = modulate(rms_norm(x, p['norm_ffn']), shift, scale)
    g, u = jnp.split(h @ p['w_gate_up'], 2, axis=-1)
    return x + gate * ((jax.nn.silu(g) * u) @ p['w_down'])


def setup_inputs(seed: int = 0) -> dict:
    key = jax.random.key(seed)
    ks = iter(jax.random.split(key, 32))

    def nrm(shape, scale):
        return jax.random.normal(next(ks), shape, jnp.float32) * scale

    def gain(n):
        return 1.0 + nrm((n,), 0.05)

    fan = D_MODEL ** -0.5
    inp = {
        'x': nrm((BATCH, SEQ, D_MODEL), 1.0),
        'c': nrm((BATCH, D_MODEL), 1.0),
        'ctx': nrm((BATCH, CTX_LEN, D_MODEL), 1.0),
        'c_ctx': nrm((D_MODEL,), 1.0),
        'l0_ada_w': nrm((D_MODEL, N_MOD * D_MODEL), 0.5 * fan),
        'l0_ada_b': nrm((N_MOD * D_MODEL,), 0.02),
        'l0_norm_mix': gain(D_MODEL),
        'l0_w_in': nrm((D_MODEL, IN_WIDTH), fan),
        'l0_lambda_q1': nrm((DIFF_HEAD_DIM,), 0.1),
        'l0_lambda_k1': nrm((DIFF_HEAD_DIM,), 0.1),
        'l0_lambda_q2': nrm((DIFF_HEAD_DIM,), 0.1),
        'l0_lambda_k2': nrm((DIFF_HEAD_DIM,), 0.1),
        'l0_subln': gain(2 * DIFF_HEAD_DIM),
        'l0_q_norm': gain(GQA_HEAD_DIM),
        'l0_k_norm': gain(GQA_HEAD_DIM),
        'l0_w_out': nrm((MIX_WIDTH, D_MODEL), MIX_WIDTH ** -0.5),
        'l0_norm_ffn': gain(D_MODEL),
        'l0_w_gate_up': nrm((D_MODEL, 2 * D_FF), fan),
        'l0_w_down': nrm((D_FF, D_MODEL), D_FF ** -0.5),
        'l1_ada_w': nrm((D_MODEL, N_MOD * D_MODEL), 0.5 * fan),
        'l1_ada_b': nrm((N_MOD * D_MODEL,), 0.02),
        'l1_norm_mix': gain(D_MODEL),
        'l1_w_out': nrm((D_MODEL, D_MODEL), fan),
        'l1_norm_ffn': gain(D_MODEL),
        'l1_w_gate_up': nrm((D_MODEL, 2 * D_FF), fan),
        'l1_w_down': nrm((D_FF, D_MODEL), D_FF ** -0.5),
        'final_norm': gain(D_MODEL),
    }
    return inp


def reference(x, c, ctx, c_ctx,
              l0_ada_w, l0_ada_b, l0_norm_mix, l0_w_in, l0_lambda_q1, l0_lambda_k1, l0_lambda_q2, l0_lambda_k2,
              l0_subln, l0_q_norm, l0_k_norm, l0_w_out, l0_norm_ffn, l0_w_gate_up, l0_w_down,
              l1_ada_w, l1_ada_b, l1_norm_mix, l1_w_out, l1_norm_ffn, l1_w_gate_up, l1_w_down,
              final_norm):
    layers = [
        dict(ada_w=l0_ada_w, ada_b=l0_ada_b, norm_mix=l0_norm_mix, w_in=l0_w_in,
             lambda_q1=l0_lambda_q1, lambda_k1=l0_lambda_k1, lambda_q2=l0_lambda_q2, lambda_k2=l0_lambda_k2,
             subln=l0_subln, q_norm=l0_q_norm, k_norm=l0_k_norm, w_out=l0_w_out,
             norm_ffn=l0_norm_ffn, w_gate_up=l0_w_gate_up, w_down=l0_w_down),
        dict(ada_w=l1_ada_w, ada_b=l1_ada_b, norm_mix=l1_norm_mix, w_out=l1_w_out,
             norm_ffn=l1_norm_ffn, w_gate_up=l1_w_gate_up, w_down=l1_w_down),
    ]
    cos, sin = rope_tables(x.shape[1])
    for l in range(DEPTH):
        p = layers[l]
        mod = ada_params(c, p['ada_w'], p['ada_b'])
        if l % 2 == 0:
            mod_c = ada_params(c_ctx[None], p['ada_w'], p['ada_b'])
            x = attention_mixer(x, ctx, cos, sin, mod[0:3], (mod_c[0], mod_c[1]), p, l)
        else:
            x = fourier_mixer(x, mod[0:3], p)
        x = swiglu_ffn(x, mod[3:6], p)
    return rms_norm(x, final_norm)
```

```python
import functools
import math

import numpy as np
import jax
import jax.numpy as jnp
from jax import lax
from jax.experimental import pallas as pl
from jax.experimental.pallas import tpu as pltpu

F32 = jnp.float32
BF16 = jnp.bfloat16
HIGHEST = lax.Precision.HIGHEST

LANES = 128
V7X_VMEM_BYTES = 64 * 1024 * 1024

D_MODEL = 1024
GRID_W = 64
DIFF_HEADS = 4
DIFF_HEAD_DIM = 64
GQA_HEADS = 8
GQA_KV_HEADS = 2
GQA_HEAD_DIM = 64
ROPE_THETA = 10000.0
ROPE_PAIRS = GQA_HEAD_DIM // 4
FOURIER_GROUPS = 8
FOURIER_GROUP_DIM = D_MODEL // FOURIER_GROUPS
D_FF = -(-8 * D_MODEL // (3 * 256)) * 256
EPS = 1e-6
N_MOD = 6
HEAD_DIM = 64
A_W = DIFF_HEADS * 2 * DIFF_HEAD_DIM
BQ_W = GQA_HEADS * GQA_HEAD_DIM
BKV_W = GQA_KV_HEADS * GQA_HEAD_DIM
Q_W = A_W + BQ_W
K_W = A_W + BKV_W
N_QG = Q_W // LANES
N_KG = K_W // LANES
LAM_INIT_L0 = 0.8 - 0.6 * math.exp(-0.3 * 0)
Q_SCALE = HEAD_DIM ** -0.5 * math.log2(math.e)
RADIX = 4


def _vmem_limit(nbytes):
    return int(min(nbytes, V7X_VMEM_BYTES - 4 * 1024 * 1024))


def _rms(x, g):
    ms = jnp.mean(x * x, axis=-1, keepdims=True)
    return x * lax.rsqrt(ms + EPS) * g


def _const_spec(shape):
    nd = len(shape)
    return pl.BlockSpec(shape, lambda *_: (0,) * nd, pipeline_mode=pl.Buffered(1))


def _fold_kernel(cd_ref, sd_ref, w_ref, o_ref):
    w = w_ref[...]
    d = w.shape[1]
    o_ref[:, 0:d] = jnp.dot(cd_ref[...], w, preferred_element_type=F32, precision=HIGHEST).astype(BF16)
    o_ref[:, d:2 * d] = jnp.dot(sd_ref[...], w, preferred_element_type=F32, precision=HIGHEST).astype(BF16)


def _fold_channel_dft(w_out):
    d = w_out.shape[0]
    gd = FOURIER_GROUP_DIM
    idx = np.arange(gd)
    ang = 2.0 * np.pi * ((idx[:, None] * idx[None, :]) % gd) / gd
    cd = jnp.asarray(np.cos(ang) / np.sqrt(gd), F32)
    sd = jnp.asarray(np.sin(ang) / np.sqrt(gd), F32)
    return pl.pallas_call(
        _fold_kernel,
        grid=(d // gd,),
        in_specs=[pl.BlockSpec((gd, gd), lambda g: (0, 0)),
                  pl.BlockSpec((gd, gd), lambda g: (0, 0)),
                  pl.BlockSpec((gd, d), lambda g: (g, 0))],
        out_specs=pl.BlockSpec((gd, 2 * d), lambda g: (g, 0)),
        out_shape=jax.ShapeDtypeStruct((d, 2 * d), BF16),
        name="fold",
    )(cd, sd, w_out)


def _ada_kernel(c_ref, w0_ref, b0_ref, w1_ref, b1_ref, o0_ref, o1_ref):
    cv = c_ref[...]
    a = cv * jax.nn.sigmoid(cv)
    o0_ref[...] = jnp.dot(a, w0_ref[...], preferred_element_type=F32, precision=HIGHEST) + b0_ref[...]
    o1_ref[...] = jnp.dot(a, w1_ref[...], preferred_element_type=F32, precision=HIGHEST) + b1_ref[...]


def _ada(cond, w0, b0, w1, b1):
    r, d = cond.shape
    n = w0.shape[1]
    bn = 512
    return pl.pallas_call(
        _ada_kernel,
        grid=(n // bn,),
        in_specs=[pl.BlockSpec((r, d), lambda j: (0, 0)),
                  pl.BlockSpec((d, bn), lambda j: (0, j)),
                  pl.BlockSpec((1, bn), lambda j: (0, j)),
                  pl.BlockSpec((d, bn), lambda j: (0, j)),
                  pl.BlockSpec((1, bn), lambda j: (0, j))],
        out_specs=[pl.BlockSpec((r, bn), lambda j: (0, j)),
                   pl.BlockSpec((r, bn), lambda j: (0, j))],
        out_shape=[jax.ShapeDtypeStruct((r, n), F32), jax.ShapeDtypeStruct((r, n), F32)],
        name="ada",
    )(cond, w0, b0.reshape(1, n), w1, b1.reshape(1, n))


def _proj_kernel(nct, x_ref, ctx_ref, mod_ref, modc_ref, g_ref, w_ref, cos_ref, s1_ref, s2_ref,
                 qg_ref, kg_ref, ones_ref, q_ref, kt_ref, v_ref):
    j = pl.program_id(1)
    is_ctx = j < nct
    xin = jnp.where(is_ctx, ctx_ref[0], x_ref[0])
    shift = jnp.where(is_ctx, modc_ref[0, 0:1, :], mod_ref[0, 0:1, :])
    scale = jnp.where(is_ctx, modc_ref[0, 1:2, :], mod_ref[0, 1:2, :])
    h = _rms(xin, g_ref[...]) * (1.0 + scale) + shift
    proj = jnp.dot(h.astype(BF16), w_ref[...], preferred_element_type=F32)

    cosv = cos_ref[...]
    s1 = s1_ref[...]
    s2 = s2_ref[...]
    lane = lax.broadcasted_iota(jnp.int32, cosv.shape, 1)
    lo_mask = lane < HEAD_DIM

    def rope(xs):
        return xs * cosv + pltpu.roll(xs, LANES - ROPE_PAIRS, 1) * s1 + pltpu.roll(xs, ROPE_PAIRS, 1) * s2

    def headnorm(xs, gain):
        sq = xs * xs
        hi = sq.astype(BF16)
        lo = (sq - hi.astype(F32)).astype(BF16)
        ss = (jnp.dot(hi, ones_ref[...], preferred_element_type=F32)
              + jnp.dot(lo, ones_ref[...], preferred_element_type=F32))
        return xs * lax.rsqrt(ss * (1.0 / HEAD_DIM) + EPS) * gain

    for s in range(N_QG):
        xs = proj[:, s * LANES:(s + 1) * LANES]
        if s >= A_W // LANES:
            xs = headnorm(xs, qg_ref[...])
        qs = rope(xs) * Q_SCALE
        q_lo = jnp.where(lo_mask, qs, 0.0)
        q_ref[0, s, 0] = q_lo.astype(BF16)
        q_ref[0, s, 1] = (qs - q_lo).astype(BF16)
    for s in range(N_KG):
        xs = proj[:, Q_W + s * LANES:Q_W + (s + 1) * LANES]
        if s >= A_W // LANES:
            xs = headnorm(xs, kg_ref[...])
        kt_ref[0, s] = rope(xs).T.astype(BF16)
    for s in range(N_KG):
        v_ref[0, s] = proj[:, Q_W + K_W + s * LANES:Q_W + K_W + (s + 1) * LANES].astype(BF16)


def _proj(x, ctx, mod, modc, g, w_perm, cos_t, s1_t, s2_t, qg, kg, ones_bd, tt):
    b, s, d = x.shape
    nctx = ctx.shape[1]
    nct = nctx // tt
    nk = nctx + s
    nj = nk // tt
    n_in = w_perm.shape[1]
    kern = functools.partial(_proj_kernel, nct)
    return pl.pallas_call(
        kern,
        grid=(b, nj),
        in_specs=[pl.BlockSpec((1, tt, d), lambda bi, j: (bi, jnp.maximum(j - nct, 0), 0)),
                  pl.BlockSpec((1, tt, d), lambda bi, j: (bi, jnp.minimum(j, nct - 1), 0)),
                  pl.BlockSpec((1, N_MOD, d), lambda bi, j: (bi, 0, 0)),
                  _const_spec((1, N_MOD, d)),
                  _const_spec((1, d)),
                  _const_spec((d, n_in)),
                  pl.BlockSpec((tt, LANES), lambda bi, j: (j, 0)),
                  pl.BlockSpec((tt, LANES), lambda bi, j: (j, 0)),
                  pl.BlockSpec((tt, LANES), lambda bi, j: (j, 0)),
                  _const_spec((1, LANES)),
                  _const_spec((1, LANES)),
                  _const_spec((LANES, LANES))],
        out_specs=[pl.BlockSpec((1, N_QG, 2, tt, LANES), lambda bi, j: (bi, 0, 0, jnp.maximum(j - nct, 0), 0)),
                   pl.BlockSpec((1, N_KG, LANES, tt), lambda bi, j: (bi, 0, 0, j)),
                   pl.BlockSpec((1, N_KG, tt, LANES), lambda bi, j: (bi, 0, j, 0))],
        out_shape=[jax.ShapeDtypeStruct((b, N_QG, 2, s, LANES), BF16),
                   jax.ShapeDtypeStruct((b, N_KG, LANES, nk), BF16),
                   jax.ShapeDtypeStruct((b, N_KG, nk, LANES), BF16)],
        compiler_params=pltpu.CompilerParams(
            dimension_semantics=("parallel", "arbitrary"),
            vmem_limit_bytes=_vmem_limit(40 * 1024 * 1024)),
        name="proj",
    )(x, ctx, mod, modc, g, w_perm, cos_t, s1_t, s2_t, qg, kg, ones_bd)


def _attn_kernel(tq, q_ref, kt_ref, v_ref, lam_ref, subln_ref, o_ref):
    lane = lax.broadcasted_iota(jnp.int32, (tq, LANES), 1)
    lo_mask = lane < HEAD_DIM
    lv = lam_ref[...]
    lam = (jnp.exp(jnp.sum(lv[0:1] * lv[1:2], axis=1, keepdims=True))
           - jnp.exp(jnp.sum(lv[2:3] * lv[3:4], axis=1, keepdims=True)) + LAM_INIT_L0)

    def unit(u, kv):
        ql = q_ref[0, u].reshape(2 * tq, LANES)
        s = jnp.dot(ql, kt_ref[0, kv], preferred_element_type=F32)
        m = jnp.max(s, axis=1, keepdims=True)
        p = jnp.exp2(s - m)
        l = jnp.sum(p, axis=1, keepdims=True)
        acc = jnp.dot(p.astype(BF16), v_ref[0, kv], preferred_element_type=F32)
        o = acc * (1.0 / l)
        return o[:tq], o[tq:]

    n_a = A_W // LANES
    for u in range(n_a):
        o1, o2 = unit(u, u)
        o = o1 - lam * o2
        o = _rms(o, subln_ref[...]) * (1.0 - LAM_INIT_L0)
        o_ref[0, :, u * LANES:(u + 1) * LANES] = o.astype(BF16)
    for u in range(n_a, N_QG):
        o1, o2 = unit(u, n_a)
        o_ref[0, :, u * LANES:(u + 1) * LANES] = jnp.where(lo_mask, o1, o2).astype(BF16)


def _attn(q, kt, v, lam_vecs, subln, tq):
    b, _, _, s, _ = q.shape
    nk = kt.shape[-1]
    kern = functools.partial(_attn_kernel, tq)
    return pl.pallas_call(
        kern,
        grid=(b, s // tq),
        in_specs=[pl.BlockSpec((1, N_QG, 2, tq, LANES), lambda bi, i: (bi, 0, 0, i, 0)),
                  pl.BlockSpec((1, N_KG, LANES, nk), lambda bi, i: (bi, 0, 0, 0)),
                  pl.BlockSpec((1, N_KG, nk, LANES), lambda bi, i: (bi, 0, 0, 0)),
                  _const_spec((4, HEAD_DIM)),
                  _const_spec((1, LANES))],
        out_specs=pl.BlockSpec((1, tq, Q_W), lambda bi, i: (bi, i, 0)),
        out_shape=jax.ShapeDtypeStruct((b, s, Q_W), BF16),
        compiler_params=pltpu.CompilerParams(
            dimension_semantics=("parallel", "arbitrary"),
            vmem_limit_bytes=_vmem_limit(48 * 1024 * 1024)),
        name="attn",
    )(q, kt, v, lam_vecs, subln)


def _swiglu(x1, mod_ref, gffn_ref, wgu_ref, wd_ref):
    shift = mod_ref[0, 3:4, :]
    scale = mod_ref[0, 4:5, :]
    gate = mod_ref[0, 5:6, :]
    h = _rms(x1, gffn_ref[...]) * (1.0 + scale) + shift
    gu = jnp.dot(h.astype(BF16), wgu_ref[...], preferred_element_type=F32)
    dff = gu.shape[1] // 2
    g = gu[:, :dff]
    u = gu[:, dff:]
    a = (g * jax.nn.sigmoid(g)) * u
    dn = jnp.dot(a.astype(BF16), wd_ref[...], preferred_element_type=F32)
    return x1 + gate * dn


def _mixffn_kernel(o_ref, x_ref, mod_ref, wout_ref, gffn_ref, wgu_ref, wd_ref, out_ref):
    y = jnp.dot(o_ref[0], wout_ref[...], preferred_element_type=F32)
    x1 = x_ref[0] + mod_ref[0, 2:3, :] * y
    out_ref[0] = _swiglu(x1, mod_ref, gffn_ref, wgu_ref, wd_ref)


def _ffn_final_kernel(x_ref, mod_ref, gffn_ref, wgu_ref, wd_ref, gfin_ref, out_ref):
    x2 = _swiglu(x_ref[0], mod_ref, gffn_ref, wgu_ref, wd_ref)
    out_ref[0] = _rms(x2, gfin_ref[...])


def _mixffn(o, x, mod, wout, gffn, wgu, wd, tm):
    b, s, d = x.shape
    tok = lambda bi, i: (bi, i, 0)
    return pl.pallas_call(
        _mixffn_kernel,
        grid=(b, s // tm),
        in_specs=[pl.BlockSpec((1, tm, o.shape[2]), tok),
                  pl.BlockSpec((1, tm, d), tok),
                  pl.BlockSpec((1, N_MOD, d), lambda bi, i: (bi, 0, 0)),
                  _const_spec(wout.shape), _const_spec((1, d)), _const_spec(wgu.shape), _const_spec(wd.shape)],
        out_specs=pl.BlockSpec((1, tm, d), tok),
        out_shape=jax.ShapeDtypeStruct((b, s, d), F32),
        compiler_params=pltpu.CompilerParams(
            dimension_semantics=("parallel", "arbitrary"),
            vmem_limit_bytes=_vmem_limit(56 * 1024 * 1024)),
        name="mixffn",
    )(o, x, mod, wout, gffn, wgu, wd)


def _ffn_final(x, mod, gffn, wgu, wd, gfin, tm):
    b, s, d = x.shape
    tok = lambda bi, i: (bi, i, 0)
    return pl.pallas_call(
        _ffn_final_kernel,
        grid=(b, s // tm),
        in_specs=[pl.BlockSpec((1, tm, d), tok),
                  pl.BlockSpec((1, N_MOD, d), lambda bi, i: (bi, 0, 0)),
                  _const_spec((1, d)), _const_spec(wgu.shape), _const_spec(wd.shape), _const_spec((1, d))],
        out_specs=pl.BlockSpec((1, tm, d), tok),
        out_shape=jax.ShapeDtypeStruct((b, s, d), F32),
        compiler_params=pltpu.CompilerParams(
            dimension_semantics=("parallel", "arbitrary"),
            vmem_limit_bytes=_vmem_limit(56 * 1024 * 1024)),
        name="ffn",
    )(x, mod, gffn, wgu, wd, gfin)


def _fproj_kernel(x_ref, mod_ref, g_ref, w_ref, z_ref):
    h = _rms(x_ref[0], g_ref[...]) * (1.0 + mod_ref[0, 1:2, :]) + mod_ref[0, 0:1, :]
    z_ref[0] = jnp.dot(h.astype(BF16), w_ref[...], preferred_element_type=F32).astype(BF16)


def _fproj(x, mod, g, wcs, tm):
    b, s, d = x.shape
    tok = lambda bi, i: (bi, i, 0)
    return pl.pallas_call(
        _fproj_kernel,
        grid=(b, s // tm),
        in_specs=[pl.BlockSpec((1, tm, d), tok),
                  pl.BlockSpec((1, N_MOD, d), lambda bi, i: (bi, 0, 0)),
                  _const_spec((1, d)), _const_spec(wcs.shape)],
        out_specs=pl.BlockSpec((1, tm, 2 * d), tok),
        out_shape=jax.ShapeDtypeStruct((b, s, 2 * d), BF16),
        compiler_params=pltpu.CompilerParams(
            dimension_semantics=("parallel", "arbitrary"),
            vmem_limit_bytes=_vmem_limit(32 * 1024 * 1024)),
        name="fproj",
    )(x, mod, g, wcs)


_QUARTER = ((1, 0), (0, 1), (-1, 0), (0, -1))


def _fseq_kernel(nq, rows, z_ref, t_ref, x_ref, mod_ref, out_ref, u_scr):
    r = pl.program_id(1)
    i = pl.program_id(2)
    d = x_ref.shape[2]

    @pl.when(i == 0)
    def _():
        for rr in range(RADIX):
            @pl.when(r == rr)
            def _():
                for c0 in range(0, nq, rows):
                    ua = None
                    ub = None
                    for m in range(RADIX):
                        cp, sp = _QUARTER[(rr * m) % 4]
                        zc = z_ref[0, m * nq + c0:m * nq + c0 + rows, 0:d].astype(F32)
                        zs = z_ref[0, m * nq + c0:m * nq + c0 + rows, d:2 * d].astype(F32)
                        ta, sa = (zc, cp) if cp else (zs, -sp)
                        tb, sb = (zs, cp) if cp else (zc, sp)
                        ua = sa * ta if ua is None else (ua + ta if sa > 0 else ua - ta)
                        ub = sb * tb if ub is None else (ub + tb if sb > 0 else ub - tb)
                    u_scr[c0:c0 + rows, :] = ua.astype(BF16)
                    u_scr[nq + c0:nq + c0 + rows, :] = ub.astype(BF16)

    y = jnp.dot(t_ref[0], u_scr[...], preferred_element_type=F32)
    out_ref[0] = x_ref[0] + mod_ref[0, 2:3, :] * y


def _seq_dft_tables(n):
    nq = n // RADIX
    p = np.arange(nq)[None, :, None]
    q = np.arange(nq)[None, None, :]
    r = np.arange(RADIX)[:, None, None]
    ang = 2.0 * np.pi * (((RADIX * p + r) * q) % n) / n
    t = np.concatenate([np.cos(ang), -np.sin(ang)], axis=2) / np.sqrt(n)
    return jnp.asarray(t, F32)


def _fseq(z, t, x, mod, tmr):
    b, s, d = x.shape
    nq = s // RADIX
    rows = min(128, nq)
    kern = functools.partial(_fseq_kernel, nq, rows)
    xv = x.reshape(b, nq, RADIX * d)
    out = pl.pallas_call(
        kern,
        grid=(b, RADIX, nq // tmr),
        in_specs=[pl.BlockSpec((1, s, 2 * d), lambda bi, r, i: (bi, 0, 0)),
                  pl.BlockSpec((1, tmr, 2 * nq), lambda bi, r, i: (r, i, 0)),
                  pl.BlockSpec((1, tmr, d), lambda bi, r, i: (bi, i, r)),
                  pl.BlockSpec((1, N_MOD, d), lambda bi, r, i: (bi, 0, 0))],
        out_specs=pl.BlockSpec((1, tmr, d), lambda bi, r, i: (bi, i, r)),
        out_shape=jax.ShapeDtypeStruct((b, nq, RADIX * d), F32),
        scratch_shapes=[pltpu.VMEM((2 * nq, d), BF16)],
        compiler_params=pltpu.CompilerParams(
            dimension_semantics=("parallel", "arbitrary", "arbitrary"),
            vmem_limit_bytes=_vmem_limit(48 * 1024 * 1024)),
        name="fseq",
    )(z, t, xv, mod)
    return out.reshape(b, s, d)


def _rope_lane_tables(n_ctx, n_tok):
    rows_count = n_tok // GRID_W
    row = jnp.repeat(jnp.arange(rows_count, dtype=jnp.int32), GRID_W).astype(F32)
    col = jnp.tile(jnp.arange(GRID_W, dtype=jnp.int32), rows_count).astype(F32)
    inv = ROPE_THETA ** (-jnp.arange(ROPE_PAIRS, dtype=F32) / ROPE_PAIRS)
    lane = np.arange(LANES)
    dd = lane % HEAD_DIM
    axis = dd // (2 * ROPE_PAIRS)
    half = (dd % (2 * ROPE_PAIRS)) // ROPE_PAIRS
    pair = dd % ROPE_PAIRS
    pos = jnp.where(jnp.asarray(axis == 0)[None, :], row[:, None], col[:, None])
    ang = pos * inv[jnp.asarray(pair)][None, :]
    cosv = jnp.cos(ang)
    sinv = jnp.sin(ang)
    first = jnp.asarray(half == 0)[None, :]
    s1 = jnp.where(first, -sinv, 0.0)
    s2 = jnp.where(first, 0.0, sinv)
    pad1 = jnp.ones((n_ctx, LANES), F32)
    pad0 = jnp.zeros((n_ctx, LANES), F32)
    return (jnp.concatenate([pad1, cosv], 0), jnp.concatenate([pad0, s1], 0), jnp.concatenate([pad0, s2], 0))


def _permute_w_in(w_in):
    off_ak = A_W
    off_av = 2 * A_W
    off_bq = 3 * A_W
    off_bk = off_bq + BQ_W
    off_bv = off_bk + BKV_W
    grp = GQA_HEADS // GQA_KV_HEADS
    bq_cols = []
    for p in range(grp):
        for kvh in range(GQA_KV_HEADS):
            hq = kvh * grp + p
            bq_cols.append(np.arange(off_bq + hq * HEAD_DIM, off_bq + (hq + 1) * HEAD_DIM))
    cols = np.concatenate([np.arange(0, A_W)] + bq_cols
                          + [np.arange(off_ak, off_av), np.arange(off_bk, off_bv),
                             np.arange(off_av, off_bq), np.arange(off_bv, off_bv + BKV_W)])
    return w_in[:, cols], np.concatenate(bq_cols) - off_bq


def kernel(x, c, ctx, c_ctx, l0_ada_w, l0_ada_b, l0_norm_mix, l0_w_in, l0_lambda_q1, l0_lambda_k1, l0_lambda_q2, l0_lambda_k2, l0_subln, l0_q_norm, l0_k_norm, l0_w_out, l0_norm_ffn, l0_w_gate_up, l0_w_down, l1_ada_w, l1_ada_b, l1_norm_mix, l1_w_out, l1_norm_ffn, l1_w_gate_up, l1_w_down, final_norm):
    b, s, d = x.shape
    n_ctx = ctx.shape[1]
    assert d == D_MODEL and s % GRID_W == 0

    tt = min(256, n_ctx, s)
    tq = min(128, s)
    tm = min(256, s)
    tmr = min(256, s // RADIX)
    assert n_ctx % tt == 0 and s % tt == 0 and s % tq == 0 and s % tm == 0 and (s // RADIX) % tmr == 0

    n_rows = -(-(b + 1) // 8) * 8
    cond = jnp.concatenate([c, c_ctx[None, :], jnp.zeros((n_rows - b - 1, d), F32)], axis=0)
    m0, m1 = _ada(cond, l0_ada_w, l0_ada_b, l1_ada_w, l1_ada_b)
    mod0 = m0[:b].reshape(b, N_MOD, d)
    mod0c = m0[b:b + 1].reshape(1, N_MOD, d)
    mod1 = m1[:b].reshape(b, N_MOD, d)

    w_perm, bq_perm = _permute_w_in(l0_w_in)
    cos_t, s1_t, s2_t = _rope_lane_tables(n_ctx, s)
    qg = jnp.tile(l0_q_norm, LANES // HEAD_DIM).reshape(1, LANES)
    kg = jnp.tile(l0_k_norm, LANES // HEAD_DIM).reshape(1, LANES)
    li = np.arange(LANES)
    ones_bd = jnp.asarray((li[:, None] // HEAD_DIM) == (li[None, :] // HEAD_DIM), BF16)
    q, kt, v = _proj(x, ctx, mod0, mod0c, l0_norm_mix.reshape(1, d), w_perm.astype(BF16),
                     cos_t, s1_t, s2_t, qg, kg, ones_bd, tt)
    lam_vecs = jnp.stack([l0_lambda_q1, l0_lambda_k1, l0_lambda_q2, l0_lambda_k2]).astype(F32)
    o = _attn(q, kt, v, lam_vecs, l0_subln.reshape(1, LANES), tq)
    w_out_rows = np.concatenate([np.arange(A_W), A_W + bq_perm])
    x = _mixffn(o, x, mod0, l0_w_out[w_out_rows].astype(BF16), l0_norm_ffn.reshape(1, d),
                l0_w_gate_up.astype(BF16), l0_w_down.astype(BF16), tm)

    wcs = _fold_channel_dft(l1_w_out)
    z = _fproj(x, mod1, l1_norm_mix.reshape(1, d), wcs, tm)
    x = _fseq(z, _seq_dft_tables(s).astype(BF16), x, mod1, tmr)
    return _ffn_final(x, mod1, l1_norm_ffn.reshape(1, d), l1_w_gate_up.astype(BF16),
                      l1_w_down.astype(BF16), final_norm.reshape(1, d), tm)
```

```python
import functools
import math

import numpy as np
import jax
import jax.numpy as jnp
from jax import lax
from jax.experimental import pallas as pl
from jax.experimental.pallas import tpu as pltpu

F32 = jnp.float32
BF16 = jnp.bfloat16
HIGHEST = lax.Precision.HIGHEST

LANES = 128
SUBLANES = 8
V7X_VMEM_BYTES = 64 * 1024 * 1024

D_MODEL = 1024
GRID_W = 64
DIFF_HEADS = 4
DIFF_HEAD_DIM = 64
GQA_HEADS = 8
GQA_KV_HEADS = 2
GQA_HEAD_DIM = 64
ROPE_THETA = 10000.0
ROPE_PAIRS = GQA_HEAD_DIM // 4
FOURIER_GROUPS = 8
FOURIER_GROUP_DIM = D_MODEL // FOURIER_GROUPS
EPS = 1e-6
N_MOD = 6
HEAD_DIM = 64
A_W = DIFF_HEADS * 2 * DIFF_HEAD_DIM
BQ_W = GQA_HEADS * GQA_HEAD_DIM
BKV_W = GQA_KV_HEADS * GQA_HEAD_DIM
Q_W = A_W + BQ_W
K_W = A_W + BKV_W
N_QG = Q_W // LANES
N_KG = K_W // LANES
LAM_INIT_L0 = 0.8 - 0.6 * math.exp(-0.3 * 0)
Q_SCALE = HEAD_DIM ** -0.5 * math.log2(math.e)
RADIX = 4
ACC_ROWS = 32
EXP_AHEAD = 4
VT_ROWS = LANES + 16


def _vmem_limit(nbytes):
    return int(min(nbytes, V7X_VMEM_BYTES - 4 * 1024 * 1024))


def _rms(x, g):
    ms = jnp.mean(x * x, axis=-1, keepdims=True)
    return x * lax.rsqrt(ms + EPS) * g


def _const_spec(shape):
    nd = len(shape)
    return pl.BlockSpec(shape, lambda *_: (0,) * nd, pipeline_mode=pl.Buffered(1))


def _fold_kernel(cd_ref, sd_ref, w_ref, o_ref):
    w = w_ref[...]
    d = w.shape[1]
    o_ref[:, 0:d] = jnp.dot(cd_ref[...], w, preferred_element_type=F32, precision=HIGHEST).astype(BF16)
    o_ref[:, d:2 * d] = jnp.dot(sd_ref[...], w, preferred_element_type=F32, precision=HIGHEST).astype(BF16)


def _fold_channel_dft(w_out):
    d = w_out.shape[0]
    gd = FOURIER_GROUP_DIM
    idx = np.arange(gd)
    ang = 2.0 * np.pi * ((idx[:, None] * idx[None, :]) % gd) / gd
    cd = jnp.asarray(np.cos(ang) / np.sqrt(gd), F32)
    sd = jnp.asarray(np.sin(ang) / np.sqrt(gd), F32)
    return pl.pallas_call(
        _fold_kernel,
        grid=(d // gd,),
        in_specs=[pl.BlockSpec((gd, gd), lambda g: (0, 0)),
                  pl.BlockSpec((gd, gd), lambda g: (0, 0)),
                  pl.BlockSpec((gd, d), lambda g: (g, 0))],
        out_specs=pl.BlockSpec((gd, 2 * d), lambda g: (g, 0)),
        out_shape=jax.ShapeDtypeStruct((d, 2 * d), BF16),
        name="fold",
    )(cd, sd, w_out)


def _ada_kernel(c_ref, w0_ref, b0_ref, w1_ref, b1_ref, o0_ref, o1_ref):
    cv = c_ref[...]
    a = cv * jax.nn.sigmoid(cv)
    o0_ref[...] = jnp.dot(a, w0_ref[...], preferred_element_type=F32, precision=HIGHEST) + b0_ref[...]
    o1_ref[...] = jnp.dot(a, w1_ref[...], preferred_element_type=F32, precision=HIGHEST) + b1_ref[...]


def _ada(cond, w0, b0, w1, b1):
    r, d = cond.shape
    n = w0.shape[1]
    bn = 512
    return pl.pallas_call(
        _ada_kernel,
        grid=(n // bn,),
        in_specs=[pl.BlockSpec((r, d), lambda j: (0, 0)),
                  pl.BlockSpec((d, bn), lambda j: (0, j)),
                  pl.BlockSpec((1, bn), lambda j: (0, j)),
                  pl.BlockSpec((d, bn), lambda j: (0, j)),
                  pl.BlockSpec((1, bn), lambda j: (0, j))],
        out_specs=[pl.BlockSpec((r, bn), lambda j: (0, j)),
                   pl.BlockSpec((r, bn), lambda j: (0, j))],
        out_shape=[jax.ShapeDtypeStruct((r, n), F32), jax.ShapeDtypeStruct((r, n), F32)],
        name="ada",
    )(cond, w0, b0.reshape(1, n), w1, b1.reshape(1, n))


def _proj_kernel(nct, tq, x_ref, ctx_ref, mod_ref, modc_ref, g_ref, w_ref, cos_ref, s1_ref, s2_ref,
                 qg_ref, kg_ref, ones_ref, qt_ref, k_ref, vt_ref):
    j = pl.program_id(1)
    is_ctx = j < nct
    xin = jnp.where(is_ctx, ctx_ref[0], x_ref[0])
    shift = jnp.where(is_ctx, modc_ref[0, 0:1, :], mod_ref[0, 0:1, :])
    scale = jnp.where(is_ctx, modc_ref[0, 1:2, :], mod_ref[0, 1:2, :])
    h = _rms(xin, g_ref[...]) * (1.0 + scale) + shift
    proj = jnp.dot(h.astype(BF16), w_ref[...], preferred_element_type=F32)

    tt = xin.shape[0]
    cosv = cos_ref[...]
    s1 = s1_ref[...]
    s2 = s2_ref[...]
    row = lax.broadcasted_iota(jnp.int32, (LANES, tt), 0)
    lo_rows = row < HEAD_DIM

    def rope(xs):
        return xs * cosv + pltpu.roll(xs, LANES - ROPE_PAIRS, 1) * s1 + pltpu.roll(xs, ROPE_PAIRS, 1) * s2

    def headnorm(xs, gain):
        sq = xs * xs
        hi = sq.astype(BF16)
        lo = (sq - hi.astype(F32)).astype(BF16)
        ss = (jnp.dot(hi, ones_ref[...], preferred_element_type=F32)
              + jnp.dot(lo, ones_ref[...], preferred_element_type=F32))
        return xs * lax.rsqrt(ss * (1.0 / HEAD_DIM) + EPS) * gain

    for s in range(N_QG):
        xs = proj[:, s * LANES:(s + 1) * LANES]
        if s >= A_W // LANES:
            xs = headnorm(xs, qg_ref[...])
        qst = (rope(xs) * Q_SCALE).T
        q_lo = jnp.where(lo_rows, qst, 0.0)
        q_hi = qst - q_lo
        for a in range(tt // tq):
            qt_ref[0, s, :, 2 * a * tq:(2 * a + 1) * tq] = q_lo[:, a * tq:(a + 1) * tq].astype(BF16)
            qt_ref[0, s, :, (2 * a + 1) * tq:(2 * a + 2) * tq] = q_hi[:, a * tq:(a + 1) * tq].astype(BF16)
    for s in range(N_KG):
        xs = proj[:, Q_W + s * LANES:Q_W + (s + 1) * LANES]
        if s >= A_W // LANES:
            xs = headnorm(xs, kg_ref[...])
        k_ref[0, s] = rope(xs).astype(BF16)
    for s in range(N_KG):
        vt_ref[0, s, 0:LANES, :] = proj[:, Q_W + K_W + s * LANES:Q_W + K_W + (s + 1) * LANES].T.astype(BF16)
        vt_ref[0, s, LANES:VT_ROWS, :] = jnp.ones((VT_ROWS - LANES, tt), BF16)


def _proj(x, ctx, mod, modc, g, w_perm, cos_t, s1_t, s2_t, qg, kg, ones_bd, tt, tq):
    b, s, d = x.shape
    nctx = ctx.shape[1]
    nct = nctx // tt
    nk = nctx + s
    nj = nk // tt
    n_in = w_perm.shape[1]
    kern = functools.partial(_proj_kernel, nct, tq)
    return pl.pallas_call(
        kern,
        grid=(b, nj),
        in_specs=[pl.BlockSpec((1, tt, d), lambda bi, j: (bi, jnp.maximum(j - nct, 0), 0)),
                  pl.BlockSpec((1, tt, d), lambda bi, j: (bi, jnp.minimum(j, nct - 1), 0)),
                  pl.BlockSpec((1, N_MOD, d), lambda bi, j: (bi, 0, 0)),
                  _const_spec((1, N_MOD, d)),
                  _const_spec((1, d)),
                  _const_spec((d, n_in)),
                  pl.BlockSpec((tt, LANES), lambda bi, j: (j, 0)),
                  pl.BlockSpec((tt, LANES), lambda bi, j: (j, 0)),
                  pl.BlockSpec((tt, LANES), lambda bi, j: (j, 0)),
                  _const_spec((1, LANES)),
                  _const_spec((1, LANES)),
                  _const_spec((LANES, LANES))],
        out_specs=[pl.BlockSpec((1, N_QG, LANES, 2 * tt), lambda bi, j: (bi, 0, 0, jnp.maximum(j - nct, 0))),
                   pl.BlockSpec((1, N_KG, tt, LANES), lambda bi, j: (bi, 0, j, 0)),
                   pl.BlockSpec((1, N_KG, VT_ROWS, tt), lambda bi, j: (bi, 0, 0, j))],
        out_shape=[jax.ShapeDtypeStruct((b, N_QG, LANES, 2 * s), BF16),
                   jax.ShapeDtypeStruct((b, N_KG, nk, LANES), BF16),
                   jax.ShapeDtypeStruct((b, N_KG, VT_ROWS, nk), BF16)],
        compiler_params=pltpu.CompilerParams(
            dimension_semantics=("parallel", "arbitrary"),
            vmem_limit_bytes=_vmem_limit(40 * 1024 * 1024)),
        name="proj",
    )(x, ctx, mod, modc, g, w_perm, cos_t, s1_t, s2_t, qg, kg, ones_bd)


def _attn_kernel(tq, tk, qt_ref, k_ref, vt_ref, lam_ref, subln_ref, o_ref, *s_bufs):
    nk = k_ref.shape[2]
    nc = nk // tk
    tq2 = 2 * tq
    row = lax.broadcasted_iota(jnp.int32, (LANES, tq), 0)
    lo_rows = row < HEAD_DIM
    lv = lam_ref[...]
    lam = (jnp.exp(jnp.sum(lv[0:1] * lv[1:2], axis=1, keepdims=True))
           - jnp.exp(jnp.sum(lv[2:3] * lv[3:4], axis=1, keepdims=True)) + LAM_INIT_L0)

    n_a = A_W // LANES

    def score_chunk(u, c, m8):
        kv = min(u, n_a)
        s_c = jnp.dot(k_ref[0, kv, c * tk:(c + 1) * tk, :], qt_ref[0, u], preferred_element_type=F32)
        s_bufs[u % len(s_bufs)][c * tk:(c + 1) * tk, :] = s_c
        mc = jnp.max(s_c.reshape(tk // ACC_ROWS, ACC_ROWS, tq2), axis=0)
        return mc if m8 is None else jnp.maximum(m8, mc)

    def value_chunk(u, c, m, acc, gate):
        kv = min(u, n_a)
        if gate is not None:
            m = m + jnp.minimum(jnp.abs(gate[LANES:LANES + 1]), 0.0)
        p = jnp.exp2((s_bufs[u % len(s_bufs)][c * tk:(c + 1) * tk, :] - m).astype(BF16))
        pv = jnp.dot(vt_ref[0, kv, :, c * tk:(c + 1) * tk], p, preferred_element_type=F32)
        return pv if acc is None else acc + pv

    def finish(u, acc):
        ot = acc[0:LANES] * (1.0 / acc[LANES:LANES + 1])
        o1, o2 = ot[:, :tq], ot[:, tq:]
        if u < n_a:
            ot = o1 - lam * o2
            ot = ot * lax.rsqrt(jnp.mean(ot * ot, axis=0, keepdims=True) + EPS)
            o_ref[0, u] = (ot.T * (subln_ref[...] * (1.0 - LAM_INIT_L0))).astype(BF16)
        else:
            o_ref[0, u] = jnp.where(lo_rows, o1, o2).T.astype(BF16)

    lead = min(2, nc)
    m8 = None
    for c in range(nc):
        m8 = score_chunk(0, c, m8)
    for u in range(N_QG):
        m = jnp.max(m8, axis=0, keepdims=True)
        m8 = None
        accs = []
        nxt = u + 1 < N_QG
        if nxt:
            for c in range(lead):
                m8 = score_chunk(u + 1, c, m8)
        for c in range(nc):
            gate = accs[c - EXP_AHEAD] if c >= EXP_AHEAD else None
            accs.append(value_chunk(u, c, m, accs[-1] if accs else None, gate))
            if nxt and c + lead < nc:
                m8 = score_chunk(u + 1, c + lead, m8)
        finish(u, accs[-1])


def _attn(qt, k, vt, lam_vecs, subln, tq, tk):
    b, _, _, s2 = qt.shape
    s = s2 // 2
    nk = k.shape[2]
    kern = functools.partial(_attn_kernel, tq, tk)
    return pl.pallas_call(
        kern,
        grid=(b, s // tq),
        in_specs=[pl.BlockSpec((1, N_QG, LANES, 2 * tq), lambda bi, i: (bi, 0, 0, i)),
                  pl.BlockSpec((1, N_KG, nk, LANES), lambda bi, i: (bi, 0, 0, 0)),
                  pl.BlockSpec((1, N_KG, VT_ROWS, nk), lambda bi, i: (bi, 0, 0, 0)),
                  _const_spec((4, HEAD_DIM)),
                  _const_spec((1, LANES))],
        out_specs=pl.BlockSpec((1, N_QG, tq, LANES), lambda bi, i: (bi, 0, i, 0)),
        out_shape=jax.ShapeDtypeStruct((b, N_QG, s, LANES), BF16),
        scratch_shapes=[pltpu.VMEM((nk, 2 * tq), F32)] * 3,
        compiler_params=pltpu.CompilerParams(
            dimension_semantics=("parallel", "arbitrary"),
            vmem_limit_bytes=_vmem_limit(48 * 1024 * 1024)),
        name="attn",
    )(qt, k, vt, lam_vecs, subln)


def _swiglu(x1, mod_ref, gffn_ref, wgu_ref, wd_ref):
    shift = mod_ref[0, 3:4, :]
    scale = mod_ref[0, 4:5, :]
    gate = mod_ref[0, 5:6, :]
    h = _rms(x1, gffn_ref[...]) * (1.0 + scale) + shift
    gu = jnp.dot(h.astype(BF16), wgu_ref[...], preferred_element_type=F32)
    dff = gu.shape[1] // 2
    g = gu[:, :dff]
    u = gu[:, dff:]
    a = (g * jax.nn.sigmoid(g)) * u
    dn = jnp.dot(a.astype(BF16), wd_ref[...], preferred_element_type=F32)
    return x1 + gate * dn


def _mixffn_kernel(o_ref, x_ref, mod_ref, wout_ref, gffn_ref, wgu_ref, wd_ref, out_ref, res_scr):
    o = jnp.concatenate([o_ref[0, u] for u in range(N_QG)], axis=1)
    y = jnp.dot(o, wout_ref[...], preferred_element_type=F32)
    x1 = x_ref[0] + mod_ref[0, 2:3, :] * y
    res = _swiglu(x1, mod_ref, gffn_ref, wgu_ref, wd_ref)
    n = res_scr.shape[1] // RADIX
    for sl in range(res_scr.shape[0]):
        res_scr[sl] = res[:, sl * LANES:(sl + 1) * LANES]
        for r in range(RADIX):
            out_ref[0, r, :, sl * LANES:(sl + 1) * LANES] = res_scr[sl, pl.ds(r, n, stride=RADIX), :]


def _ffn_final_kernel(x_ref, mod_ref, gffn_ref, wgu_ref, wd_ref, gfin_ref, out_ref, x_scr):
    n = x_scr.shape[1] // RADIX
    for sl in range(x_scr.shape[0]):
        for r in range(RADIX):
            x_scr[sl, pl.ds(r, n, stride=RADIX), :] = x_ref[0, r, :, sl * LANES:(sl + 1) * LANES]
    x1 = jnp.concatenate([x_scr[sl] for sl in range(x_scr.shape[0])], axis=1)
    x2 = _swiglu(x1, mod_ref, gffn_ref, wgu_ref, wd_ref)
    out_ref[0] = _rms(x2, gfin_ref[...])


def _mixffn(o, x, mod, wout, gffn, wgu, wd, tm):
    b, s, d = x.shape
    tok = lambda bi, i: (bi, i, 0)
    return pl.pallas_call(
        _mixffn_kernel,
        grid=(b, s // tm),
        in_specs=[pl.BlockSpec((1, N_QG, tm, LANES), lambda bi, i: (bi, 0, i, 0)),
                  pl.BlockSpec((1, tm, d), tok),
                  pl.BlockSpec((1, N_MOD, d), lambda bi, i: (bi, 0, 0)),
                  _const_spec(wout.shape), _const_spec((1, d)), _const_spec(wgu.shape), _const_spec(wd.shape)],
        out_specs=pl.BlockSpec((1, RADIX, tm // RADIX, d), lambda bi, i: (bi, 0, i, 0)),
        out_shape=jax.ShapeDtypeStruct((b, RADIX, s // RADIX, d), F32),
        scratch_shapes=[pltpu.VMEM((d // LANES, tm, LANES), F32)],
        compiler_params=pltpu.CompilerParams(
            dimension_semantics=("parallel", "arbitrary"),
            vmem_limit_bytes=_vmem_limit(56 * 1024 * 1024)),
        name="mixffn",
    )(o, x, mod, wout, gffn, wgu, wd)


def _ffn_final(xr, mod, gffn, wgu, wd, gfin, tm):
    b, _, nq, d = xr.shape
    s = nq * RADIX
    return pl.pallas_call(
        _ffn_final_kernel,
        grid=(b, s // tm),
        in_specs=[pl.BlockSpec((1, RADIX, tm // RADIX, d), lambda bi, i: (bi, 0, i, 0)),
                  pl.BlockSpec((1, N_MOD, d), lambda bi, i: (bi, 0, 0)),
                  _const_spec((1, d)), _const_spec(wgu.shape), _const_spec(wd.shape), _const_spec((1, d))],
        out_specs=pl.BlockSpec((1, tm, d), lambda bi, i: (bi, i, 0)),
        out_shape=jax.ShapeDtypeStruct((b, s, d), F32),
        scratch_shapes=[pltpu.VMEM((d // LANES, tm, LANES), F32)],
        compiler_params=pltpu.CompilerParams(
            dimension_semantics=("parallel", "arbitrary"),
            vmem_limit_bytes=_vmem_limit(56 * 1024 * 1024)),
        name="ffn",
    )(xr, mod, gffn, wgu, wd, gfin)


def _fproj_kernel(x_ref, mod_ref, g_ref, w_ref, z_ref):
    h = _rms(x_ref[0], g_ref[...]) * (1.0 + mod_ref[0, 1:2, :]) + mod_ref[0, 0:1, :]
    z_ref[0] = jnp.dot(h.astype(BF16), w_ref[...], preferred_element_type=F32).astype(BF16)


def _fproj(x, mod, g, wcs, tm):
    b, s, d = x.shape
    tok = lambda bi, i: (bi, i, 0)
    return pl.pallas_call(
        _fproj_kernel,
        grid=(b, s // tm),
        in_specs=[pl.BlockSpec((1, tm, d), tok),
                  pl.BlockSpec((1, N_MOD, d), lambda bi, i: (bi, 0, 0)),
                  _const_spec((1, d)), _const_spec(wcs.shape)],
        out_specs=pl.BlockSpec((1, tm, 2 * d), tok),
        out_shape=jax.ShapeDtypeStruct((b, s, 2 * d), BF16),
        compiler_params=pltpu.CompilerParams(
            dimension_semantics=("parallel", "arbitrary"),
            vmem_limit_bytes=_vmem_limit(32 * 1024 * 1024)),
        name="fproj",
    )(x, mod, g, wcs)


_QUARTER = ((1, 0), (0, 1), (-1, 0), (0, -1))


def _fseq_kernel(nq, rows, z_ref, t_ref, x_ref, mod_ref, out_ref, u_scr):
    r = pl.program_id(1)
    i = pl.program_id(2)
    d = x_ref.shape[2]
    nqq = nq // RADIX

    @pl.when(i == 0)
    def _():
        for rr in range(RADIX):
            @pl.when(r == rr)
            def _():
                for e in range(RADIX):
                    for c0 in range(0, nqq, rows):
                        ua = None
                        ub = None
                        for m in range(RADIX):
                            cp, sp = _QUARTER[(rr * m) % 4]
                            lo = e * nq + m * nqq + c0
                            zc = z_ref[0, lo:lo + rows, 0:d].astype(F32)
                            zs = z_ref[0, lo:lo + rows, d:2 * d].astype(F32)
                            ta, sa = (zc, cp) if cp else (zs, -sp)
                            tb, sb = (zs, cp) if cp else (zc, sp)
                            ua = sa * ta if ua is None else (ua + ta if sa > 0 else ua - ta)
                            ub = sb * tb if ub is None else (ub + tb if sb > 0 else ub - tb)
                        uo = e * nqq + c0
                        u_scr[uo:uo + rows, :] = ua.astype(BF16)
                        u_scr[nq + uo:nq + uo + rows, :] = ub.astype(BF16)

    y = jnp.dot(t_ref[0], u_scr[...], preferred_element_type=F32)
    out_ref[0] = x_ref[0] + mod_ref[0, 2:3, :] * y


def _seq_dft_tables(n):
    nq = n // RADIX
    nqq = nq // RADIX
    u = np.arange(nq)
    q = (RADIX * (u % nqq) + u // nqq)[None, None, :]
    p = np.arange(nq)[None, :, None]
    r = np.arange(RADIX)[:, None, None]
    ang = 2.0 * np.pi * (((RADIX * p + r) * q) % n) / n
    t = np.concatenate([np.cos(ang), -np.sin(ang)], axis=2) / np.sqrt(n)
    return jnp.asarray(t, F32)


def _fseq(z, t, x, mod, tmr):
    b, s, d = x.shape
    nq = s // RADIX
    rows = min(128, nq // RADIX)
    nblk = nq // tmr
    kern = functools.partial(_fseq_kernel, nq, rows)
    return pl.pallas_call(
        kern,
        grid=(b, RADIX, nblk),
        in_specs=[pl.BlockSpec((1, s, 2 * d), lambda bi, r, i: (bi, 0, 0)),
                  pl.BlockSpec((1, tmr, 2 * nq), lambda bi, r, i: (r, i, 0)),
                  pl.BlockSpec((1, tmr, d), lambda bi, r, i: (bi, r * nblk + i, 0)),
                  pl.BlockSpec((1, N_MOD, d), lambda bi, r, i: (bi, 0, 0))],
        out_specs=pl.BlockSpec((1, tmr, d), lambda bi, r, i: (bi, r * nblk + i, 0)),
        out_shape=jax.ShapeDtypeStruct((b, s, d), F32),
        scratch_shapes=[pltpu.VMEM((2 * nq, d), BF16)],
        compiler_params=pltpu.CompilerParams(
            dimension_semantics=("parallel", "arbitrary", "arbitrary"),
            vmem_limit_bytes=_vmem_limit(48 * 1024 * 1024)),
        name="fseq",
    )(z, t, x, mod)


def _rope_lane_tables(n_ctx, n_tok):
    rows_count = n_tok // GRID_W
    row = jnp.repeat(jnp.arange(rows_count, dtype=jnp.int32), GRID_W).astype(F32)
    col = jnp.tile(jnp.arange(GRID_W, dtype=jnp.int32), rows_count).astype(F32)
    inv = ROPE_THETA ** (-jnp.arange(ROPE_PAIRS, dtype=F32) / ROPE_PAIRS)
    lane = np.arange(LANES)
    dd = lane % HEAD_DIM
    axis = dd // (2 * ROPE_PAIRS)
    half = (dd % (2 * ROPE_PAIRS)) // ROPE_PAIRS
    pair = dd % ROPE_PAIRS
    pos = jnp.where(jnp.asarray(axis == 0)[None, :], row[:, None], col[:, None])
    ang = pos * inv[jnp.asarray(pair)][None, :]
    cosv = jnp.cos(ang)
    sinv = jnp.sin(ang)
    first = jnp.asarray(half == 0)[None, :]
    s1 = jnp.where(first, -sinv, 0.0)
    s2 = jnp.where(first, 0.0, sinv)
    pad1 = jnp.ones((n_ctx, LANES), F32)
    pad0 = jnp.zeros((n_ctx, LANES), F32)
    return (jnp.concatenate([pad1, cosv], 0), jnp.concatenate([pad0, s1], 0), jnp.concatenate([pad0, s2], 0))


def _permute_w_in(w_in):
    off_ak = A_W
    off_av = 2 * A_W
    off_bq = 3 * A_W
    off_bk = off_bq + BQ_W
    off_bv = off_bk + BKV_W
    grp = GQA_HEADS // GQA_KV_HEADS
    bq_cols = []
    for p in range(grp):
        for kvh in range(GQA_KV_HEADS):
            hq = kvh * grp + p
            bq_cols.append(np.arange(off_bq + hq * HEAD_DIM, off_bq + (hq + 1) * HEAD_DIM))
    cols = np.concatenate([np.arange(0, A_W)] + bq_cols
                          + [np.arange(off_ak, off_av), np.arange(off_bk, off_bv),
                             np.arange(off_av, off_bq), np.arange(off_bv, off_bv + BKV_W)])
    return w_in[:, cols], np.concatenate(bq_cols) - off_bq


def kernel(x, c, ctx, c_ctx, l0_ada_w, l0_ada_b, l0_norm_mix, l0_w_in, l0_lambda_q1, l0_lambda_k1, l0_lambda_q2, l0_lambda_k2, l0_subln, l0_q_norm, l0_k_norm, l0_w_out, l0_norm_ffn, l0_w_gate_up, l0_w_down, l1_ada_w, l1_ada_b, l1_norm_mix, l1_w_out, l1_norm_ffn, l1_w_gate_up, l1_w_down, final_norm):
    b, s, d = x.shape
    n_ctx = ctx.shape[1]
    assert d == D_MODEL and s % GRID_W == 0

    tt = min(256, n_ctx, s)
    tq = min(128, tt)
    tk = min(256, n_ctx)
    tm = min(256, s)
    tmr = min(256, s // RADIX)
    assert n_ctx % tt == 0 and s % tt == 0 and tt % tq == 0 and (n_ctx + s) % tk == 0
    assert s % tm == 0 and (s // RADIX) % tmr == 0 and (s // RADIX) % RADIX == 0

    n_rows = -(-(b + 1) // SUBLANES) * SUBLANES
    cond = jnp.concatenate([c, c_ctx[None, :], jnp.zeros((n_rows - b - 1, d), F32)], axis=0)
    m0, m1 = _ada(cond, l0_ada_w, l0_ada_b, l1_ada_w, l1_ada_b)
    mod0 = m0[:b].reshape(b, N_MOD, d)
    mod0c = m0[b:b + 1].reshape(1, N_MOD, d)
    mod1 = m1[:b].reshape(b, N_MOD, d)

    w_perm, bq_perm = _permute_w_in(l0_w_in)
    cos_t, s1_t, s2_t = _rope_lane_tables(n_ctx, s)
    qg = jnp.tile(l0_q_norm, LANES // HEAD_DIM).reshape(1, LANES)
    kg = jnp.tile(l0_k_norm, LANES // HEAD_DIM).reshape(1, LANES)
    li = np.arange(LANES)
    ones_bd = jnp.asarray((li[:, None] // HEAD_DIM) == (li[None, :] // HEAD_DIM), BF16)
    qt, k, vt = _proj(x, ctx, mod0, mod0c, l0_norm_mix.reshape(1, d), w_perm.astype(BF16),
                      cos_t, s1_t, s2_t, qg, kg, ones_bd, tt, tq)
    lam_vecs = jnp.stack([l0_lambda_q1, l0_lambda_k1, l0_lambda_q2, l0_lambda_k2]).astype(F32)
    o = _attn(qt, k, vt, lam_vecs, l0_subln.reshape(1, LANES), tq, tk)
    w_out_rows = np.concatenate([np.arange(A_W), A_W + bq_perm])
    xr = _mixffn(o, x, mod0, l0_w_out[w_out_rows].astype(BF16), l0_norm_ffn.reshape(1, d),
                 l0_w_gate_up.astype(BF16), l0_w_down.astype(BF16), tm)
    xr = xr.reshape(b, s, d)

    wcs = _fold_channel_dft(l1_w_out)
    z = _fproj(xr, mod1, l1_norm_mix.reshape(1, d), wcs, tm)
    xr = _fseq(z, _seq_dft_tables(s).astype(BF16), xr, mod1, tmr)
    return _ffn_final(xr.reshape(b, RADIX, s // RADIX, d), mod1, l1_norm_ffn.reshape(1, d),
                      l1_w_gate_up.astype(BF16), l1_w_down.astype(BF16), final_norm.reshape(1, d), tm)
```

```python
import functools
import math

import numpy as np
import jax
import jax.numpy as jnp
from jax import lax
from jax.experimental import pallas as pl
from jax.experimental.pallas import tpu as pltpu

F32 = jnp.float32
BF16 = jnp.bfloat16
HIGHEST = lax.Precision.HIGHEST

LANES = 128
SUBLANES = 8
V7X_VMEM_BYTES = 64 * 1024 * 1024

D_MODEL = 1024
GRID_W = 64
DIFF_HEADS = 4
DIFF_HEAD_DIM = 64
GQA_HEADS = 8
GQA_KV_HEADS = 2
GQA_HEAD_DIM = 64
ROPE_THETA = 10000.0
ROPE_PAIRS = GQA_HEAD_DIM // 4
FOURIER_GROUPS = 8
FOURIER_GROUP_DIM = D_MODEL // FOURIER_GROUPS
EPS = 1e-6
N_MOD = 6
HEAD_DIM = 64
A_W = DIFF_HEADS * 2 * DIFF_HEAD_DIM
BQ_W = GQA_HEADS * GQA_HEAD_DIM
BKV_W = GQA_KV_HEADS * GQA_HEAD_DIM
Q_W = A_W + BQ_W
K_W = A_W + BKV_W
N_QG = Q_W // LANES
N_KG = K_W // LANES
LAM_INIT_L0 = 0.8 - 0.6 * math.exp(-0.3 * 0)
Q_SCALE = HEAD_DIM ** -0.5 * math.log2(math.e)
RADIX = 4
ACC_ROWS = 32
EXP_AHEAD = 3
VT_ROWS = LANES + 16


def _vmem_limit(nbytes):
    return int(min(nbytes, V7X_VMEM_BYTES - 4 * 1024 * 1024))


def _rms(x, g):
    ms = jnp.mean(x * x, axis=-1, keepdims=True)
    return x * lax.rsqrt(ms + EPS) * g


def _const_spec(shape):
    nd = len(shape)
    return pl.BlockSpec(shape, lambda *_: (0,) * nd, pipeline_mode=pl.Buffered(1))


def _fold_kernel(cd_ref, sd_ref, w_ref, o_ref):
    w = w_ref[...]
    d = w.shape[1]
    o_ref[:, 0:d] = jnp.dot(cd_ref[...], w, preferred_element_type=F32, precision=HIGHEST).astype(BF16)
    o_ref[:, d:2 * d] = jnp.dot(sd_ref[...], w, preferred_element_type=F32, precision=HIGHEST).astype(BF16)


def _fold_channel_dft(w_out):
    d = w_out.shape[0]
    gd = FOURIER_GROUP_DIM
    idx = np.arange(gd)
    ang = 2.0 * np.pi * ((idx[:, None] * idx[None, :]) % gd) / gd
    cd = jnp.asarray(np.cos(ang) / np.sqrt(gd), F32)
    sd = jnp.asarray(np.sin(ang) / np.sqrt(gd), F32)
    return pl.pallas_call(
        _fold_kernel,
        grid=(d // gd,),
        in_specs=[pl.BlockSpec((gd, gd), lambda g: (0, 0)),
                  pl.BlockSpec((gd, gd), lambda g: (0, 0)),
                  pl.BlockSpec((gd, d), lambda g: (g, 0))],
        out_specs=pl.BlockSpec((gd, 2 * d), lambda g: (g, 0)),
        out_shape=jax.ShapeDtypeStruct((d, 2 * d), BF16),
        name="fold",
    )(cd, sd, w_out)


def _ada_kernel(c_ref, w0_ref, b0_ref, w1_ref, b1_ref, o0_ref, o1_ref):
    cv = c_ref[...]
    a = cv * jax.nn.sigmoid(cv)
    o0_ref[...] = jnp.dot(a, w0_ref[...], preferred_element_type=F32, precision=HIGHEST) + b0_ref[...]
    o1_ref[...] = jnp.dot(a, w1_ref[...], preferred_element_type=F32, precision=HIGHEST) + b1_ref[...]


def _ada(cond, w0, b0, w1, b1):
    r, d = cond.shape
    n = w0.shape[1]
    bn = 512
    return pl.pallas_call(
        _ada_kernel,
        grid=(n // bn,),
        in_specs=[pl.BlockSpec((r, d), lambda j: (0, 0)),
                  pl.BlockSpec((d, bn), lambda j: (0, j)),
                  pl.BlockSpec((1, bn), lambda j: (0, j)),
                  pl.BlockSpec((d, bn), lambda j: (0, j)),
                  pl.BlockSpec((1, bn), lambda j: (0, j))],
        out_specs=[pl.BlockSpec((r, bn), lambda j: (0, j)),
                   pl.BlockSpec((r, bn), lambda j: (0, j))],
        out_shape=[jax.ShapeDtypeStruct((r, n), F32), jax.ShapeDtypeStruct((r, n), F32)],
        name="ada",
    )(cond, w0, b0.reshape(1, n), w1, b1.reshape(1, n))


def _proj_kernel(nct, tq, x_ref, ctx_ref, mod_ref, modc_ref, g_ref, w_ref, cos_ref, s1_ref, s2_ref,
                 qg_ref, kg_ref, ones_ref, qt_ref, k_ref, vt_ref):
    j = pl.program_id(1)
    is_ctx = j < nct
    xin = jnp.where(is_ctx, ctx_ref[0], x_ref[0])
    shift = jnp.where(is_ctx, modc_ref[0, 0:1, :], mod_ref[0, 0:1, :])
    scale = jnp.where(is_ctx, modc_ref[0, 1:2, :], mod_ref[0, 1:2, :])
    h = _rms(xin, g_ref[...]) * (1.0 + scale) + shift
    proj = jnp.dot(h.astype(BF16), w_ref[...], preferred_element_type=F32)

    tt = xin.shape[0]
    cosv = cos_ref[...]
    s1 = s1_ref[...]
    s2 = s2_ref[...]
    row = lax.broadcasted_iota(jnp.int32, (LANES, tt), 0)
    lo_rows = row < HEAD_DIM

    def rope(xs):
        return xs * cosv + pltpu.roll(xs, LANES - ROPE_PAIRS, 1) * s1 + pltpu.roll(xs, ROPE_PAIRS, 1) * s2

    def headnorm(xs, gain):
        sq = xs * xs
        hi = sq.astype(BF16)
        lo = (sq - hi.astype(F32)).astype(BF16)
        ss = (jnp.dot(hi, ones_ref[...], preferred_element_type=F32)
              + jnp.dot(lo, ones_ref[...], preferred_element_type=F32))
        return xs * lax.rsqrt(ss * (1.0 / HEAD_DIM) + EPS) * gain

    for s in range(N_QG):
        xs = proj[:, s * LANES:(s + 1) * LANES]
        if s >= A_W // LANES:
            xs = headnorm(xs, qg_ref[...])
        qst = (rope(xs) * Q_SCALE).T
        q_lo = jnp.where(lo_rows, qst, 0.0)
        q_hi = qst - q_lo
        for a in range(tt // tq):
            qt_ref[0, s, :, 2 * a * tq:(2 * a + 1) * tq] = q_lo[:, a * tq:(a + 1) * tq].astype(BF16)
            qt_ref[0, s, :, (2 * a + 1) * tq:(2 * a + 2) * tq] = q_hi[:, a * tq:(a + 1) * tq].astype(BF16)
    for s in range(N_KG):
        xs = proj[:, Q_W + s * LANES:Q_W + (s + 1) * LANES]
        if s >= A_W // LANES:
            xs = headnorm(xs, kg_ref[...])
        k_ref[0, s] = rope(xs).astype(BF16)
    for s in range(N_KG):
        vt_ref[0, s, 0:LANES, :] = proj[:, Q_W + K_W + s * LANES:Q_W + K_W + (s + 1) * LANES].T.astype(BF16)
        vt_ref[0, s, LANES:VT_ROWS, :] = jnp.ones((VT_ROWS - LANES, tt), BF16)


def _proj(x, ctx, mod, modc, g, w_perm, cos_t, s1_t, s2_t, qg, kg, ones_bd, tt, tq):
    b, s, d = x.shape
    nctx = ctx.shape[1]
    nct = nctx // tt
    nk = nctx + s
    nj = nk // tt
    n_in = w_perm.shape[1]
    kern = functools.partial(_proj_kernel, nct, tq)
    return pl.pallas_call(
        kern,
        grid=(b, nj),
        in_specs=[pl.BlockSpec((1, tt, d), lambda bi, j: (bi, jnp.maximum(j - nct, 0), 0)),
                  pl.BlockSpec((1, tt, d), lambda bi, j: (bi, jnp.minimum(j, nct - 1), 0)),
                  pl.BlockSpec((1, N_MOD, d), lambda bi, j: (bi, 0, 0)),
                  _const_spec((1, N_MOD, d)),
                  _const_spec((1, d)),
                  _const_spec((d, n_in)),
                  pl.BlockSpec((tt, LANES), lambda bi, j: (j, 0)),
                  pl.BlockSpec((tt, LANES), lambda bi, j: (j, 0)),
                  pl.BlockSpec((tt, LANES), lambda bi, j: (j, 0)),
                  _const_spec((1, LANES)),
                  _const_spec((1, LANES)),
                  _const_spec((LANES, LANES))],
        out_specs=[pl.BlockSpec((1, N_QG, LANES, 2 * tt), lambda bi, j: (bi, 0, 0, jnp.maximum(j - nct, 0))),
                   pl.BlockSpec((1, N_KG, tt, LANES), lambda bi, j: (bi, 0, j, 0)),
                   pl.BlockSpec((1, N_KG, VT_ROWS, tt), lambda bi, j: (bi, 0, 0, j))],
        out_shape=[jax.ShapeDtypeStruct((b, N_QG, LANES, 2 * s), BF16),
                   jax.ShapeDtypeStruct((b, N_KG, nk, LANES), BF16),
                   jax.ShapeDtypeStruct((b, N_KG, VT_ROWS, nk), BF16)],
        compiler_params=pltpu.CompilerParams(
            dimension_semantics=("parallel", "arbitrary"),
            vmem_limit_bytes=_vmem_limit(40 * 1024 * 1024)),
        name="proj",
    )(x, ctx, mod, modc, g, w_perm, cos_t, s1_t, s2_t, qg, kg, ones_bd)


def _attn_kernel(tq, tk, qt_ref, k_ref, vt_ref, lam_ref, subln_ref, o_ref, *s_bufs):
    nk = k_ref.shape[2]
    nc = nk // tk
    tq2 = 2 * tq
    row = lax.broadcasted_iota(jnp.int32, (LANES, tq), 0)
    lo_rows = row < HEAD_DIM
    lv = lam_ref[...]
    lam = (jnp.exp(jnp.sum(lv[0:1] * lv[1:2], axis=1, keepdims=True))
           - jnp.exp(jnp.sum(lv[2:3] * lv[3:4], axis=1, keepdims=True)) + LAM_INIT_L0)

    n_a = A_W // LANES

    def score_chunk(u, c, m8):
        kv = min(u, n_a)
        s_c = jnp.dot(k_ref[0, kv, c * tk:(c + 1) * tk, :], qt_ref[0, u], preferred_element_type=F32)
        s_bufs[u % len(s_bufs)][c * tk:(c + 1) * tk, :] = s_c
        mc = jnp.max(s_c.reshape(tk // ACC_ROWS, ACC_ROWS, tq2), axis=0)
        return mc if m8 is None else jnp.maximum(m8, mc)

    def value_chunk(u, c, m, acc, gate):
        kv = min(u, n_a)
        if gate is not None:
            m = m + jnp.minimum(jnp.abs(gate[LANES:LANES + 1]), 0.0)
        p = jnp.exp2((s_bufs[u % len(s_bufs)][c * tk:(c + 1) * tk, :] - m).astype(BF16))
        pv = jnp.dot(vt_ref[0, kv, :, c * tk:(c + 1) * tk], p, preferred_element_type=F32)
        return pv if acc is None else acc + pv

    def finish(u, acc):
        ot = acc[0:LANES] * (1.0 / acc[LANES:LANES + 1])
        o1, o2 = ot[:, :tq], ot[:, tq:]
        if u < n_a:
            ot = o1 - lam * o2
            ot = ot * lax.rsqrt(jnp.mean(ot * ot, axis=0, keepdims=True) + EPS)
            o_ref[0, u] = (ot.T * (subln_ref[...] * (1.0 - LAM_INIT_L0))).astype(BF16)
        else:
            o_ref[0, u] = jnp.where(lo_rows, o1, o2).T.astype(BF16)

    lead = min(3, nc)
    m8 = None
    for c in range(nc):
        m8 = score_chunk(0, c, m8)
    for u in range(N_QG):
        m = jnp.max(m8, axis=0, keepdims=True)
        m8 = None
        accs = []
        nxt = u + 1 < N_QG
        if nxt:
            for c in range(lead):
                m8 = score_chunk(u + 1, c, m8)
        for c in range(nc):
            gate = accs[c - EXP_AHEAD] if c >= EXP_AHEAD else None
            accs.append(value_chunk(u, c, m, accs[-1] if accs else None, gate))
            if nxt and c + lead < nc:
                m8 = score_chunk(u + 1, c + lead, m8)
        finish(u, accs[-1])


def _attn(qt, k, vt, lam_vecs, subln, tq, tk):
    b, _, _, s2 = qt.shape
    s = s2 // 2
    nk = k.shape[2]
    kern = functools.partial(_attn_kernel, tq, tk)
    return pl.pallas_call(
        kern,
        grid=(b, s // tq),
        in_specs=[pl.BlockSpec((1, N_QG, LANES, 2 * tq), lambda bi, i: (bi, 0, 0, i)),
                  pl.BlockSpec((1, N_KG, nk, LANES), lambda bi, i: (bi, 0, 0, 0)),
                  pl.BlockSpec((1, N_KG, VT_ROWS, nk), lambda bi, i: (bi, 0, 0, 0)),
                  _const_spec((4, HEAD_DIM)),
                  _const_spec((1, LANES))],
        out_specs=pl.BlockSpec((1, N_QG, tq, LANES), lambda bi, i: (bi, 0, i, 0)),
        out_shape=jax.ShapeDtypeStruct((b, N_QG, s, LANES), BF16),
        scratch_shapes=[pltpu.VMEM((nk, 2 * tq), F32)] * 3,
        compiler_params=pltpu.CompilerParams(
            dimension_semantics=("parallel", "arbitrary"),
            vmem_limit_bytes=_vmem_limit(48 * 1024 * 1024)),
        name="attn",
    )(qt, k, vt, lam_vecs, subln)


def _swiglu(x1, mod_ref, gffn_ref, wgu_ref, wd_ref):
    shift = mod_ref[0, 3:4, :]
    scale = mod_ref[0, 4:5, :]
    gate = mod_ref[0, 5:6, :]
    h = _rms(x1, gffn_ref[...]) * (1.0 + scale) + shift
    gu = jnp.dot(h.astype(BF16), wgu_ref[...], preferred_element_type=F32)
    dff = gu.shape[1] // 2
    g = gu[:, :dff]
    u = gu[:, dff:]
    a = (g * jax.nn.sigmoid(g)) * u
    dn = jnp.dot(a.astype(BF16), wd_ref[...], preferred_element_type=F32)
    return x1 + gate * dn


def _mixffn_kernel(o_ref, x_ref, mod_ref, wout_ref, gffn_ref, wgu_ref, wd_ref, out_ref, res_scr):
    o = jnp.concatenate([o_ref[0, u] for u in range(N_QG)], axis=1)
    y = jnp.dot(o, wout_ref[...], preferred_element_type=F32)
    x1 = x_ref[0] + mod_ref[0, 2:3, :] * y
    res = _swiglu(x1, mod_ref, gffn_ref, wgu_ref, wd_ref)
    n = res_scr.shape[1] // RADIX
    for sl in range(res_scr.shape[0]):
        res_scr[sl] = res[:, sl * LANES:(sl + 1) * LANES]
        for r in range(RADIX):
            out_ref[0, r, :, sl * LANES:(sl + 1) * LANES] = res_scr[sl, pl.ds(r, n, stride=RADIX), :]


def _ffn_final_kernel(x_ref, mod_ref, gffn_ref, wgu_ref, wd_ref, gfin_ref, out_ref, x_scr):
    n = x_scr.shape[1] // RADIX
    for sl in range(x_scr.shape[0]):
        for r in range(RADIX):
            x_scr[sl, pl.ds(r, n, stride=RADIX), :] = x_ref[0, r, :, sl * LANES:(sl + 1) * LANES]
    x1 = jnp.concatenate([x_scr[sl] for sl in range(x_scr.shape[0])], axis=1)
    x2 = _swiglu(x1, mod_ref, gffn_ref, wgu_ref, wd_ref)
    out_ref[0] = _rms(x2, gfin_ref[...])


def _mixffn(o, x, mod, wout, gffn, wgu, wd, tm):
    b, s, d = x.shape
    tok = lambda bi, i: (bi, i, 0)
    return pl.pallas_call(
        _mixffn_kernel,
        grid=(b, s // tm),
        in_specs=[pl.BlockSpec((1, N_QG, tm, LANES), lambda bi, i: (bi, 0, i, 0)),
                  pl.BlockSpec((1, tm, d), tok),
                  pl.BlockSpec((1, N_MOD, d), lambda bi, i: (bi, 0, 0)),
                  _const_spec(wout.shape), _const_spec((1, d)), _const_spec(wgu.shape), _const_spec(wd.shape)],
        out_specs=pl.BlockSpec((1, RADIX, tm // RADIX, d), lambda bi, i: (bi, 0, i, 0)),
        out_shape=jax.ShapeDtypeStruct((b, RADIX, s // RADIX, d), F32),
        scratch_shapes=[pltpu.VMEM((d // LANES, tm, LANES), F32)],
        compiler_params=pltpu.CompilerParams(
            dimension_semantics=("parallel", "arbitrary"),
            vmem_limit_bytes=_vmem_limit(56 * 1024 * 1024)),
        name="mixffn",
    )(o, x, mod, wout, gffn, wgu, wd)


def _ffn_final(xr, mod, gffn, wgu, wd, gfin, tm):
    b, _, nq, d = xr.shape
    s = nq * RADIX
    return pl.pallas_call(
        _ffn_final_kernel,
        grid=(b, s // tm),
        in_specs=[pl.BlockSpec((1, RADIX, tm // RADIX, d), lambda bi, i: (bi, 0, i, 0)),
                  pl.BlockSpec((1, N_MOD, d), lambda bi, i: (bi, 0, 0)),
                  _const_spec((1, d)), _const_spec(wgu.shape), _const_spec(wd.shape), _const_spec((1, d))],
        out_specs=pl.BlockSpec((1, tm, d), lambda bi, i: (bi, i, 0)),
        out_shape=jax.ShapeDtypeStruct((b, s, d), F32),
        scratch_shapes=[pltpu.VMEM((d // LANES, tm, LANES), F32)],
        compiler_params=pltpu.CompilerParams(
            dimension_semantics=("parallel", "arbitrary"),
            vmem_limit_bytes=_vmem_limit(56 * 1024 * 1024)),
        name="ffn",
    )(xr, mod, gffn, wgu, wd, gfin)


_QUARTER = ((1, 0), (0, 1), (-1, 0), (0, -1))


def _fproj_kernel(x0_ref, x1_ref, x2_ref, x3_ref, mod_ref, g_ref, w_ref, u_ref):
    d = x0_ref.shape[2]
    ta = x0_ref.shape[1]
    shift = mod_ref[0, 0:1, :]
    scale1 = 1.0 + mod_ref[0, 1:2, :]
    h = jnp.concatenate([(_rms(xr[0], g_ref[...]) * scale1 + shift).astype(BF16)
                         for xr in (x0_ref, x1_ref, x2_ref, x3_ref)], axis=0)
    z = jnp.dot(h, w_ref[...], preferred_element_type=F32)
    zc = [z[m * ta:(m + 1) * ta, 0:d] for m in range(RADIX)]
    zs = [z[m * ta:(m + 1) * ta, d:2 * d] for m in range(RADIX)]
    for r in range(RADIX):
        ua = None
        ub = None
        for m in range(RADIX):
            cp, sp = _QUARTER[(r * m) % 4]
            ta_, sa = (zc[m], cp) if cp else (zs[m], -sp)
            tb_, sb = (zs[m], cp) if cp else (zc[m], sp)
            ua = sa * ta_ if ua is None else (ua + ta_ if sa > 0 else ua - ta_)
            ub = sb * tb_ if ub is None else (ub + tb_ if sb > 0 else ub - tb_)
        u_ref[0, r, 0] = ua.astype(BF16)
        u_ref[0, r, 1] = ub.astype(BF16)


def _fproj(x, mod, g, wcs, ta):
    b, s, d = x.shape
    nq = s // RADIX
    nqq = nq // RADIX
    na = nqq // ta

    def x_spec(m):
        return pl.BlockSpec((1, ta, d), lambda bi, e, a: (bi, e * (nq // ta) + m * na + a, 0))

    return pl.pallas_call(
        _fproj_kernel,
        grid=(b, RADIX, na),
        in_specs=[x_spec(0), x_spec(1), x_spec(2), x_spec(3),
                  pl.BlockSpec((1, N_MOD, d), lambda bi, e, a: (bi, 0, 0)),
                  _const_spec((1, d)), _const_spec(wcs.shape)],
        out_specs=pl.BlockSpec((1, RADIX, 2, ta, d), lambda bi, e, a: (bi, 0, 0, e * na + a, 0)),
        out_shape=jax.ShapeDtypeStruct((b, RADIX, 2, nq, d), BF16),
        compiler_params=pltpu.CompilerParams(
            dimension_semantics=("parallel", "arbitrary", "arbitrary"),
            vmem_limit_bytes=_vmem_limit(40 * 1024 * 1024)),
        name="fproj",
    )(x, x, x, x, mod, g, wcs)


def _fseq_kernel(u_ref, t_ref, x_ref, mod_ref, out_ref):
    nq, d = u_ref.shape[3], u_ref.shape[4]
    u = u_ref[0, 0].reshape(2 * nq, d)
    y = jnp.dot(t_ref[0], u, preferred_element_type=F32)
    out_ref[0] = x_ref[0] + mod_ref[0, 2:3, :] * y


def _seq_dft_tables(n):
    nq = n // RADIX
    nqq = nq // RADIX
    u = np.arange(nq)
    q = (RADIX * (u % nqq) + u // nqq)[None, None, :]
    p = np.arange(nq)[None, :, None]
    r = np.arange(RADIX)[:, None, None]
    ang = 2.0 * np.pi * (((RADIX * p + r) * q) % n) / n
    t = np.concatenate([np.cos(ang), -np.sin(ang)], axis=2) / np.sqrt(n)
    return jnp.asarray(t, F32)


def _fseq(u, t, x, mod, tmr):
    b, s, d = x.shape
    nq = s // RADIX
    nblk = nq // tmr
    return pl.pallas_call(
        _fseq_kernel,
        grid=(b, RADIX, nblk),
        in_specs=[pl.BlockSpec((1, 1, 2, nq, d), lambda bi, r, i: (bi, r, 0, 0, 0)),
                  pl.BlockSpec((1, tmr, 2 * nq), lambda bi, r, i: (r, i, 0)),
                  pl.BlockSpec((1, tmr, d), lambda bi, r, i: (bi, r * nblk + i, 0)),
                  pl.BlockSpec((1, N_MOD, d), lambda bi, r, i: (bi, 0, 0))],
        out_specs=pl.BlockSpec((1, tmr, d), lambda bi, r, i: (bi, r * nblk + i, 0)),
        out_shape=jax.ShapeDtypeStruct((b, s, d), F32),
        compiler_params=pltpu.CompilerParams(
            dimension_semantics=("parallel", "arbitrary", "arbitrary"),
            vmem_limit_bytes=_vmem_limit(40 * 1024 * 1024)),
        name="fseq",
    )(u, t, x, mod)


def _rope_lane_tables(n_ctx, n_tok):
    rows_count = n_tok // GRID_W
    row = jnp.repeat(jnp.arange(rows_count, dtype=jnp.int32), GRID_W).astype(F32)
    col = jnp.tile(jnp.arange(GRID_W, dtype=jnp.int32), rows_count).astype(F32)
    inv = ROPE_THETA ** (-jnp.arange(ROPE_PAIRS, dtype=F32) / ROPE_PAIRS)
    lane = np.arange(LANES)
    dd = lane % HEAD_DIM
    axis = dd // (2 * ROPE_PAIRS)
    half = (dd % (2 * ROPE_PAIRS)) // ROPE_PAIRS
    pair = dd % ROPE_PAIRS
    pos = jnp.where(jnp.asarray(axis == 0)[None, :], row[:, None], col[:, None])
    ang = pos * inv[jnp.asarray(pair)][None, :]
    cosv = jnp.cos(ang)
    sinv = jnp.sin(ang)
    first = jnp.asarray(half == 0)[None, :]
    s1 = jnp.where(first, -sinv, 0.0)
    s2 = jnp.where(first, 0.0, sinv)
    pad1 = jnp.ones((n_ctx, LANES), F32)
    pad0 = jnp.zeros((n_ctx, LANES), F32)
    return (jnp.concatenate([pad1, cosv], 0), jnp.concatenate([pad0, s1], 0), jnp.concatenate([pad0, s2], 0))


def _permute_w_in(w_in):
    off_ak = A_W
    off_av = 2 * A_W
    off_bq = 3 * A_W
    off_bk = off_bq + BQ_W
    off_bv = off_bk + BKV_W
    grp = GQA_HEADS // GQA_KV_HEADS
    bq_cols = []
    for p in range(grp):
        for kvh in range(GQA_KV_HEADS):
            hq = kvh * grp + p
            bq_cols.append(np.arange(off_bq + hq * HEAD_DIM, off_bq + (hq + 1) * HEAD_DIM))
    cols = np.concatenate([np.arange(0, A_W)] + bq_cols
                          + [np.arange(off_ak, off_av), np.arange(off_bk, off_bv),
                             np.arange(off_av, off_bq), np.arange(off_bv, off_bv + BKV_W)])
    return w_in[:, cols], np.concatenate(bq_cols) - off_bq


def kernel(x, c, ctx, c_ctx, l0_ada_w, l0_ada_b, l0_norm_mix, l0_w_in, l0_lambda_q1, l0_lambda_k1, l0_lambda_q2, l0_lambda_k2, l0_subln, l0_q_norm, l0_k_norm, l0_w_out, l0_norm_ffn, l0_w_gate_up, l0_w_down, l1_ada_w, l1_ada_b, l1_norm_mix, l1_w_out, l1_norm_ffn, l1_w_gate_up, l1_w_down, final_norm):
    b, s, d = x.shape
    n_ctx = ctx.shape[1]
    assert d == D_MODEL and s % GRID_W == 0

    tt = min(256, n_ctx, s)
    tq = min(128, tt)
    tk = min(256, n_ctx)
    tm = min(512, s)
    tmr = min(512, s // RADIX)
    ta = min(128, s // RADIX ** 2)
    assert n_ctx % tt == 0 and s % tt == 0 and tt % tq == 0 and (n_ctx + s) % tk == 0
    assert s % tm == 0 and (s // RADIX) % tmr == 0 and (s // RADIX ** 2) % ta == 0

    n_rows = -(-(b + 1) // SUBLANES) * SUBLANES
    cond = jnp.concatenate([c, c_ctx[None, :], jnp.zeros((n_rows - b - 1, d), F32)], axis=0)
    m0, m1 = _ada(cond, l0_ada_w, l0_ada_b, l1_ada_w, l1_ada_b)
    mod0 = m0[:b].reshape(b, N_MOD, d)
    mod0c = m0[b:b + 1].reshape(1, N_MOD, d)
    mod1 = m1[:b].reshape(b, N_MOD, d)

    w_perm, bq_perm = _permute_w_in(l0_w_in)
    cos_t, s1_t, s2_t = _rope_lane_tables(n_ctx, s)
    qg = jnp.tile(l0_q_norm, LANES // HEAD_DIM).reshape(1, LANES)
    kg = jnp.tile(l0_k_norm, LANES // HEAD_DIM).reshape(1, LANES)
    li = np.arange(LANES)
    ones_bd = jnp.asarray((li[:, None] // HEAD_DIM) == (li[None, :] // HEAD_DIM), BF16)
    qt, k, vt = _proj(x, ctx, mod0, mod0c, l0_norm_mix.reshape(1, d), w_perm.astype(BF16),
                      cos_t, s1_t, s2_t, qg, kg, ones_bd, tt, tq)
    lam_vecs = jnp.stack([l0_lambda_q1, l0_lambda_k1, l0_lambda_q2, l0_lambda_k2]).astype(F32)
    o = _attn(qt, k, vt, lam_vecs, l0_subln.reshape(1, LANES), tq, tk)
    w_out_rows = np.concatenate([np.arange(A_W), A_W + bq_perm])
    xr = _mixffn(o, x, mod0, l0_w_out[w_out_rows].astype(BF16), l0_norm_ffn.reshape(1, d),
                 l0_w_gate_up.astype(BF16), l0_w_down.astype(BF16), tm)
    xr = xr.reshape(b, s, d)

    wcs = _fold_channel_dft(l1_w_out)
    u = _fproj(xr, mod1, l1_norm_mix.reshape(1, d), wcs, ta)
    xr = _fseq(u, _seq_dft_tables(s).astype(BF16), xr, mod1, tmr)
    return _ffn_final(xr.reshape(b, RADIX, s // RADIX, d), mod1, l1_norm_ffn.reshape(1, d),
                      l1_w_gate_up.astype(BF16), l1_w_down.astype(BF16), final_norm.reshape(1, d), tm)
```

```python
import functools
import math

import numpy as np
import jax
import jax.numpy as jnp
from jax import lax
from jax.experimental import pallas as pl
from jax.experimental.pallas import tpu as pltpu

F32 = jnp.float32
BF16 = jnp.bfloat16
HIGHEST = lax.Precision.HIGHEST

LANES = 128
SUBLANES = 8
V7X_VMEM_BYTES = 64 * 1024 * 1024

D_MODEL = 1024
GRID_W = 64
DIFF_HEADS = 4
DIFF_HEAD_DIM = 64
GQA_HEADS = 8
GQA_KV_HEADS = 2
GQA_HEAD_DIM = 64
ROPE_THETA = 10000.0
ROPE_PAIRS = GQA_HEAD_DIM // 4
FOURIER_GROUPS = 8
FOURIER_GROUP_DIM = D_MODEL // FOURIER_GROUPS
EPS = 1e-6
N_MOD = 6
HEAD_DIM = 64
A_W = DIFF_HEADS * 2 * DIFF_HEAD_DIM
BQ_W = GQA_HEADS * GQA_HEAD_DIM
BKV_W = GQA_KV_HEADS * GQA_HEAD_DIM
Q_W = A_W + BQ_W
K_W = A_W + BKV_W
N_QG = Q_W // LANES
N_KG = K_W // LANES
LAM_INIT_L0 = 0.8 - 0.6 * math.exp(-0.3 * 0)
Q_SCALE = HEAD_DIM ** -0.5 * math.log2(math.e)
RADIX = 4
ACC_ROWS = 32
VT_ROWS = LANES + 16


def _vmem_limit(nbytes):
    return int(min(nbytes, V7X_VMEM_BYTES - 4 * 1024 * 1024))


def _rms(x, g):
    ms = jnp.mean(x * x, axis=-1, keepdims=True)
    return x * lax.rsqrt(ms + EPS) * g


def _const_spec(shape):
    nd = len(shape)
    return pl.BlockSpec(shape, lambda *_: (0,) * nd, pipeline_mode=pl.Buffered(1))


def _fold_kernel(cd_ref, sd_ref, w_ref, o_ref):
    w = w_ref[...]
    d = w.shape[1]
    o_ref[:, 0:d] = jnp.dot(cd_ref[...], w, preferred_element_type=F32, precision=HIGHEST).astype(BF16)
    o_ref[:, d:2 * d] = jnp.dot(sd_ref[...], w, preferred_element_type=F32, precision=HIGHEST).astype(BF16)


def _fold_channel_dft(w_out):
    d = w_out.shape[0]
    gd = FOURIER_GROUP_DIM
    idx = np.arange(gd)
    ang = 2.0 * np.pi * ((idx[:, None] * idx[None, :]) % gd) / gd
    cd = jnp.asarray(np.cos(ang) / np.sqrt(gd), F32)
    sd = jnp.asarray(np.sin(ang) / np.sqrt(gd), F32)
    return pl.pallas_call(
        _fold_kernel,
        grid=(d // gd,),
        in_specs=[pl.BlockSpec((gd, gd), lambda g: (0, 0)),
                  pl.BlockSpec((gd, gd), lambda g: (0, 0)),
                  pl.BlockSpec((gd, d), lambda g: (g, 0))],
        out_specs=pl.BlockSpec((gd, 2 * d), lambda g: (g, 0)),
        out_shape=jax.ShapeDtypeStruct((d, 2 * d), BF16),
        name="fold",
    )(cd, sd, w_out)


def _ada_kernel(c_ref, w0_ref, b0_ref, w1_ref, b1_ref, o0_ref, o1_ref):
    cv = c_ref[...]
    a = cv * jax.nn.sigmoid(cv)
    o0_ref[...] = jnp.dot(a, w0_ref[...], preferred_element_type=F32, precision=HIGHEST) + b0_ref[...]
    o1_ref[...] = jnp.dot(a, w1_ref[...], preferred_element_type=F32, precision=HIGHEST) + b1_ref[...]


def _ada(cond, w0, b0, w1, b1):
    r, d = cond.shape
    n = w0.shape[1]
    bn = 512
    return pl.pallas_call(
        _ada_kernel,
        grid=(n // bn,),
        in_specs=[pl.BlockSpec((r, d), lambda j: (0, 0)),
                  pl.BlockSpec((d, bn), lambda j: (0, j)),
                  pl.BlockSpec((1, bn), lambda j: (0, j)),
                  pl.BlockSpec((d, bn), lambda j: (0, j)),
                  pl.BlockSpec((1, bn), lambda j: (0, j))],
        out_specs=[pl.BlockSpec((r, bn), lambda j: (0, j)),
                   pl.BlockSpec((r, bn), lambda j: (0, j))],
        out_shape=[jax.ShapeDtypeStruct((r, n), F32), jax.ShapeDtypeStruct((r, n), F32)],
        name="ada",
    )(cond, w0, b0.reshape(1, n), w1, b1.reshape(1, n))


def _proj_kernel(nct, tq, ng, *refs):
    x_refs, ctx_refs = refs[:ng], refs[ng:2 * ng]
    (mod_ref, modc_ref, g_ref, w_ref, cos_ref, s1_ref, s2_ref,
     qg_ref, kg_ref, ones_ref, qt_ref, k_ref, vt_ref) = refs[2 * ng:]
    j = pl.program_id(1)
    hs = []
    for t in range(ng):
        is_ctx = j * ng + t < nct
        xin = jnp.where(is_ctx, ctx_refs[t][0], x_refs[t][0])
        shift = jnp.where(is_ctx, modc_ref[0, 0:1, :], mod_ref[0, 0:1, :])
        scale = jnp.where(is_ctx, modc_ref[0, 1:2, :], mod_ref[0, 1:2, :])
        hs.append((_rms(xin, g_ref[...]) * (1.0 + scale) + shift).astype(BF16))
    h = jnp.concatenate(hs, axis=0)
    proj = jnp.dot(h, w_ref[...], preferred_element_type=F32)

    tt = h.shape[0]
    cosv = cos_ref[...]
    s1 = s1_ref[...]
    s2 = s2_ref[...]
    row = lax.broadcasted_iota(jnp.int32, (LANES, tt), 0)
    lo_rows = row < HEAD_DIM

    def rope(xs):
        return xs * cosv + pltpu.roll(xs, LANES - ROPE_PAIRS, 1) * s1 + pltpu.roll(xs, ROPE_PAIRS, 1) * s2

    def headnorm(xs, gain):
        sq = xs * xs
        hi = sq.astype(BF16)
        lo = (sq - hi.astype(F32)).astype(BF16)
        ss = (jnp.dot(hi, ones_ref[...], preferred_element_type=F32)
              + jnp.dot(lo, ones_ref[...], preferred_element_type=F32))
        return xs * lax.rsqrt(ss * (1.0 / HEAD_DIM) + EPS) * gain

    for s in range(N_QG):
        xs = proj[:, s * LANES:(s + 1) * LANES]
        if s >= A_W // LANES:
            xs = headnorm(xs, qg_ref[...])
        qst = (rope(xs) * Q_SCALE).T
        q_lo = jnp.where(lo_rows, qst, 0.0)
        q_hi = qst - q_lo
        for a in range(tt // tq):
            qt_ref[0, s, :, 2 * a * tq:(2 * a + 1) * tq] = q_lo[:, a * tq:(a + 1) * tq].astype(BF16)
            qt_ref[0, s, :, (2 * a + 1) * tq:(2 * a + 2) * tq] = q_hi[:, a * tq:(a + 1) * tq].astype(BF16)
    for s in range(N_KG):
        xs = proj[:, Q_W + s * LANES:Q_W + (s + 1) * LANES]
        if s >= A_W // LANES:
            xs = headnorm(xs, kg_ref[...])
        k_ref[0, s] = rope(xs).astype(BF16)
    for s in range(N_KG):
        vt_ref[0, s, 0:LANES, :] = proj[:, Q_W + K_W + s * LANES:Q_W + K_W + (s + 1) * LANES].T.astype(BF16)
        vt_ref[0, s, LANES:VT_ROWS, :] = jnp.ones((VT_ROWS - LANES, tt), BF16)


def _proj(x, ctx, mod, modc, g, w_perm, cos_t, s1_t, s2_t, qg, kg, ones_bd, tt, tq):
    b, s, d = x.shape
    nctx = ctx.shape[1]
    nct = nctx // tt
    nk = nctx + s
    nj = nk // tt
    ng = max(gsz for gsz in (1, 2, 3) if nj % gsz == 0)
    rows = ng * tt
    n_in = w_perm.shape[1]
    kern = functools.partial(_proj_kernel, nct, tq, ng)

    def x_spec(t):
        return pl.BlockSpec((1, tt, d), lambda bi, j: (bi, jnp.maximum(j * ng + t - nct, 0), 0))

    def ctx_spec(t):
        return pl.BlockSpec((1, tt, d), lambda bi, j: (bi, jnp.minimum(j * ng + t, nct - 1), 0))

    tab_spec = pl.BlockSpec((rows, LANES), lambda bi, j: (j, 0))
    return pl.pallas_call(
        kern,
        grid=(b, nj // ng),
        in_specs=[x_spec(t) for t in range(ng)] + [ctx_spec(t) for t in range(ng)]
        + [pl.BlockSpec((1, N_MOD, d), lambda bi, j: (bi, 0, 0)),
           _const_spec((1, N_MOD, d)),
           _const_spec((1, d)),
           _const_spec((d, n_in)),
           tab_spec, tab_spec, tab_spec,
           _const_spec((1, LANES)),
           _const_spec((1, LANES)),
           _const_spec((LANES, LANES))],
        out_specs=[pl.BlockSpec((1, N_QG, LANES, 2 * rows), lambda bi, j: (bi, 0, 0, j)),
                   pl.BlockSpec((1, N_KG, rows, LANES), lambda bi, j: (bi, 0, j, 0)),
                   pl.BlockSpec((1, N_KG, VT_ROWS, rows), lambda bi, j: (bi, 0, 0, j))],
        out_shape=[jax.ShapeDtypeStruct((b, N_QG, LANES, 2 * nk), BF16),
                   jax.ShapeDtypeStruct((b, N_KG, nk, LANES), BF16),
                   jax.ShapeDtypeStruct((b, N_KG, VT_ROWS, nk), BF16)],
        compiler_params=pltpu.CompilerParams(
            dimension_semantics=("parallel", "arbitrary"),
            vmem_limit_bytes=_vmem_limit(56 * 1024 * 1024)),
        name="proj",
    )(*([x] * ng), *([ctx] * ng), mod, modc, g, w_perm, cos_t, s1_t, s2_t, qg, kg, ones_bd)


def _attn_kernel(tq, tk, qt_ref, k_ref, vt_ref, lam_ref, subln_ref, o_ref, *bufs):
    nk = k_ref.shape[2]
    nc = nk // tk
    tq2 = 2 * tq
    row = lax.broadcasted_iota(jnp.int32, (LANES, tq), 0)
    lo_rows = row < HEAD_DIM
    lv = lam_ref[...]
    lam = (jnp.exp(jnp.sum(lv[0:1] * lv[1:2], axis=1, keepdims=True))
           - jnp.exp(jnp.sum(lv[2:3] * lv[3:4], axis=1, keepdims=True)) + LAM_INIT_L0)

    n_a = A_W // LANES

    s_bufs, p_bufs = bufs[:3], bufs[3:]

    def score_chunk(u, c, m8):
        kv = min(u, n_a)
        s_c = jnp.dot(k_ref[0, kv, c * tk:(c + 1) * tk, :], qt_ref[0, u], preferred_element_type=F32)
        s_bufs[u % 3][c * tk:(c + 1) * tk, :] = s_c
        mc = jnp.max(s_c.reshape(tk // ACC_ROWS, ACC_ROWS, tq2), axis=0)
        return mc if m8 is None else jnp.maximum(m8, mc)

    def exp_chunk(u, c, m):
        p_bufs[u % 2][c * tk:(c + 1) * tk, :] = jnp.exp2((s_bufs[u % 3][c * tk:(c + 1) * tk, :] - m).astype(BF16))

    def value_chunk(u, c, acc):
        kv = min(u, n_a)
        pv = jnp.dot(vt_ref[0, kv, :, c * tk:(c + 1) * tk], p_bufs[u % 2][c * tk:(c + 1) * tk, :],
                     preferred_element_type=F32)
        return pv if acc is None else acc + pv

    def finish(u, acc):
        ot = acc[0:LANES] * (1.0 / acc[LANES:LANES + 1])
        o1, o2 = ot[:, :tq], ot[:, tq:]
        if u < n_a:
            ot = o1 - lam * o2
            ot = ot * lax.rsqrt(jnp.mean(ot * ot, axis=0, keepdims=True) + EPS)
            o_ref[0, u] = (ot.T * (subln_ref[...] * (1.0 - LAM_INIT_L0))).astype(BF16)
        else:
            o_ref[0, u] = jnp.where(lo_rows, o1, o2).T.astype(BF16)

    m8s = {}
    ms = {}
    accs = {}
    for t in range(N_QG + 2):
        if 0 <= t - 1 < N_QG:
            ms[t - 1] = jnp.max(m8s.pop(t - 1), axis=0, keepdims=True)
        for c in range(nc):
            pace = None
            if c > 0:
                pace = m8s[t][0:1] if t < N_QG else accs[t - 2][0:1]
            if t < N_QG:
                m8s[t] = score_chunk(t, c, m8s.get(t))
            if 0 <= t - 1 < N_QG:
                m = ms[t - 1]
                if pace is not None:
                    m = m + jnp.minimum(jnp.abs(pace), 0.0)
                exp_chunk(t - 1, c, m)
            if 0 <= t - 2 < N_QG:
                accs[t - 2] = value_chunk(t - 2, c, accs.get(t - 2))
        if 0 <= t - 2 < N_QG:
            finish(t - 2, accs.pop(t - 2))


def _attn(qt, k, vt, lam_vecs, subln, s, tq, tk):
    b = qt.shape[0]
    nk = k.shape[2]
    q0 = (nk - s) // tq
    kern = functools.partial(_attn_kernel, tq, tk)
    return pl.pallas_call(
        kern,
        grid=(b, s // tq),
        in_specs=[pl.BlockSpec((1, N_QG, LANES, 2 * tq), lambda bi, i: (bi, 0, 0, i + q0)),
                  pl.BlockSpec((1, N_KG, nk, LANES), lambda bi, i: (bi, 0, 0, 0)),
                  pl.BlockSpec((1, N_KG, VT_ROWS, nk), lambda bi, i: (bi, 0, 0, 0)),
                  _const_spec((4, HEAD_DIM)),
                  _const_spec((1, LANES))],
        out_specs=pl.BlockSpec((1, N_QG, tq, LANES), lambda bi, i: (bi, 0, i, 0)),
        out_shape=jax.ShapeDtypeStruct((b, N_QG, s, LANES), BF16),
        scratch_shapes=[pltpu.VMEM((nk, 2 * tq), F32)] * 3 + [pltpu.VMEM((nk, 2 * tq), BF16)] * 2,
        compiler_params=pltpu.CompilerParams(
            dimension_semantics=("parallel", "arbitrary"),
            vmem_limit_bytes=_vmem_limit(48 * 1024 * 1024)),
        name="attn",
    )(qt, k, vt, lam_vecs, subln)


def _swiglu(x1, mod_ref, gffn_ref, wgu_ref, wd_ref):
    shift = mod_ref[0, 3:4, :]
    scale = mod_ref[0, 4:5, :]
    gate = mod_ref[0, 5:6, :]
    h = _rms(x1, gffn_ref[...]) * (1.0 + scale) + shift
    gu = jnp.dot(h.astype(BF16), wgu_ref[...], preferred_element_type=F32)
    dff = gu.shape[1] // 2
    g = gu[:, :dff]
    u = gu[:, dff:]
    a = (g * jax.nn.sigmoid(g)) * u
    dn = jnp.dot(a.astype(BF16), wd_ref[...], preferred_element_type=F32)
    return x1 + gate * dn


def _mixffn_kernel(o_ref, x_ref, mod_ref, wout_ref, gffn_ref, wgu_ref, wd_ref, out_ref, res_scr):
    o = jnp.concatenate([o_ref[0, u] for u in range(N_QG)], axis=1)
    y = jnp.dot(o, wout_ref[...], preferred_element_type=F32)
    x1 = x_ref[0] + mod_ref[0, 2:3, :] * y
    res = _swiglu(x1, mod_ref, gffn_ref, wgu_ref, wd_ref)
    n = res_scr.shape[1] // RADIX
    for sl in range(res_scr.shape[0]):
        res_scr[sl] = res[:, sl * LANES:(sl + 1) * LANES]
        for r in range(RADIX):
            out_ref[0, r, :, sl * LANES:(sl + 1) * LANES] = res_scr[sl, pl.ds(r, n, stride=RADIX), :]


def _ffn_final_kernel(x_ref, mod_ref, gffn_ref, wgu_ref, wd_ref, gfin_ref, out_ref, x_scr):
    n = x_scr.shape[1] // RADIX
    for sl in range(x_scr.shape[0]):
        for r in range(RADIX):
            x_scr[sl, pl.ds(r, n, stride=RADIX), :] = x_ref[0, r, :, sl * LANES:(sl + 1) * LANES]
    x1 = jnp.concatenate([x_scr[sl] for sl in range(x_scr.shape[0])], axis=1)
    x2 = _swiglu(x1, mod_ref, gffn_ref, wgu_ref, wd_ref)
    out_ref[0] = _rms(x2, gfin_ref[...])


def _mixffn(o, x, mod, wout, gffn, wgu, wd, tm):
    b, s, d = x.shape
    tok = lambda bi, i: (bi, i, 0)
    return pl.pallas_call(
        _mixffn_kernel,
        grid=(b, s // tm),
        in_specs=[pl.BlockSpec((1, N_QG, tm, LANES), lambda bi, i: (bi, 0, i, 0)),
                  pl.BlockSpec((1, tm, d), tok),
                  pl.BlockSpec((1, N_MOD, d), lambda bi, i: (bi, 0, 0)),
                  _const_spec(wout.shape), _const_spec((1, d)), _const_spec(wgu.shape), _const_spec(wd.shape)],
        out_specs=pl.BlockSpec((1, RADIX, tm // RADIX, d), lambda bi, i: (bi, 0, i, 0)),
        out_shape=jax.ShapeDtypeStruct((b, RADIX, s // RADIX, d), F32),
        scratch_shapes=[pltpu.VMEM((d // LANES, tm, LANES), F32)],
        compiler_params=pltpu.CompilerParams(
            dimension_semantics=("parallel", "arbitrary"),
            vmem_limit_bytes=_vmem_limit(56 * 1024 * 1024)),
        name="mixffn",
    )(o, x, mod, wout, gffn, wgu, wd)


def _ffn_final(xr, mod, gffn, wgu, wd, gfin, tm):
    b, _, nq, d = xr.shape
    s = nq * RADIX
    return pl.pallas_call(
        _ffn_final_kernel,
        grid=(b, s // tm),
        in_specs=[pl.BlockSpec((1, RADIX, tm // RADIX, d), lambda bi, i: (bi, 0, i, 0)),
                  pl.BlockSpec((1, N_MOD, d), lambda bi, i: (bi, 0, 0)),
                  _const_spec((1, d)), _const_spec(wgu.shape), _const_spec(wd.shape), _const_spec((1, d))],
        out_specs=pl.BlockSpec((1, tm, d), lambda bi, i: (bi, i, 0)),
        out_shape=jax.ShapeDtypeStruct((b, s, d), F32),
        scratch_shapes=[pltpu.VMEM((d // LANES, tm, LANES), F32)],
        compiler_params=pltpu.CompilerParams(
            dimension_semantics=("parallel", "arbitrary"),
            vmem_limit_bytes=_vmem_limit(56 * 1024 * 1024)),
        name="ffn",
    )(xr, mod, gffn, wgu, wd, gfin)


_QUARTER = ((1, 0), (0, 1), (-1, 0), (0, -1))


def _fproj_kernel(x0_ref, x1_ref, x2_ref, x3_ref, mod_ref, g_ref, w_ref, u_ref):
    d = x0_ref.shape[2]
    ta = x0_ref.shape[1]
    shift = mod_ref[0, 0:1, :]
    scale1 = 1.0 + mod_ref[0, 1:2, :]
    h = jnp.concatenate([(_rms(xr[0], g_ref[...]) * scale1 + shift).astype(BF16)
                         for xr in (x0_ref, x1_ref, x2_ref, x3_ref)], axis=0)
    z = jnp.dot(h, w_ref[...], preferred_element_type=F32)
    zc = [z[m * ta:(m + 1) * ta, 0:d] for m in range(RADIX)]
    zs = [z[m * ta:(m + 1) * ta, d:2 * d] for m in range(RADIX)]
    for r in range(RADIX):
        ua = None
        ub = None
        for m in range(RADIX):
            cp, sp = _QUARTER[(r * m) % 4]
            ta_, sa = (zc[m], cp) if cp else (zs[m], -sp)
            tb_, sb = (zs[m], cp) if cp else (zc[m], sp)
            ua = sa * ta_ if ua is None else (ua + ta_ if sa > 0 else ua - ta_)
            ub = sb * tb_ if ub is None else (ub + tb_ if sb > 0 else ub - tb_)
        u_ref[0, r, 0] = ua.astype(BF16)
        u_ref[0, r, 1] = ub.astype(BF16)


def _fproj(x, mod, g, wcs, ta):
    b, s, d = x.shape
    nq = s // RADIX
    nqq = nq // RADIX
    na = nqq // ta

    def x_spec(m):
        return pl.BlockSpec((1, ta, d), lambda bi, e, a: (bi, e * (nq // ta) + m * na + a, 0))

    return pl.pallas_call(
        _fproj_kernel,
        grid=(b, RADIX, na),
        in_specs=[x_spec(0), x_spec(1), x_spec(2), x_spec(3),
                  pl.BlockSpec((1, N_MOD, d), lambda bi, e, a: (bi, 0, 0)),
                  _const_spec((1, d)), _const_spec(wcs.shape)],
        out_specs=pl.BlockSpec((1, RADIX, 2, ta, d), lambda bi, e, a: (bi, 0, 0, e * na + a, 0)),
        out_shape=jax.ShapeDtypeStruct((b, RADIX, 2, nq, d), BF16),
        compiler_params=pltpu.CompilerParams(
            dimension_semantics=("parallel", "arbitrary", "arbitrary"),
            vmem_limit_bytes=_vmem_limit(40 * 1024 * 1024)),
        name="fproj",
    )(x, x, x, x, mod, g, wcs)


def _fseq_kernel(u_ref, t_ref, x_ref, mod_ref, out_ref):
    nq, d = u_ref.shape[3], u_ref.shape[4]
    u = u_ref[0, 0].reshape(2 * nq, d)
    y = jnp.dot(t_ref[0], u, preferred_element_type=F32)
    out_ref[0] = x_ref[0] + mod_ref[0, 2:3, :] * y


def _seq_dft_tables(n):
    nq = n // RADIX
    nqq = nq // RADIX
    u = np.arange(nq)
    q = (RADIX * (u % nqq) + u // nqq)[None, None, :]
    p = np.arange(nq)[None, :, None]
    r = np.arange(RADIX)[:, None, None]
    ang = 2.0 * np.pi * (((RADIX * p + r) * q) % n) / n
    t = np.concatenate([np.cos(ang), -np.sin(ang)], axis=2) / np.sqrt(n)
    return jnp.asarray(t, F32)


def _fseq(u, t, x, mod, tmr):
    b, s, d = x.shape
    nq = s // RADIX
    nblk = nq // tmr
    return pl.pallas_call(
        _fseq_kernel,
        grid=(b, RADIX, nblk),
        in_specs=[pl.BlockSpec((1, 1, 2, nq, d), lambda bi, r, i: (bi, r, 0, 0, 0)),
                  pl.BlockSpec((1, tmr, 2 * nq), lambda bi, r, i: (r, i, 0)),
                  pl.BlockSpec((1, tmr, d), lambda bi, r, i: (bi, r * nblk + i, 0)),
                  pl.BlockSpec((1, N_MOD, d), lambda bi, r, i: (bi, 0, 0))],
        out_specs=pl.BlockSpec((1, tmr, d), lambda bi, r, i: (bi, r * nblk + i, 0)),
        out_shape=jax.ShapeDtypeStruct((b, s, d), F32),
        compiler_params=pltpu.CompilerParams(
            dimension_semantics=("parallel", "arbitrary", "arbitrary"),
            vmem_limit_bytes=_vmem_limit(40 * 1024 * 1024)),
        name="fseq",
    )(u, t, x, mod)


def _rope_lane_tables(n_ctx, n_tok):
    rows_count = n_tok // GRID_W
    row = jnp.repeat(jnp.arange(rows_count, dtype=jnp.int32), GRID_W).astype(F32)
    col = jnp.tile(jnp.arange(GRID_W, dtype=jnp.int32), rows_count).astype(F32)
    inv = ROPE_THETA ** (-jnp.arange(ROPE_PAIRS, dtype=F32) / ROPE_PAIRS)
    lane = np.arange(LANES)
    dd = lane % HEAD_DIM
    axis = dd // (2 * ROPE_PAIRS)
    half = (dd % (2 * ROPE_PAIRS)) // ROPE_PAIRS
    pair = dd % ROPE_PAIRS
    pos = jnp.where(jnp.asarray(axis == 0)[None, :], row[:, None], col[:, None])
    ang = pos * inv[jnp.asarray(pair)][None, :]
    cosv = jnp.cos(ang)
    sinv = jnp.sin(ang)
    first = jnp.asarray(half == 0)[None, :]
    s1 = jnp.where(first, -sinv, 0.0)
    s2 = jnp.where(first, 0.0, sinv)
    pad1 = jnp.ones((n_ctx, LANES), F32)
    pad0 = jnp.zeros((n_ctx, LANES), F32)
    return (jnp.concatenate([pad1, cosv], 0), jnp.concatenate([pad0, s1], 0), jnp.concatenate([pad0, s2], 0))


def _permute_w_in(w_in):
    off_ak = A_W
    off_av = 2 * A_W
    off_bq = 3 * A_W
    off_bk = off_bq + BQ_W
    off_bv = off_bk + BKV_W
    grp = GQA_HEADS // GQA_KV_HEADS
    bq_cols = []
    for p in range(grp):
        for kvh in range(GQA_KV_HEADS):
            hq = kvh * grp + p
            bq_cols.append(np.arange(off_bq + hq * HEAD_DIM, off_bq + (hq + 1) * HEAD_DIM))
    cols = np.concatenate([np.arange(0, A_W)] + bq_cols
                          + [np.arange(off_ak, off_av), np.arange(off_bk, off_bv),
                             np.arange(off_av, off_bq), np.arange(off_bv, off_bv + BKV_W)])
    return w_in[:, cols], np.concatenate(bq_cols) - off_bq


def kernel(x, c, ctx, c_ctx, l0_ada_w, l0_ada_b, l0_norm_mix, l0_w_in, l0_lambda_q1, l0_lambda_k1, l0_lambda_q2, l0_lambda_k2, l0_subln, l0_q_norm, l0_k_norm, l0_w_out, l0_norm_ffn, l0_w_gate_up, l0_w_down, l1_ada_w, l1_ada_b, l1_norm_mix, l1_w_out, l1_norm_ffn, l1_w_gate_up, l1_w_down, final_norm):
    b, s, d = x.shape
    n_ctx = ctx.shape[1]
    assert d == D_MODEL and s % GRID_W == 0

    tt = min(256, n_ctx, s)
    tq = min(128, tt)
    tk = min(256, n_ctx)
    tm = min(512, s)
    tmr = min(512, s // RADIX)
    ta = min(128, s // RADIX ** 2)
    assert n_ctx % tt == 0 and s % tt == 0 and tt % tq == 0 and (n_ctx + s) % tk == 0
    assert s % tm == 0 and (s // RADIX) % tmr == 0 and (s // RADIX ** 2) % ta == 0

    n_rows = -(-(b + 1) // SUBLANES) * SUBLANES
    cond = jnp.concatenate([c, c_ctx[None, :], jnp.zeros((n_rows - b - 1, d), F32)], axis=0)
    m0, m1 = _ada(cond, l0_ada_w, l0_ada_b, l1_ada_w, l1_ada_b)
    mod0 = m0[:b].reshape(b, N_MOD, d)
    mod0c = m0[b:b + 1].reshape(1, N_MOD, d)
    mod1 = m1[:b].reshape(b, N_MOD, d)

    w_perm, bq_perm = _permute_w_in(l0_w_in)
    cos_t, s1_t, s2_t = _rope_lane_tables(n_ctx, s)
    qg = jnp.tile(l0_q_norm, LANES // HEAD_DIM).reshape(1, LANES)
    kg = jnp.tile(l0_k_norm, LANES // HEAD_DIM).reshape(1, LANES)
    li = np.arange(LANES)
    ones_bd = jnp.asarray((li[:, None] // HEAD_DIM) == (li[None, :] // HEAD_DIM), BF16)
    qt, k, vt = _proj(x, ctx, mod0, mod0c, l0_norm_mix.reshape(1, d), w_perm.astype(BF16),
                      cos_t, s1_t, s2_t, qg, kg, ones_bd, tt, tq)
    lam_vecs = jnp.stack([l0_lambda_q1, l0_lambda_k1, l0_lambda_q2, l0_lambda_k2]).astype(F32)
    o = _attn(qt, k, vt, lam_vecs, l0_subln.reshape(1, LANES), s, tq, tk)
    w_out_rows = np.concatenate([np.arange(A_W), A_W + bq_perm])
    xr = _mixffn(o, x, mod0, l0_w_out[w_out_rows].astype(BF16), l0_norm_ffn.reshape(1, d),
                 l0_w_gate_up.astype(BF16), l0_w_down.astype(BF16), tm)
    xr = xr.reshape(b, s, d)

    wcs = _fold_channel_dft(l1_w_out)
    u = _fproj(xr, mod1, l1_norm_mix.reshape(1, d), wcs, ta)
    xr = _fseq(u, _seq_dft_tables(s).astype(BF16), xr, mod1, tmr)
    return _ffn_final(xr.reshape(b, RADIX, s // RADIX, d), mod1, l1_norm_ffn.reshape(1, d),
                      l1_w_gate_up.astype(BF16), l1_w_down.astype(BF16), final_norm.reshape(1, d), tm)
```

```python
import functools
import math

import numpy as np
import jax
import jax.numpy as jnp
from jax import lax
from jax.experimental import pallas as pl
from jax.experimental.pallas import tpu as pltpu

F32 = jnp.float32
BF16 = jnp.bfloat16
HIGHEST = lax.Precision.HIGHEST

LANES = 128
SUBLANES = 8
V7X_VMEM_BYTES = 64 * 1024 * 1024

D_MODEL = 1024
GRID_W = 64
DIFF_HEADS = 4
DIFF_HEAD_DIM = 64
GQA_HEADS = 8
GQA_KV_HEADS = 2
GQA_HEAD_DIM = 64
ROPE_THETA = 10000.0
ROPE_PAIRS = GQA_HEAD_DIM // 4
FOURIER_GROUPS = 8
FOURIER_GROUP_DIM = D_MODEL // FOURIER_GROUPS
EPS = 1e-6
N_MOD = 6
HEAD_DIM = 64
A_W = DIFF_HEADS * 2 * DIFF_HEAD_DIM
BQ_W = GQA_HEADS * GQA_HEAD_DIM
BKV_W = GQA_KV_HEADS * GQA_HEAD_DIM
Q_W = A_W + BQ_W
K_W = A_W + BKV_W
N_QG = Q_W // LANES
N_KG = K_W // LANES
LAM_INIT_L0 = 0.8 - 0.6 * math.exp(-0.3 * 0)
Q_SCALE = HEAD_DIM ** -0.5 * math.log2(math.e)
RADIX = 4
ACC_ROWS = 32
PACE_LAG = 1
FFN_SPLIT = 2
VALUE_CHUNKS = 1
VT_ROWS = LANES + 16


def _vmem_limit(nbytes):
    return int(min(nbytes, V7X_VMEM_BYTES - 4 * 1024 * 1024))


def _rms(x, g):
    ms = jnp.mean(x * x, axis=-1, keepdims=True)
    return x * lax.rsqrt(ms + EPS) * g


def _const_spec(shape):
    nd = len(shape)
    return pl.BlockSpec(shape, lambda *_: (0,) * nd, pipeline_mode=pl.Buffered(1))


def _fold_kernel(cd_ref, sd_ref, w_ref, o_ref):
    w = w_ref[...]
    d = w.shape[1]
    o_ref[:, 0:d] = jnp.dot(cd_ref[...], w, preferred_element_type=F32, precision=HIGHEST).astype(BF16)
    o_ref[:, d:2 * d] = jnp.dot(sd_ref[...], w, preferred_element_type=F32, precision=HIGHEST).astype(BF16)


def _fold_channel_dft(w_out):
    d = w_out.shape[0]
    gd = FOURIER_GROUP_DIM
    idx = np.arange(gd)
    ang = 2.0 * np.pi * ((idx[:, None] * idx[None, :]) % gd) / gd
    cd = jnp.asarray(np.cos(ang) / np.sqrt(gd), F32)
    sd = jnp.asarray(np.sin(ang) / np.sqrt(gd), F32)
    return pl.pallas_call(
        _fold_kernel,
        grid=(d // gd,),
        in_specs=[pl.BlockSpec((gd, gd), lambda g: (0, 0)),
                  pl.BlockSpec((gd, gd), lambda g: (0, 0)),
                  pl.BlockSpec((gd, d), lambda g: (g, 0))],
        out_specs=pl.BlockSpec((gd, 2 * d), lambda g: (g, 0)),
        out_shape=jax.ShapeDtypeStruct((d, 2 * d), BF16),
        name="fold",
    )(cd, sd, w_out)


def _ada_kernel(c_ref, w0_ref, b0_ref, w1_ref, b1_ref, o0_ref, o1_ref):
    cv = c_ref[...]
    a = (cv * jax.nn.sigmoid(cv)).astype(BF16)
    o0_ref[...] = jnp.dot(a, w0_ref[...].astype(BF16), preferred_element_type=F32) + b0_ref[...]
    o1_ref[...] = jnp.dot(a, w1_ref[...].astype(BF16), preferred_element_type=F32) + b1_ref[...]


def _ada(cond, w0, b0, w1, b1):
    r, d = cond.shape
    n = w0.shape[1]
    bn = 512
    return pl.pallas_call(
        _ada_kernel,
        grid=(n // bn,),
        in_specs=[pl.BlockSpec((r, d), lambda j: (0, 0)),
                  pl.BlockSpec((d, bn), lambda j: (0, j)),
                  pl.BlockSpec((1, bn), lambda j: (0, j)),
                  pl.BlockSpec((d, bn), lambda j: (0, j)),
                  pl.BlockSpec((1, bn), lambda j: (0, j))],
        out_specs=[pl.BlockSpec((r, bn), lambda j: (0, j)),
                   pl.BlockSpec((r, bn), lambda j: (0, j))],
        out_shape=[jax.ShapeDtypeStruct((r, n), F32), jax.ShapeDtypeStruct((r, n), F32)],
        name="ada",
    )(cond, w0, b0.reshape(1, n), w1, b1.reshape(1, n))


def _proj_kernel(nct, tq, ng, *refs):
    x_refs, ctx_refs = refs[:ng], refs[ng:2 * ng]
    (mod_ref, modc_ref, g_ref, w_ref, cos_ref, s1_ref, s2_ref,
     qg_ref, kg_ref, ones_ref, qt_ref, k_ref, vt_ref) = refs[2 * ng:]
    j = pl.program_id(1)
    hs = []
    for t in range(ng):
        is_ctx = j * ng + t < nct
        xin = jnp.where(is_ctx, ctx_refs[t][0], x_refs[t][0])
        shift = jnp.where(is_ctx, modc_ref[0, 0:1, :], mod_ref[0, 0:1, :])
        scale = jnp.where(is_ctx, modc_ref[0, 1:2, :], mod_ref[0, 1:2, :])
        hs.append((_rms(xin, g_ref[...]) * (1.0 + scale) + shift).astype(BF16))
    h = jnp.concatenate(hs, axis=0)
    proj = jnp.dot(h, w_ref[...], preferred_element_type=F32)

    tt = h.shape[0]
    cosv = cos_ref[...]
    s1 = s1_ref[...]
    s2 = s2_ref[...]
    row = lax.broadcasted_iota(jnp.int32, (LANES, tt), 0)
    lo_rows = row < HEAD_DIM

    def rope(xs):
        return xs * cosv + pltpu.roll(xs, LANES - ROPE_PAIRS, 1) * s1 + pltpu.roll(xs, ROPE_PAIRS, 1) * s2

    def headnorm(xs, gain):
        sq = xs * xs
        hi = sq.astype(BF16)
        lo = (sq - hi.astype(F32)).astype(BF16)
        ss = (jnp.dot(hi, ones_ref[...], preferred_element_type=F32)
              + jnp.dot(lo, ones_ref[...], preferred_element_type=F32))
        return xs * lax.rsqrt(ss * (1.0 / HEAD_DIM) + EPS) * gain

    for s in range(N_QG):
        xs = proj[:, s * LANES:(s + 1) * LANES]
        if s >= A_W // LANES:
            xs = headnorm(xs, qg_ref[...])
        qst = (rope(xs) * Q_SCALE).T
        q_lo = jnp.where(lo_rows, qst, 0.0)
        q_hi = qst - q_lo
        for a in range(tt // tq):
            qt_ref[0, s, :, 2 * a * tq:(2 * a + 1) * tq] = q_lo[:, a * tq:(a + 1) * tq].astype(BF16)
            qt_ref[0, s, :, (2 * a + 1) * tq:(2 * a + 2) * tq] = q_hi[:, a * tq:(a + 1) * tq].astype(BF16)
    for s in range(N_KG):
        xs = proj[:, Q_W + s * LANES:Q_W + (s + 1) * LANES]
        if s >= A_W // LANES:
            xs = headnorm(xs, kg_ref[...])
        k_ref[0, s] = rope(xs).astype(BF16)
    for s in range(N_KG):
        vt_ref[0, s, 0:LANES, :] = proj[:, Q_W + K_W + s * LANES:Q_W + K_W + (s + 1) * LANES].T.astype(BF16)
        vt_ref[0, s, LANES:VT_ROWS, :] = jnp.ones((VT_ROWS - LANES, tt), BF16)


def _proj(x, ctx, mod, modc, g, w_perm, cos_t, s1_t, s2_t, qg, kg, ones_bd, tt, tq):
    b, s, d = x.shape
    nctx = ctx.shape[1]
    nct = nctx // tt
    nk = nctx + s
    nj = nk // tt
    ng = max(gsz for gsz in (1, 2, 3) if nj % gsz == 0)
    rows = ng * tt
    n_in = w_perm.shape[1]
    kern = functools.partial(_proj_kernel, nct, tq, ng)

    def x_spec(t):
        return pl.BlockSpec((1, tt, d), lambda bi, j: (bi, jnp.maximum(j * ng + t - nct, 0), 0))

    def ctx_spec(t):
        return pl.BlockSpec((1, tt, d), lambda bi, j: (bi, jnp.minimum(j * ng + t, nct - 1), 0))

    tab_spec = pl.BlockSpec((rows, LANES), lambda bi, j: (j, 0))
    return pl.pallas_call(
        kern,
        grid=(b, nj // ng),
        in_specs=[x_spec(t) for t in range(ng)] + [ctx_spec(t) for t in range(ng)]
        + [pl.BlockSpec((1, N_MOD, d), lambda bi, j: (bi, 0, 0)),
           _const_spec((1, N_MOD, d)),
           _const_spec((1, d)),
           _const_spec((d, n_in)),
           tab_spec, tab_spec, tab_spec,
           _const_spec((1, LANES)),
           _const_spec((1, LANES)),
           _const_spec((LANES, LANES))],
        out_specs=[pl.BlockSpec((1, N_QG, LANES, 2 * rows), lambda bi, j: (bi, 0, 0, j)),
                   pl.BlockSpec((1, N_KG, rows, LANES), lambda bi, j: (bi, 0, j, 0)),
                   pl.BlockSpec((1, N_KG, VT_ROWS, rows), lambda bi, j: (bi, 0, 0, j))],
        out_shape=[jax.ShapeDtypeStruct((b, N_QG, LANES, 2 * nk), BF16),
                   jax.ShapeDtypeStruct((b, N_KG, nk, LANES), BF16),
                   jax.ShapeDtypeStruct((b, N_KG, VT_ROWS, nk), BF16)],
        compiler_params=pltpu.CompilerParams(
            dimension_semantics=("parallel", "arbitrary"),
            vmem_limit_bytes=_vmem_limit(56 * 1024 * 1024)),
        name="proj",
    )(*([x] * ng), *([ctx] * ng), mod, modc, g, w_perm, cos_t, s1_t, s2_t, qg, kg, ones_bd)


def _attn_kernel(tq, tk, nsub, qt_ref, k_ref, vt_ref, lam_ref, subln_ref, o_ref, *bufs):
    nk = k_ref.shape[2]
    nc = nk // tk
    tq2 = 2 * tq
    row = lax.broadcasted_iota(jnp.int32, (LANES, tq), 0)
    lo_rows = row < HEAD_DIM
    lv = lam_ref[...]
    lam = (jnp.exp(jnp.sum(lv[0:1] * lv[1:2], axis=1, keepdims=True))
           - jnp.exp(jnp.sum(lv[2:3] * lv[3:4], axis=1, keepdims=True)) + LAM_INIT_L0)

    n_a = A_W // LANES

    s_bufs, p_bufs = bufs[:3], bufs[3:]

    def item(w):
        return divmod(w, N_QG)

    def score_chunk(w, c, m8):
        qi, u = item(w)
        kv = min(u, n_a)
        s_c = jnp.dot(k_ref[0, kv, c * tk:(c + 1) * tk, :], qt_ref[0, u, :, qi * tq2:(qi + 1) * tq2],
                      preferred_element_type=F32)
        s_bufs[w % 3][c * tk:(c + 1) * tk, :] = s_c
        mc = jnp.max(s_c.reshape(tk // ACC_ROWS, ACC_ROWS, tq2), axis=0)
        return mc if m8 is None else jnp.maximum(m8, mc)

    def exp_chunk(w, c, m):
        p_bufs[w % 2][c * tk:(c + 1) * tk, :] = jnp.exp2((s_bufs[w % 3][c * tk:(c + 1) * tk, :] - m).astype(BF16))

    def value_chunk(w, c, acc):
        if (c + 1) % VALUE_CHUNKS and c + 1 < nc:
            return acc
        c0 = c - c % VALUE_CHUNKS
        kv = min(item(w)[1], n_a)
        pv = jnp.dot(vt_ref[0, kv, :, c0 * tk:(c + 1) * tk], p_bufs[w % 2][c0 * tk:(c + 1) * tk, :],
                     preferred_element_type=F32)
        return pv if acc is None else acc + pv

    def finish(w, acc):
        qi, u = item(w)
        ot = acc[0:LANES] * (1.0 / acc[LANES:LANES + 1])
        o1, o2 = ot[:, :tq], ot[:, tq:]
        if u < n_a:
            ot = o1 - lam * o2
            ot = ot * lax.rsqrt(jnp.mean(ot * ot, axis=0, keepdims=True) + EPS)
            o_ref[0, u, qi * tq:(qi + 1) * tq, :] = (ot.T * (subln_ref[...] * (1.0 - LAM_INIT_L0))).astype(BF16)
        else:
            o_ref[0, u, qi * tq:(qi + 1) * tq, :] = jnp.where(lo_rows, o1, o2).T.astype(BF16)

    n_items = nsub * N_QG
    m8s = {}
    ms = {}
    accs = {}
    for t in range(n_items + 2):
        if 0 <= t - 1 < n_items:
            ms[t - 1] = jnp.max(m8s.pop(t - 1), axis=0, keepdims=True)
        hist = []
        for c in range(nc):
            hist.append(m8s.get(t) if t < n_items else accs.get(t - 2))
            pace = hist[c - PACE_LAG + 1] if c >= PACE_LAG else None
            pace = None if pace is None else pace[0:1]
            if t < n_items:
                m8s[t] = score_chunk(t, c, m8s.get(t))
            if 0 <= t - 1 < n_items:
                m = ms[t - 1]
                if pace is not None:
                    m = m + jnp.minimum(jnp.abs(pace), 0.0)
                exp_chunk(t - 1, c, m)
            if 0 <= t - 2 < n_items:
                accs[t - 2] = value_chunk(t - 2, c, accs.get(t - 2))
        if 0 <= t - 2 < n_items:
            finish(t - 2, accs.pop(t - 2))


def _attn(qt, k, vt, lam_vecs, subln, s, tq, tk, nsub):
    b = qt.shape[0]
    nk = k.shape[2]
    tqs = tq * nsub
    q0 = (nk - s) // tqs
    kern = functools.partial(_attn_kernel, tq, tk, nsub)
    return pl.pallas_call(
        kern,
        grid=(b, s // tqs),
        in_specs=[pl.BlockSpec((1, N_QG, LANES, 2 * tqs), lambda bi, i: (bi, 0, 0, i + q0)),
                  pl.BlockSpec((1, N_KG, nk, LANES), lambda bi, i: (bi, 0, 0, 0)),
                  pl.BlockSpec((1, N_KG, VT_ROWS, nk), lambda bi, i: (bi, 0, 0, 0)),
                  _const_spec((4, HEAD_DIM)),
                  _const_spec((1, LANES))],
        out_specs=pl.BlockSpec((1, N_QG, tqs, LANES), lambda bi, i: (bi, 0, i, 0)),
        out_shape=jax.ShapeDtypeStruct((b, N_QG, s, LANES), BF16),
        scratch_shapes=[pltpu.VMEM((nk, 2 * tq), F32)] * 3 + [pltpu.VMEM((nk, 2 * tq), BF16)] * 2,
        compiler_params=pltpu.CompilerParams(
            dimension_semantics=("parallel", "arbitrary"),
            vmem_limit_bytes=_vmem_limit(48 * 1024 * 1024)),
        name="attn",
    )(qt, k, vt, lam_vecs, subln)


def _swiglu(x_all, mod_ref, gffn_ref, wgu_ref, wd_ref):
    shift = mod_ref[0, 3:4, :]
    scale = mod_ref[0, 4:5, :]
    gate = mod_ref[0, 5:6, :]
    n = x_all.shape[0]
    nh = n // FFN_SPLIT
    outs = []
    for r0 in range(0, n, nh):
        x1 = x_all[r0:r0 + nh]
        h = _rms(x1, gffn_ref[...]) * (1.0 + scale) + shift
        gu = jnp.dot(h.astype(BF16), wgu_ref[...], preferred_element_type=F32)
        dff = gu.shape[1] // 2
        g = gu[:, :dff]
        u = gu[:, dff:]
        a = (g * jax.nn.sigmoid(g)) * u
        dn = jnp.dot(a.astype(BF16), wd_ref[...], preferred_element_type=F32)
        outs.append(x1 + gate * dn)
    return outs[0] if len(outs) == 1 else jnp.concatenate(outs, axis=0)


def _mixffn_kernel(o_ref, x_ref, mod_ref, wout_ref, gffn_ref, wgu_ref, wd_ref, out_ref, res_scr):
    o = jnp.concatenate([o_ref[0, u] for u in range(N_QG)], axis=1)
    y = jnp.dot(o, wout_ref[...], preferred_element_type=F32)
    x1 = x_ref[0] + mod_ref[0, 2:3, :] * y
    res = _swiglu(x1, mod_ref, gffn_ref, wgu_ref, wd_ref)
    n = res_scr.shape[1] // RADIX
    for sl in range(res_scr.shape[0]):
        res_scr[sl] = res[:, sl * LANES:(sl + 1) * LANES]
        for r in range(RADIX):
            out_ref[0, r, :, sl * LANES:(sl + 1) * LANES] = res_scr[sl, pl.ds(r, n, stride=RADIX), :]


def _ffn_final_kernel(x_ref, mod_ref, gffn_ref, wgu_ref, wd_ref, gfin_ref, out_ref, x_scr):
    n = x_scr.shape[1] // RADIX
    for sl in range(x_scr.shape[0]):
        for r in range(RADIX):
            x_scr[sl, pl.ds(r, n, stride=RADIX), :] = x_ref[0, r, :, sl * LANES:(sl + 1) * LANES]
    x1 = jnp.concatenate([x_scr[sl] for sl in range(x_scr.shape[0])], axis=1)
    x2 = _swiglu(x1, mod_ref, gffn_ref, wgu_ref, wd_ref)
    out_ref[0] = _rms(x2, gfin_ref[...])


def _mixffn(o, x, mod, wout, gffn, wgu, wd, tm):
    b, s, d = x.shape
    tok = lambda bi, i: (bi, i, 0)
    return pl.pallas_call(
        _mixffn_kernel,
        grid=(b, s // tm),
        in_specs=[pl.BlockSpec((1, N_QG, tm, LANES), lambda bi, i: (bi, 0, i, 0)),
                  pl.BlockSpec((1, tm, d), tok),
                  pl.BlockSpec((1, N_MOD, d), lambda bi, i: (bi, 0, 0)),
                  _const_spec(wout.shape), _const_spec((1, d)), _const_spec(wgu.shape), _const_spec(wd.shape)],
        out_specs=pl.BlockSpec((1, RADIX, tm // RADIX, d), lambda bi, i: (bi, 0, i, 0)),
        out_shape=jax.ShapeDtypeStruct((b, RADIX, s // RADIX, d), F32),
        scratch_shapes=[pltpu.VMEM((d // LANES, tm, LANES), F32)],
        compiler_params=pltpu.CompilerParams(
            dimension_semantics=("parallel", "arbitrary"),
            vmem_limit_bytes=_vmem_limit(56 * 1024 * 1024)),
        name="mixffn",
    )(o, x, mod, wout, gffn, wgu, wd)


def _ffn_final(xr, mod, gffn, wgu, wd, gfin, tm):
    b, _, nq, d = xr.shape
    s = nq * RADIX
    return pl.pallas_call(
        _ffn_final_kernel,
        grid=(b, s // tm),
        in_specs=[pl.BlockSpec((1, RADIX, tm // RADIX, d), lambda bi, i: (bi, 0, i, 0)),
                  pl.BlockSpec((1, N_MOD, d), lambda bi, i: (bi, 0, 0)),
                  _const_spec((1, d)), _const_spec(wgu.shape), _const_spec(wd.shape), _const_spec((1, d))],
        out_specs=pl.BlockSpec((1, tm, d), lambda bi, i: (bi, i, 0)),
        out_shape=jax.ShapeDtypeStruct((b, s, d), F32),
        scratch_shapes=[pltpu.VMEM((d // LANES, tm, LANES), F32)],
        compiler_params=pltpu.CompilerParams(
            dimension_semantics=("parallel", "arbitrary"),
            vmem_limit_bytes=_vmem_limit(56 * 1024 * 1024)),
        name="ffn",
    )(xr, mod, gffn, wgu, wd, gfin)


_QUARTER = ((1, 0), (0, 1), (-1, 0), (0, -1))


def _fproj_kernel(x0_ref, x1_ref, x2_ref, x3_ref, mod_ref, g_ref, w_ref, u_ref):
    d = x0_ref.shape[2]
    ta = x0_ref.shape[1]
    shift = mod_ref[0, 0:1, :]
    scale1 = 1.0 + mod_ref[0, 1:2, :]
    h = jnp.concatenate([(_rms(xr[0], g_ref[...]) * scale1 + shift).astype(BF16)
                         for xr in (x0_ref, x1_ref, x2_ref, x3_ref)], axis=0)
    z = jnp.dot(h, w_ref[...], preferred_element_type=F32)
    zc = [z[m * ta:(m + 1) * ta, 0:d] for m in range(RADIX)]
    zs = [z[m * ta:(m + 1) * ta, d:2 * d] for m in range(RADIX)]
    for r in range(RADIX):
        ua = None
        ub = None
        for m in range(RADIX):
            cp, sp = _QUARTER[(r * m) % 4]
            ta_, sa = (zc[m], cp) if cp else (zs[m], -sp)
            tb_, sb = (zs[m], cp) if cp else (zc[m], sp)
            ua = sa * ta_ if ua is None else (ua + ta_ if sa > 0 else ua - ta_)
            ub = sb * tb_ if ub is None else (ub + tb_ if sb > 0 else ub - tb_)
        u_ref[0, r, 0] = ua.astype(BF16)
        u_ref[0, r, 1] = ub.astype(BF16)


def _fproj(x, mod, g, wcs, ta):
    b, s, d = x.shape
    nq = s // RADIX
    nqq = nq // RADIX
    na = nqq // ta

    def x_spec(m):
        return pl.BlockSpec((1, ta, d), lambda bi, e, a: (bi, e * (nq // ta) + m * na + a, 0))

    return pl.pallas_call(
        _fproj_kernel,
        grid=(b, RADIX, na),
        in_specs=[x_spec(0), x_spec(1), x_spec(2), x_spec(3),
                  pl.BlockSpec((1, N_MOD, d), lambda bi, e, a: (bi, 0, 0)),
                  _const_spec((1, d)), _const_spec(wcs.shape)],
        out_specs=pl.BlockSpec((1, RADIX, 2, ta, d), lambda bi, e, a: (bi, 0, 0, e * na + a, 0)),
        out_shape=jax.ShapeDtypeStruct((b, RADIX, 2, nq, d), BF16),
        compiler_params=pltpu.CompilerParams(
            dimension_semantics=("parallel", "arbitrary", "arbitrary"),
            vmem_limit_bytes=_vmem_limit(40 * 1024 * 1024)),
        name="fproj",
    )(x, x, x, x, mod, g, wcs)


def _fseq_kernel(u_ref, t_ref, x_ref, mod_ref, out_ref):
    nq, d = u_ref.shape[3], u_ref.shape[4]
    u = u_ref[0, 0].reshape(2 * nq, d)
    y = jnp.dot(t_ref[0], u, preferred_element_type=F32)
    out_ref[0] = x_ref[0] + mod_ref[0, 2:3, :] * y


def _seq_dft_tables(n):
    nq = n // RADIX
    nqq = nq // RADIX
    u = np.arange(nq)
    q = (RADIX * (u % nqq) + u // nqq)[None, None, :]
    p = np.arange(nq)[None, :, None]
    r = np.arange(RADIX)[:, None, None]
    ang = 2.0 * np.pi * (((RADIX * p + r) * q) % n) / n
    t = np.concatenate([np.cos(ang), -np.sin(ang)], axis=2) / np.sqrt(n)
    return jnp.asarray(t, F32)


def _fseq(u, t, x, mod, tmr):
    b, s, d = x.shape
    nq = s // RADIX
    nblk = nq // tmr
    return pl.pallas_call(
        _fseq_kernel,
        grid=(b, RADIX, nblk),
        in_specs=[pl.BlockSpec((1, 1, 2, nq, d), lambda bi, r, i: (bi, r, 0, 0, 0)),
                  pl.BlockSpec((1, tmr, 2 * nq), lambda bi, r, i: (r, i, 0)),
                  pl.BlockSpec((1, tmr, d), lambda bi, r, i: (bi, r * nblk + i, 0)),
                  pl.BlockSpec((1, N_MOD, d), lambda bi, r, i: (bi, 0, 0))],
        out_specs=pl.BlockSpec((1, tmr, d), lambda bi, r, i: (bi, r * nblk + i, 0)),
        out_shape=jax.ShapeDtypeStruct((b, s, d), F32),
        compiler_params=pltpu.CompilerParams(
            dimension_semantics=("parallel", "arbitrary", "arbitrary"),
            vmem_limit_bytes=_vmem_limit(40 * 1024 * 1024)),
        name="fseq",
    )(u, t, x, mod)


def _rope_lane_tables(n_ctx, n_tok):
    rows_count = n_tok // GRID_W
    row = jnp.repeat(jnp.arange(rows_count, dtype=jnp.int32), GRID_W).astype(F32)
    col = jnp.tile(jnp.arange(GRID_W, dtype=jnp.int32), rows_count).astype(F32)
    inv = ROPE_THETA ** (-jnp.arange(ROPE_PAIRS, dtype=F32) / ROPE_PAIRS)
    lane = np.arange(LANES)
    dd = lane % HEAD_DIM
    axis = dd // (2 * ROPE_PAIRS)
    half = (dd % (2 * ROPE_PAIRS)) // ROPE_PAIRS
    pair = dd % ROPE_PAIRS
    pos = jnp.where(jnp.asarray(axis == 0)[None, :], row[:, None], col[:, None])
    ang = pos * inv[jnp.asarray(pair)][None, :]
    cosv = jnp.cos(ang)
    sinv = jnp.sin(ang)
    first = jnp.asarray(half == 0)[None, :]
    s1 = jnp.where(first, -sinv, 0.0)
    s2 = jnp.where(first, 0.0, sinv)
    pad1 = jnp.ones((n_ctx, LANES), F32)
    pad0 = jnp.zeros((n_ctx, LANES), F32)
    return (jnp.concatenate([pad1, cosv], 0), jnp.concatenate([pad0, s1], 0), jnp.concatenate([pad0, s2], 0))


def _permute_w_in(w_in):
    off_ak = A_W
    off_av = 2 * A_W
    off_bq = 3 * A_W
    off_bk = off_bq + BQ_W
    off_bv = off_bk + BKV_W
    grp = GQA_HEADS // GQA_KV_HEADS
    bq_cols = []
    for p in range(grp):
        for kvh in range(GQA_KV_HEADS):
            hq = kvh * grp + p
            bq_cols.append(np.arange(off_bq + hq * HEAD_DIM, off_bq + (hq + 1) * HEAD_DIM))
    cols = np.concatenate([np.arange(0, A_W)] + bq_cols
                          + [np.arange(off_ak, off_av), np.arange(off_bk, off_bv),
                             np.arange(off_av, off_bq), np.arange(off_bv, off_bv + BKV_W)])
    return w_in[:, cols], np.concatenate(bq_cols) - off_bq


def kernel(x, c, ctx, c_ctx, l0_ada_w, l0_ada_b, l0_norm_mix, l0_w_in, l0_lambda_q1, l0_lambda_k1, l0_lambda_q2, l0_lambda_k2, l0_subln, l0_q_norm, l0_k_norm, l0_w_out, l0_norm_ffn, l0_w_gate_up, l0_w_down, l1_ada_w, l1_ada_b, l1_norm_mix, l1_w_out, l1_norm_ffn, l1_w_gate_up, l1_w_down, final_norm):
    b, s, d = x.shape
    n_ctx = ctx.shape[1]
    assert d == D_MODEL and s % GRID_W == 0

    tt = min(256, n_ctx, s)
    tq = min(128, tt)
    nsub = 2 if n_ctx % (2 * tq) == 0 and s % (2 * tq) == 0 else 1
    tk = min(256, n_ctx)
    tm = min(512, s)
    tmr = min(512, s // RADIX)
    ta = min(128, s // RADIX ** 2)
    assert n_ctx % tt == 0 and s % tt == 0 and tt % tq == 0 and (n_ctx + s) % tk == 0
    assert s % tm == 0 and (s // RADIX) % tmr == 0 and (s // RADIX ** 2) % ta == 0

    n_rows = -(-(b + 1) // SUBLANES) * SUBLANES
    cond = jnp.concatenate([c, c_ctx[None, :], jnp.zeros((n_rows - b - 1, d), F32)], axis=0)
    m0, m1 = _ada(cond, l0_ada_w, l0_ada_b, l1_ada_w, l1_ada_b)
    mod0 = m0[:b].reshape(b, N_MOD, d)
    mod0c = m0[b:b + 1].reshape(1, N_MOD, d)
    mod1 = m1[:b].reshape(b, N_MOD, d)

    w_perm, bq_perm = _permute_w_in(l0_w_in)
    cos_t, s1_t, s2_t = _rope_lane_tables(n_ctx, s)
    qg = jnp.tile(l0_q_norm, LANES // HEAD_DIM).reshape(1, LANES)
    kg = jnp.tile(l0_k_norm, LANES // HEAD_DIM).reshape(1, LANES)
    li = np.arange(LANES)
    ones_bd = jnp.asarray((li[:, None] // HEAD_DIM) == (li[None, :] // HEAD_DIM), BF16)
    qt, k, vt = _proj(x, ctx, mod0, mod0c, l0_norm_mix.reshape(1, d), w_perm.astype(BF16),
                      cos_t, s1_t, s2_t, qg, kg, ones_bd, tt, tq)
    lam_vecs = jnp.stack([l0_lambda_q1, l0_lambda_k1, l0_lambda_q2, l0_lambda_k2]).astype(F32)
    o = _attn(qt, k, vt, lam_vecs, l0_subln.reshape(1, LANES), s, tq, tk, nsub)
    w_out_rows = np.concatenate([np.arange(A_W), A_W + bq_perm])
    xr = _mixffn(o, x, mod0, l0_w_out[w_out_rows].astype(BF16), l0_norm_ffn.reshape(1, d),
                 l0_w_gate_up.astype(BF16), l0_w_down.astype(BF16), tm)
    xr = xr.reshape(b, s, d)

    wcs = _fold_channel_dft(l1_w_out)
    u = _fproj(xr, mod1, l1_norm_mix.reshape(1, d), wcs, ta)
    xr = _fseq(u, _seq_dft_tables(s).astype(BF16), xr, mod1, tmr)
    return _ffn_final(xr.reshape(b, RADIX, s // RADIX, d), mod1, l1_norm_ffn.reshape(1, d),
                      l1_w_gate_up.astype(BF16), l1_w_down.astype(BF16), final_norm.reshape(1, d), tm)
```

```python
import functools
import math

import numpy as np
import jax
import jax.numpy as jnp
from jax import lax
from jax.experimental import pallas as pl
from jax.experimental.pallas import tpu as pltpu

F32 = jnp.float32
BF16 = jnp.bfloat16
HIGHEST = lax.Precision.HIGHEST

LANES = 128
SUBLANES = 8
V7X_VMEM_BYTES = 64 * 1024 * 1024

D_MODEL = 1024
GRID_W = 64
DIFF_HEADS = 4
DIFF_HEAD_DIM = 64
GQA_HEADS = 8
GQA_KV_HEADS = 2
GQA_HEAD_DIM = 64
ROPE_THETA = 10000.0
ROPE_PAIRS = GQA_HEAD_DIM // 4
FOURIER_GROUPS = 8
FOURIER_GROUP_DIM = D_MODEL // FOURIER_GROUPS
EPS = 1e-6
N_MOD = 6
HEAD_DIM = 64
A_W = DIFF_HEADS * 2 * DIFF_HEAD_DIM
BQ_W = GQA_HEADS * GQA_HEAD_DIM
BKV_W = GQA_KV_HEADS * GQA_HEAD_DIM
Q_W = A_W + BQ_W
K_W = A_W + BKV_W
N_QG = Q_W // LANES
N_KG = K_W // LANES
LAM_INIT_L0 = 0.8 - 0.6 * math.exp(-0.3 * 0)
Q_SCALE = HEAD_DIM ** -0.5 * math.log2(math.e)
RADIX = 4
ACC_ROWS = 32
PACE_LAG = 1
FFN_SPLIT = 2
VALUE_CHUNKS = 1
VT_ROWS = LANES + 16


def _vmem_limit(nbytes):
    return int(min(nbytes, V7X_VMEM_BYTES - 4 * 1024 * 1024))


def _rms(x, g):
    ms = jnp.mean(x * x, axis=-1, keepdims=True)
    return x * lax.rsqrt(ms + EPS) * g


def _const_spec(shape):
    nd = len(shape)
    return pl.BlockSpec(shape, lambda *_: (0,) * nd, pipeline_mode=pl.Buffered(1))


def _fold_kernel(cd_ref, sd_ref, w_ref, o_ref):
    w = w_ref[...]
    d = w.shape[1]
    o_ref[:, 0:d] = jnp.dot(cd_ref[...], w, preferred_element_type=F32, precision=HIGHEST).astype(BF16)
    o_ref[:, d:2 * d] = jnp.dot(sd_ref[...], w, preferred_element_type=F32, precision=HIGHEST).astype(BF16)


def _fold_channel_dft(w_out):
    d = w_out.shape[0]
    gd = FOURIER_GROUP_DIM
    idx = np.arange(gd)
    ang = 2.0 * np.pi * ((idx[:, None] * idx[None, :]) % gd) / gd
    cd = jnp.asarray(np.cos(ang) / np.sqrt(gd), F32)
    sd = jnp.asarray(np.sin(ang) / np.sqrt(gd), F32)
    return pl.pallas_call(
        _fold_kernel,
        grid=(d // gd,),
        in_specs=[pl.BlockSpec((gd, gd), lambda g: (0, 0)),
                  pl.BlockSpec((gd, gd), lambda g: (0, 0)),
                  pl.BlockSpec((gd, d), lambda g: (g, 0))],
        out_specs=pl.BlockSpec((gd, 2 * d), lambda g: (g, 0)),
        out_shape=jax.ShapeDtypeStruct((d, 2 * d), BF16),
        name="fold",
    )(cd, sd, w_out)


def _ada_kernel(c_ref, w0_ref, b0_ref, w1_ref, b1_ref, o0_ref, o1_ref):
    cv = c_ref[...]
    a = (cv * jax.nn.sigmoid(cv)).astype(BF16)
    o0_ref[...] = jnp.dot(a, w0_ref[...].astype(BF16), preferred_element_type=F32) + b0_ref[...]
    o1_ref[...] = jnp.dot(a, w1_ref[...].astype(BF16), preferred_element_type=F32) + b1_ref[...]


def _ada(cond, w0, b0, w1, b1):
    r, d = cond.shape
    n = w0.shape[1]
    bn = 512
    return pl.pallas_call(
        _ada_kernel,
        grid=(n // bn,),
        in_specs=[pl.BlockSpec((r, d), lambda j: (0, 0)),
                  pl.BlockSpec((d, bn), lambda j: (0, j)),
                  pl.BlockSpec((1, bn), lambda j: (0, j)),
                  pl.BlockSpec((d, bn), lambda j: (0, j)),
                  pl.BlockSpec((1, bn), lambda j: (0, j))],
        out_specs=[pl.BlockSpec((r, bn), lambda j: (0, j)),
                   pl.BlockSpec((r, bn), lambda j: (0, j))],
        out_shape=[jax.ShapeDtypeStruct((r, n), F32), jax.ShapeDtypeStruct((r, n), F32)],
        name="ada",
    )(cond, w0, b0.reshape(1, n), w1, b1.reshape(1, n))


def _proj_kernel(nct, tq, ng, *refs):
    x_refs, ctx_refs = refs[:ng], refs[ng:2 * ng]
    (mod_ref, modc_ref, g_ref, w_ref, cos_ref, s1_ref, s2_ref,
     qg_ref, kg_ref, ones_ref, qt_ref, k_ref, vt_ref) = refs[2 * ng:]
    j = pl.program_id(1)
    gain_x = g_ref[...] * (1.0 + mod_ref[0, 1:2, :])
    gain_c = g_ref[...] * (1.0 + modc_ref[0, 1:2, :])
    hs = []
    for t in range(ng):
        if t >= nct:
            xin, gain, shift = x_refs[t][0], gain_x, mod_ref[0, 0:1, :]
        else:
            is_ctx = j * ng + t < nct
            xin = jnp.where(is_ctx, ctx_refs[t][0], x_refs[t][0])
            gain = jnp.where(is_ctx, gain_c, gain_x)
            shift = jnp.where(is_ctx, modc_ref[0, 0:1, :], mod_ref[0, 0:1, :])
        ms = jnp.mean(xin * xin, axis=-1, keepdims=True)
        hs.append((xin * lax.rsqrt(ms + EPS) * gain + shift).astype(BF16))
    h = jnp.concatenate(hs, axis=0)

    tt = h.shape[0]
    cosv = cos_ref[...]
    s1 = s1_ref[...]
    s2 = s2_ref[...]
    row = lax.broadcasted_iota(jnp.int32, (LANES, tt), 0)
    lo_rows = row < HEAD_DIM

    def rope(xs):
        return xs * cosv + pltpu.roll(xs, LANES - ROPE_PAIRS, 1) * s1 + pltpu.roll(xs, ROPE_PAIRS, 1) * s2

    def headnorm(xs, gain):
        sq = xs * xs
        hi = sq.astype(BF16)
        lo = (sq - hi.astype(F32)).astype(BF16)
        ss = (jnp.dot(hi, ones_ref[...], preferred_element_type=F32)
              + jnp.dot(lo, ones_ref[...], preferred_element_type=F32))
        return xs * lax.rsqrt(ss * (1.0 / HEAD_DIM) + EPS) * gain

    def finish_slab(idx, xs):
        if idx < N_QG:
            s = idx
            if s >= A_W // LANES:
                xs = headnorm(xs, qg_ref[...])
            qst = (rope(xs) * Q_SCALE).T
            q_lo = jnp.where(lo_rows, qst, 0.0)
            q_hi = qst - q_lo
            for a in range(tt // tq):
                qt_ref[0, s, :, 2 * a * tq:(2 * a + 1) * tq] = q_lo[:, a * tq:(a + 1) * tq].astype(BF16)
                qt_ref[0, s, :, (2 * a + 1) * tq:(2 * a + 2) * tq] = q_hi[:, a * tq:(a + 1) * tq].astype(BF16)
        elif idx < N_QG + N_KG:
            s = idx - N_QG
            if s >= A_W // LANES:
                xs = headnorm(xs, kg_ref[...])
            k_ref[0, s] = rope(xs).astype(BF16)
        else:
            s = idx - N_QG - N_KG
            vt_ref[0, s, 0:LANES, :] = xs.T.astype(BF16)
            vt_ref[0, s, LANES:VT_ROWS, :] = jnp.ones((VT_ROWS - LANES, tt), BF16)

    gw = 2 * LANES
    n_groups = w_ref.shape[1] // gw
    pending = None
    for gi in range(n_groups + 1):
        z = None
        if gi < n_groups:
            z = jnp.dot(h, w_ref[:, gi * gw:(gi + 1) * gw], preferred_element_type=F32)
        if pending is not None:
            pg, pz = pending
            for half in range(gw // LANES):
                finish_slab(pg * (gw // LANES) + half, pz[:, half * LANES:(half + 1) * LANES])
        pending = (gi, z)


def _proj(x, ctx, mod, modc, g, w_perm, cos_t, s1_t, s2_t, qg, kg, ones_bd, tt, tq):
    b, s, d = x.shape
    nctx = ctx.shape[1]
    nct = nctx // tt
    nk = nctx + s
    nj = nk // tt
    ng = max(gsz for gsz in (1, 2, 3) if nj % gsz == 0)
    rows = ng * tt
    n_in = w_perm.shape[1]
    kern = functools.partial(_proj_kernel, nct, tq, ng)

    def x_spec(t):
        return pl.BlockSpec((1, tt, d), lambda bi, j: (bi, jnp.maximum(j * ng + t - nct, 0), 0))

    def ctx_spec(t):
        return pl.BlockSpec((1, tt, d), lambda bi, j: (bi, jnp.minimum(j * ng + t, nct - 1), 0))

    tab_spec = pl.BlockSpec((rows, LANES), lambda bi, j: (j, 0))
    return pl.pallas_call(
        kern,
        grid=(b, nj // ng),
        in_specs=[x_spec(t) for t in range(ng)] + [ctx_spec(t) for t in range(ng)]
        + [pl.BlockSpec((1, N_MOD, d), lambda bi, j: (bi, 0, 0)),
           _const_spec((1, N_MOD, d)),
           _const_spec((1, d)),
           _const_spec((d, n_in)),
           tab_spec, tab_spec, tab_spec,
           _const_spec((1, LANES)),
           _const_spec((1, LANES)),
           _const_spec((LANES, LANES))],
        out_specs=[pl.BlockSpec((1, N_QG, LANES, 2 * rows), lambda bi, j: (bi, 0, 0, j)),
                   pl.BlockSpec((1, N_KG, rows, LANES), lambda bi, j: (bi, 0, j, 0)),
                   pl.BlockSpec((1, N_KG, VT_ROWS, rows), lambda bi, j: (bi, 0, 0, j))],
        out_shape=[jax.ShapeDtypeStruct((b, N_QG, LANES, 2 * nk), BF16),
                   jax.ShapeDtypeStruct((b, N_KG, nk, LANES), BF16),
                   jax.ShapeDtypeStruct((b, N_KG, VT_ROWS, nk), BF16)],
        compiler_params=pltpu.CompilerParams(
            dimension_semantics=("parallel", "arbitrary"),
            vmem_limit_bytes=_vmem_limit(56 * 1024 * 1024)),
        name="proj",
    )(*([x] * ng), *([ctx] * ng), mod, modc, g, w_perm, cos_t, s1_t, s2_t, qg, kg, ones_bd)


def _attn_kernel(tq, tk, nsub, qt_ref, k_ref, vt_ref, lam_ref, subln_ref, o_ref, *bufs):
    nk = k_ref.shape[2]
    nc = nk // tk
    tq2 = 2 * tq
    row = lax.broadcasted_iota(jnp.int32, (LANES, tq), 0)
    lo_rows = row < HEAD_DIM
    lv = lam_ref[...]
    lam = (jnp.exp(jnp.sum(lv[0:1] * lv[1:2], axis=1, keepdims=True))
           - jnp.exp(jnp.sum(lv[2:3] * lv[3:4], axis=1, keepdims=True)) + LAM_INIT_L0)

    n_a = A_W // LANES

    s_bufs, p_bufs = bufs[:3], bufs[3:]

    def item(w):
        return divmod(w, N_QG)

    def score_chunk(w, c, m8):
        qi, u = item(w)
        kv = min(u, n_a)
        s_c = jnp.dot(k_ref[0, kv, c * tk:(c + 1) * tk, :], qt_ref[0, u, :, qi * tq2:(qi + 1) * tq2],
                      preferred_element_type=F32)
        s_bufs[w % 3][c * tk:(c + 1) * tk, :] = s_c
        mc = jnp.max(s_c.reshape(tk // ACC_ROWS, ACC_ROWS, tq2), axis=0)
        return mc if m8 is None else jnp.maximum(m8, mc)

    def exp_chunk(w, c, m):
        p_bufs[w % 2][c * tk:(c + 1) * tk, :] = jnp.exp2((s_bufs[w % 3][c * tk:(c + 1) * tk, :] - m).astype(BF16))

    def value_chunk(w, c, acc):
        if (c + 1) % VALUE_CHUNKS and c + 1 < nc:
            return acc
        c0 = c - c % VALUE_CHUNKS
        kv = min(item(w)[1], n_a)
        pv = jnp.dot(vt_ref[0, kv, :, c0 * tk:(c + 1) * tk], p_bufs[w % 2][c0 * tk:(c + 1) * tk, :],
                     preferred_element_type=F32)
        return pv if acc is None else acc + pv

    def finish(w, acc):
        qi, u = item(w)
        ot = acc[0:LANES] * (1.0 / acc[LANES:LANES + 1])
        o1, o2 = ot[:, :tq], ot[:, tq:]
        if u < n_a:
            ot = o1 - lam * o2
            ot = ot * lax.rsqrt(jnp.mean(ot * ot, axis=0, keepdims=True) + EPS)
            o_ref[0, u, qi * tq:(qi + 1) * tq, :] = (ot.T * (subln_ref[...] * (1.0 - LAM_INIT_L0))).astype(BF16)
        else:
            o_ref[0, u, qi * tq:(qi + 1) * tq, :] = jnp.where(lo_rows, o1, o2).T.astype(BF16)

    n_items = nsub * N_QG
    m8s = {}
    ms = {}
    accs = {}
    for t in range(n_items + 2):
        if 0 <= t - 1 < n_items:
            ms[t - 1] = jnp.max(m8s.pop(t - 1), axis=0, keepdims=True)
        hist = []
        for c in range(nc):
            hist.append(m8s.get(t) if t < n_items else accs.get(t - 2))
            pace = hist[c - PACE_LAG + 1] if c >= PACE_LAG else None
            pace = None if pace is None else pace[0:1]
            if t < n_items:
                m8s[t] = score_chunk(t, c, m8s.get(t))
            if 0 <= t - 1 < n_items:
                m = ms[t - 1]
                if pace is not None:
                    m = m + jnp.minimum(jnp.abs(pace), 0.0)
                exp_chunk(t - 1, c, m)
            if 0 <= t - 2 < n_items:
                accs[t - 2] = value_chunk(t - 2, c, accs.get(t - 2))
        if 0 <= t - 2 < n_items:
            finish(t - 2, accs.pop(t - 2))


def _attn(qt, k, vt, lam_vecs, subln, s, tq, tk, nsub):
    b = qt.shape[0]
    nk = k.shape[2]
    tqs = tq * nsub
    q0 = (nk - s) // tqs
    kern = functools.partial(_attn_kernel, tq, tk, nsub)
    return pl.pallas_call(
        kern,
        grid=(b, s // tqs),
        in_specs=[pl.BlockSpec((1, N_QG, LANES, 2 * tqs), lambda bi, i: (bi, 0, 0, i + q0)),
                  pl.BlockSpec((1, N_KG, nk, LANES), lambda bi, i: (bi, 0, 0, 0)),
                  pl.BlockSpec((1, N_KG, VT_ROWS, nk), lambda bi, i: (bi, 0, 0, 0)),
                  _const_spec((4, HEAD_DIM)),
                  _const_spec((1, LANES))],
        out_specs=pl.BlockSpec((1, N_QG, tqs, LANES), lambda bi, i: (bi, 0, i, 0)),
        out_shape=jax.ShapeDtypeStruct((b, N_QG, s, LANES), BF16),
        scratch_shapes=[pltpu.VMEM((nk, 2 * tq), F32)] * 3 + [pltpu.VMEM((nk, 2 * tq), BF16)] * 2,
        compiler_params=pltpu.CompilerParams(
            dimension_semantics=("parallel", "arbitrary"),
            vmem_limit_bytes=_vmem_limit(48 * 1024 * 1024)),
        name="attn",
    )(qt, k, vt, lam_vecs, subln)


def _swiglu(x_all, mod_ref, gffn_ref, wgu_ref, wd_ref):
    shift = mod_ref[0, 3:4, :]
    gain = gffn_ref[...] * (1.0 + mod_ref[0, 4:5, :])
    gate = mod_ref[0, 5:6, :]
    n = x_all.shape[0]
    nh = n // FFN_SPLIT
    outs = []
    for r0 in range(0, n, nh):
        x1 = x_all[r0:r0 + nh]
        h = _rms(x1, gain) + shift
        gu = jnp.dot(h.astype(BF16), wgu_ref[...], preferred_element_type=F32)
        dff = gu.shape[1] // 2
        g = gu[:, :dff]
        u = gu[:, dff:]
        a = (g * jax.nn.sigmoid(g)) * u
        dn = jnp.dot(a.astype(BF16), wd_ref[...], preferred_element_type=F32)
        outs.append(x1 + gate * dn)
    return outs[0] if len(outs) == 1 else jnp.concatenate(outs, axis=0)


def _mixffn_kernel(o_ref, x_ref, mod_ref, wout_ref, gffn_ref, wgu_ref, wd_ref, out_ref, res_scr):
    o = jnp.concatenate([o_ref[0, u] for u in range(N_QG)], axis=1)
    y = jnp.dot(o, wout_ref[...], preferred_element_type=F32)
    x1 = x_ref[0] + mod_ref[0, 2:3, :] * y
    res = _swiglu(x1, mod_ref, gffn_ref, wgu_ref, wd_ref)
    n = res_scr.shape[1] // RADIX
    for sl in range(res_scr.shape[0]):
        res_scr[sl] = res[:, sl * LANES:(sl + 1) * LANES]
        for r in range(RADIX):
            out_ref[0, r, :, sl * LANES:(sl + 1) * LANES] = res_scr[sl, pl.ds(r, n, stride=RADIX), :]


def _ffn_final_kernel(x_ref, mod_ref, gffn_ref, wgu_ref, wd_ref, gfin_ref, out_ref, x_scr):
    n = x_scr.shape[1] // RADIX
    for sl in range(x_scr.shape[0]):
        for r in range(RADIX):
            x_scr[sl, pl.ds(r, n, stride=RADIX), :] = x_ref[0, r, :, sl * LANES:(sl + 1) * LANES]
    x1 = jnp.concatenate([x_scr[sl] for sl in range(x_scr.shape[0])], axis=1)
    x2 = _swiglu(x1, mod_ref, gffn_ref, wgu_ref, wd_ref)
    out_ref[0] = _rms(x2, gfin_ref[...])


def _mixffn(o, x, mod, wout, gffn, wgu, wd, tm):
    b, s, d = x.shape
    tok = lambda bi, i: (bi, i, 0)
    return pl.pallas_call(
        _mixffn_kernel,
        grid=(b, s // tm),
        in_specs=[pl.BlockSpec((1, N_QG, tm, LANES), lambda bi, i: (bi, 0, i, 0)),
                  pl.BlockSpec((1, tm, d), tok),
                  pl.BlockSpec((1, N_MOD, d), lambda bi, i: (bi, 0, 0)),
                  _const_spec(wout.shape), _const_spec((1, d)), _const_spec(wgu.shape), _const_spec(wd.shape)],
        out_specs=pl.BlockSpec((1, RADIX, tm // RADIX, d), lambda bi, i: (bi, 0, i, 0)),
        out_shape=jax.ShapeDtypeStruct((b, RADIX, s // RADIX, d), F32),
        scratch_shapes=[pltpu.VMEM((d // LANES, tm, LANES), F32)],
        compiler_params=pltpu.CompilerParams(
            dimension_semantics=("parallel", "arbitrary"),
            vmem_limit_bytes=_vmem_limit(56 * 1024 * 1024)),
        name="mixffn",
    )(o, x, mod, wout, gffn, wgu, wd)


def _ffn_final(xr, mod, gffn, wgu, wd, gfin, tm):
    b, _, nq, d = xr.shape
    s = nq * RADIX
    return pl.pallas_call(
        _ffn_final_kernel,
        grid=(b, s // tm),
        in_specs=[pl.BlockSpec((1, RADIX, tm // RADIX, d), lambda bi, i: (bi, 0, i, 0)),
                  pl.BlockSpec((1, N_MOD, d), lambda bi, i: (bi, 0, 0)),
                  _const_spec((1, d)), _const_spec(wgu.shape), _const_spec(wd.shape), _const_spec((1, d))],
        out_specs=pl.BlockSpec((1, tm, d), lambda bi, i: (bi, i, 0)),
        out_shape=jax.ShapeDtypeStruct((b, s, d), F32),
        scratch_shapes=[pltpu.VMEM((d // LANES, tm, LANES), F32)],
        compiler_params=pltpu.CompilerParams(
            dimension_semantics=("parallel", "arbitrary"),
            vmem_limit_bytes=_vmem_limit(56 * 1024 * 1024)),
        name="ffn",
    )(xr, mod, gffn, wgu, wd, gfin)


_QUARTER = ((1, 0), (0, 1), (-1, 0), (0, -1))


def _fproj_kernel(ng, *refs):
    x_refs = refs[:RADIX * ng]
    mod_ref, g_ref, w_ref, u_ref = refs[RADIX * ng:]
    d = x_refs[0].shape[2]
    ta = x_refs[0].shape[1]
    shift = mod_ref[0, 0:1, :]
    gain = g_ref[...] * (1.0 + mod_ref[0, 1:2, :])
    for t in range(ng):
        h = jnp.concatenate([(_rms(xr[0], gain) + shift).astype(BF16)
                             for xr in x_refs[RADIX * t:RADIX * (t + 1)]], axis=0)
        z = jnp.dot(h, w_ref[...], preferred_element_type=F32)
        zc = [z[m * ta:(m + 1) * ta, 0:d] for m in range(RADIX)]
        zs = [z[m * ta:(m + 1) * ta, d:2 * d] for m in range(RADIX)]
        for r in range(RADIX):
            ua = None
            ub = None
            for m in range(RADIX):
                cp, sp = _QUARTER[(r * m) % 4]
                ta_, sa = (zc[m], cp) if cp else (zs[m], -sp)
                tb_, sb = (zs[m], cp) if cp else (zc[m], sp)
                ua = sa * ta_ if ua is None else (ua + ta_ if sa > 0 else ua - ta_)
                ub = sb * tb_ if ub is None else (ub + tb_ if sb > 0 else ub - tb_)
            u_ref[0, r, 0, t * ta:(t + 1) * ta, :] = ua.astype(BF16)
            u_ref[0, r, 1, t * ta:(t + 1) * ta, :] = ub.astype(BF16)


def _fproj(x, mod, g, wcs, ta):
    b, s, d = x.shape
    nq = s // RADIX
    nqq = nq // RADIX
    na = nqq // ta
    nrb = RADIX * na
    ng = 2

    def x_spec(t, m):
        def index(bi, j):
            rb = j * ng + t
            return (bi, (rb // na) * (RADIX * na) + m * na + rb % na, 0)
        return pl.BlockSpec((1, ta, d), index)

    return pl.pallas_call(
        functools.partial(_fproj_kernel, ng),
        grid=(b, nrb // ng),
        in_specs=[x_spec(t, m) for t in range(ng) for m in range(RADIX)]
        + [pl.BlockSpec((1, N_MOD, d), lambda bi, j: (bi, 0, 0)),
           _const_spec((1, d)), _const_spec(wcs.shape)],
        out_specs=pl.BlockSpec((1, RADIX, 2, ng * ta, d), lambda bi, j: (bi, 0, 0, j, 0)),
        out_shape=jax.ShapeDtypeStruct((b, RADIX, 2, nq, d), BF16),
        compiler_params=pltpu.CompilerParams(
            dimension_semantics=("parallel", "arbitrary"),
            vmem_limit_bytes=_vmem_limit(48 * 1024 * 1024)),
        name="fproj",
    )(*([x] * (RADIX * ng)), mod, g, wcs)


def _fseq_kernel(u_ref, t_ref, x_ref, mod_ref, out_ref):
    nq, d = u_ref.shape[3], u_ref.shape[4]
    u = u_ref[0, 0].reshape(2 * nq, d)
    y = jnp.dot(t_ref[0], u, preferred_element_type=F32)
    out_ref[0] = x_ref[0] + mod_ref[0, 2:3, :] * y


def _seq_dft_tables(n):
    nq = n // RADIX
    nqq = nq // RADIX
    u = np.arange(nq)
    q = (RADIX * (u % nqq) + u // nqq)[None, None, :]
    p = np.arange(nq)[None, :, None]
    r = np.arange(RADIX)[:, None, None]
    ang = 2.0 * np.pi * (((RADIX * p + r) * q) % n) / n
    t = np.concatenate([np.cos(ang), -np.sin(ang)], axis=2) / np.sqrt(n)
    return jnp.asarray(t, F32)


def _fseq(u, t, x, mod, tmr):
    b, s, d = x.shape
    nq = s // RADIX
    nblk = nq // tmr
    return pl.pallas_call(
        _fseq_kernel,
        grid=(b, RADIX, nblk),
        in_specs=[pl.BlockSpec((1, 1, 2, nq, d), lambda bi, r, i: (bi, r, 0, 0, 0)),
                  pl.BlockSpec((1, tmr, 2 * nq), lambda bi, r, i: (r, i, 0)),
                  pl.BlockSpec((1, tmr, d), lambda bi, r, i: (bi, r * nblk + i, 0)),
                  pl.BlockSpec((1, N_MOD, d), lambda bi, r, i: (bi, 0, 0))],
        out_specs=pl.BlockSpec((1, tmr, d), lambda bi, r, i: (bi, r * nblk + i, 0)),
        out_shape=jax.ShapeDtypeStruct((b, s, d), F32),
        compiler_params=pltpu.CompilerParams(
            dimension_semantics=("parallel", "arbitrary", "arbitrary"),
            vmem_limit_bytes=_vmem_limit(40 * 1024 * 1024)),
        name="fseq",
    )(u, t, x, mod)


def _rope_lane_tables(n_ctx, n_tok):
    rows_count = n_tok // GRID_W
    row = np.repeat(np.arange(rows_count), GRID_W).astype(np.float32)
    col = np.tile(np.arange(GRID_W), rows_count).astype(np.float32)
    inv = np.float32(ROPE_THETA) ** (-np.arange(ROPE_PAIRS, dtype=np.float32) / np.float32(ROPE_PAIRS))
    lane = np.arange(LANES)
    dd = lane % HEAD_DIM
    axis = dd // (2 * ROPE_PAIRS)
    half = (dd % (2 * ROPE_PAIRS)) // ROPE_PAIRS
    pair = dd % ROPE_PAIRS
    pos = np.where((axis == 0)[None, :], row[:, None], col[:, None])
    ang = (pos * inv.astype(np.float32)[pair][None, :]).astype(np.float32).astype(np.float64)
    cosv = np.cos(ang)
    sinv = np.sin(ang)
    first = (half == 0)[None, :]
    s1 = np.where(first, -sinv, 0.0)
    s2 = np.where(first, 0.0, sinv)
    pad1 = np.ones((n_ctx, LANES))
    pad0 = np.zeros((n_ctx, LANES))
    return tuple(jnp.asarray(np.concatenate([pad, tab], 0), F32)
                 for pad, tab in ((pad1, cosv), (pad0, s1), (pad0, s2)))


def _permute_w_in(w_in):
    d = w_in.shape[0]
    off_ak = A_W
    off_av = 2 * A_W
    off_bq = 3 * A_W
    off_bk = off_bq + BQ_W
    off_bv = off_bk + BKV_W
    grp = GQA_HEADS // GQA_KV_HEADS
    bq = w_in[:, off_bq:off_bk].reshape(d, GQA_KV_HEADS, grp, HEAD_DIM).swapaxes(1, 2).reshape(d, BQ_W)
    return jnp.concatenate([w_in[:, 0:A_W], bq, w_in[:, off_ak:off_av], w_in[:, off_bk:off_bv],
                            w_in[:, off_av:off_bq], w_in[:, off_bv:off_bv + BKV_W]], axis=1)


def _permute_w_out(w_out):
    d = w_out.shape[1]
    grp = GQA_HEADS // GQA_KV_HEADS
    wb = w_out[A_W:].reshape(GQA_KV_HEADS, grp, HEAD_DIM, d).swapaxes(0, 1).reshape(BQ_W, d)
    return jnp.concatenate([w_out[:A_W], wb], axis=0)


def kernel(x, c, ctx, c_ctx, l0_ada_w, l0_ada_b, l0_norm_mix, l0_w_in, l0_lambda_q1, l0_lambda_k1, l0_lambda_q2, l0_lambda_k2, l0_subln, l0_q_norm, l0_k_norm, l0_w_out, l0_norm_ffn, l0_w_gate_up, l0_w_down, l1_ada_w, l1_ada_b, l1_norm_mix, l1_w_out, l1_norm_ffn, l1_w_gate_up, l1_w_down, final_norm):
    b, s, d = x.shape
    n_ctx = ctx.shape[1]
    assert d == D_MODEL and s % GRID_W == 0

    tt = min(256, n_ctx, s)
    tq = min(128, tt)
    nsub = 2 if n_ctx % (2 * tq) == 0 and s % (2 * tq) == 0 else 1
    tk = min(256, n_ctx)
    tm = min(512, s)
    tmr = min(512, s // RADIX)
    ta = min(128, s // RADIX ** 2)
    assert n_ctx % tt == 0 and s % tt == 0 and tt % tq == 0 and (n_ctx + s) % tk == 0
    assert s % tm == 0 and (s // RADIX) % tmr == 0 and (s // RADIX ** 2) % ta == 0

    n_rows = -(-(b + 1) // SUBLANES) * SUBLANES
    cond = jnp.concatenate([c, c_ctx[None, :], jnp.zeros((n_rows - b - 1, d), F32)], axis=0)
    m0, m1 = _ada(cond, l0_ada_w, l0_ada_b, l1_ada_w, l1_ada_b)
    mod0 = m0[:b].reshape(b, N_MOD, d)
    mod0c = m0[b:b + 1].reshape(1, N_MOD, d)
    mod1 = m1[:b].reshape(b, N_MOD, d)

    w_perm = _permute_w_in(l0_w_in)
    cos_t, s1_t, s2_t = _rope_lane_tables(n_ctx, s)
    qg = jnp.tile(l0_q_norm, LANES // HEAD_DIM).reshape(1, LANES)
    kg = jnp.tile(l0_k_norm, LANES // HEAD_DIM).reshape(1, LANES)
    li = np.arange(LANES)
    ones_bd = jnp.asarray((li[:, None] // HEAD_DIM) == (li[None, :] // HEAD_DIM), BF16)
    qt, k, vt = _proj(x, ctx, mod0, mod0c, l0_norm_mix.reshape(1, d), w_perm.astype(BF16),
                      cos_t, s1_t, s2_t, qg, kg, ones_bd, tt, tq)
    lam_vecs = jnp.stack([l0_lambda_q1, l0_lambda_k1, l0_lambda_q2, l0_lambda_k2]).astype(F32)
    o = _attn(qt, k, vt, lam_vecs, l0_subln.reshape(1, LANES), s, tq, tk, nsub)
    xr = _mixffn(o, x, mod0, _permute_w_out(l0_w_out).astype(BF16), l0_norm_ffn.reshape(1, d),
                 l0_w_gate_up.astype(BF16), l0_w_down.astype(BF16), tm)
    xr = xr.reshape(b, s, d)

    wcs = _fold_channel_dft(l1_w_out)
    u = _fproj(xr, mod1, l1_norm_mix.reshape(1, d), wcs, ta)
    xr = _fseq(u, _seq_dft_tables(s).astype(BF16), xr, mod1, tmr)
    return _ffn_final(xr.reshape(b, RADIX, s // RADIX, d), mod1, l1_norm_ffn.reshape(1, d),
                      l1_w_gate_up.astype(BF16), l1_w_down.astype(BF16), final_norm.reshape(1, d), tm)
```

```python
import functools
import math

import numpy as np
import jax
import jax.numpy as jnp
from jax import lax
from jax.experimental import pallas as pl
from jax.experimental.pallas import tpu as pltpu

F32 = jnp.float32
BF16 = jnp.bfloat16

LANES = 128
SUBLANES = 8
V7X_VMEM_BYTES = 64 * 1024 * 1024

D_MODEL = 1024
GRID_W = 64
DIFF_HEADS = 4
DIFF_HEAD_DIM = 64
GQA_HEADS = 8
GQA_KV_HEADS = 2
GQA_HEAD_DIM = 64
ROPE_THETA = 10000.0
ROPE_PAIRS = GQA_HEAD_DIM // 4
FOURIER_GROUPS = 8
FOURIER_GROUP_DIM = D_MODEL // FOURIER_GROUPS
EPS = 1e-6
N_MOD = 6
HEAD_DIM = 64
A_W = DIFF_HEADS * 2 * DIFF_HEAD_DIM
BQ_W = GQA_HEADS * GQA_HEAD_DIM
BKV_W = GQA_KV_HEADS * GQA_HEAD_DIM
Q_W = A_W + BQ_W
K_W = A_W + BKV_W
N_QG = Q_W // LANES
N_KG = K_W // LANES
LAM_INIT_L0 = 0.8 - 0.6 * math.exp(-0.3 * 0)
Q_SCALE = HEAD_DIM ** -0.5 * math.log2(math.e)
RADIX = 4
ACC_ROWS = 32
PACE_LAG = 1
FFN_SPLIT = 2
VALUE_CHUNKS = 1
VT_ROWS = LANES + 16


def _vmem_limit(nbytes):
    return int(min(nbytes, V7X_VMEM_BYTES - 4 * 1024 * 1024))


def _rms(x, g):
    ms = jnp.mean(x * x, axis=-1, keepdims=True)
    return x * lax.rsqrt(ms + EPS) * g


def _const_spec(shape):
    nd = len(shape)
    return pl.BlockSpec(shape, lambda *_: (0,) * nd, pipeline_mode=pl.Buffered(1))


def _fold_kernel(cd_ref, sd_ref, w_ref, o_ref):
    w = w_ref[...]
    d = w.shape[1]
    w_hi = w.astype(BF16)
    w_lo = (w - w_hi.astype(F32)).astype(BF16)

    def dot3(m):
        m_hi = m.astype(BF16)
        m_lo = (m - m_hi.astype(F32)).astype(BF16)
        return (jnp.dot(m_hi, w_hi, preferred_element_type=F32) + jnp.dot(m_hi, w_lo, preferred_element_type=F32)
                + jnp.dot(m_lo, w_hi, preferred_element_type=F32))

    o_ref[:, 0:d] = dot3(cd_ref[...]).astype(BF16)
    o_ref[:, d:2 * d] = dot3(sd_ref[...]).astype(BF16)


def _fold_channel_dft(w_out):
    d = w_out.shape[0]
    gd = FOURIER_GROUP_DIM
    idx = np.arange(gd)
    ang = 2.0 * np.pi * ((idx[:, None] * idx[None, :]) % gd) / gd
    cd = jnp.asarray(np.cos(ang) / np.sqrt(gd), F32)
    sd = jnp.asarray(np.sin(ang) / np.sqrt(gd), F32)
    return pl.pallas_call(
        _fold_kernel,
        grid=(d // gd,),
        in_specs=[pl.BlockSpec((gd, gd), lambda g: (0, 0)),
                  pl.BlockSpec((gd, gd), lambda g: (0, 0)),
                  pl.BlockSpec((gd, d), lambda g: (g, 0))],
        out_specs=pl.BlockSpec((gd, 2 * d), lambda g: (g, 0)),
        out_shape=jax.ShapeDtypeStruct((d, 2 * d), BF16),
        name="fold",
    )(cd, sd, w_out)


def _ada_kernel(c_ref, w0_ref, b0_ref, w1_ref, b1_ref, o0_ref, o1_ref):
    cv = c_ref[...]
    a = (cv * jax.nn.sigmoid(cv)).astype(BF16)
    o0_ref[...] = jnp.dot(a, w0_ref[...].astype(BF16), preferred_element_type=F32) + b0_ref[...]
    o1_ref[...] = jnp.dot(a, w1_ref[...].astype(BF16), preferred_element_type=F32) + b1_ref[...]


def _ada(cond, w0, b0, w1, b1):
    r, d = cond.shape
    n = w0.shape[1]
    bn = 512
    return pl.pallas_call(
        _ada_kernel,
        grid=(n // bn,),
        in_specs=[pl.BlockSpec((r, d), lambda j: (0, 0)),
                  pl.BlockSpec((d, bn), lambda j: (0, j)),
                  pl.BlockSpec((1, bn), lambda j: (0, j)),
                  pl.BlockSpec((d, bn), lambda j: (0, j)),
                  pl.BlockSpec((1, bn), lambda j: (0, j))],
        out_specs=[pl.BlockSpec((r, bn), lambda j: (0, j)),
                   pl.BlockSpec((r, bn), lambda j: (0, j))],
        out_shape=[jax.ShapeDtypeStruct((r, n), F32), jax.ShapeDtypeStruct((r, n), F32)],
        name="ada",
    )(cond, w0, b0.reshape(1, n), w1, b1.reshape(1, n))


def _proj_kernel(nct, tq, ng, *refs):
    x_refs, ctx_refs = refs[:ng], refs[ng:2 * ng]
    (mod_ref, modc_ref, g_ref, w_ref, cos_ref, s1_ref, s2_ref,
     qg_ref, kg_ref, ones_ref, qt_ref, k_ref, vt_ref) = refs[2 * ng:]
    j = pl.program_id(1)
    gain_x = g_ref[...] * (1.0 + mod_ref[0, 1:2, :])
    gain_c = g_ref[...] * (1.0 + modc_ref[0, 1:2, :])
    hs = []
    for t in range(ng):
        if t >= nct:
            xin, gain, shift = x_refs[t][0], gain_x, mod_ref[0, 0:1, :]
        else:
            is_ctx = j * ng + t < nct
            xin = jnp.where(is_ctx, ctx_refs[t][0], x_refs[t][0])
            gain = jnp.where(is_ctx, gain_c, gain_x)
            shift = jnp.where(is_ctx, modc_ref[0, 0:1, :], mod_ref[0, 0:1, :])
        ms = jnp.mean(xin * xin, axis=-1, keepdims=True)
        hs.append((xin * lax.rsqrt(ms + EPS) * gain + shift).astype(BF16))
    h = jnp.concatenate(hs, axis=0)

    tt = h.shape[0]
    cosv = cos_ref[...]
    s1 = s1_ref[...]
    s2 = s2_ref[...]
    row = lax.broadcasted_iota(jnp.int32, (LANES, tt), 0)
    lo_rows = row < HEAD_DIM

    def rope(xs):
        return xs * cosv + pltpu.roll(xs, LANES - ROPE_PAIRS, 1) * s1 + pltpu.roll(xs, ROPE_PAIRS, 1) * s2

    def headnorm(xs, gain):
        sq = xs * xs
        hi = sq.astype(BF16)
        lo = (sq - hi.astype(F32)).astype(BF16)
        ss = (jnp.dot(hi, ones_ref[...], preferred_element_type=F32)
              + jnp.dot(lo, ones_ref[...], preferred_element_type=F32))
        return xs * lax.rsqrt(ss * (1.0 / HEAD_DIM) + EPS) * gain

    def finish_slab(idx, xs):
        if idx < N_QG:
            s = idx
            if s >= A_W // LANES:
                xs = headnorm(xs, qg_ref[...])
            qst = (rope(xs) * Q_SCALE).T
            q_lo = jnp.where(lo_rows, qst, 0.0)
            q_hi = qst - q_lo
            for a in range(tt // tq):
                qt_ref[0, s, :, 2 * a * tq:(2 * a + 1) * tq] = q_lo[:, a * tq:(a + 1) * tq].astype(BF16)
                qt_ref[0, s, :, (2 * a + 1) * tq:(2 * a + 2) * tq] = q_hi[:, a * tq:(a + 1) * tq].astype(BF16)
        elif idx < N_QG + N_KG:
            s = idx - N_QG
            if s >= A_W // LANES:
                xs = headnorm(xs, kg_ref[...])
            k_ref[0, s] = rope(xs).astype(BF16)
        else:
            s = idx - N_QG - N_KG
            vt_ref[0, s, 0:LANES, :] = xs.T.astype(BF16)
            vt_ref[0, s, LANES:VT_ROWS, :] = jnp.ones((VT_ROWS - LANES, tt), BF16)

    gw = 2 * LANES
    n_groups = w_ref.shape[1] // gw
    pending = None
    for gi in range(n_groups + 1):
        z = None
        if gi < n_groups:
            z = jnp.dot(h, w_ref[:, gi * gw:(gi + 1) * gw], preferred_element_type=F32)
        if pending is not None:
            pg, pz = pending
            for half in range(gw // LANES):
                finish_slab(pg * (gw // LANES) + half, pz[:, half * LANES:(half + 1) * LANES])
        pending = (gi, z)


def _proj(x, ctx, mod, modc, g, w_perm, cos_t, s1_t, s2_t, qg, kg, ones_bd, tt, tq):
    b, s, d = x.shape
    nctx = ctx.shape[1]
    nct = nctx // tt
    nk = nctx + s
    nj = nk // tt
    ng = max(gsz for gsz in (1, 2, 3) if nj % gsz == 0)
    rows = ng * tt
    n_in = w_perm.shape[1]
    kern = functools.partial(_proj_kernel, nct, tq, ng)

    def x_spec(t):
        return pl.BlockSpec((1, tt, d), lambda bi, j: (bi, jnp.maximum(j * ng + t - nct, 0), 0))

    def ctx_spec(t):
        return pl.BlockSpec((1, tt, d), lambda bi, j: (bi, jnp.minimum(j * ng + t, nct - 1), 0))

    tab_spec = pl.BlockSpec((rows, LANES), lambda bi, j: (j, 0))
    return pl.pallas_call(
        kern,
        grid=(b, nj // ng),
        in_specs=[x_spec(t) for t in range(ng)] + [ctx_spec(t) for t in range(ng)]
        + [pl.BlockSpec((1, N_MOD, d), lambda bi, j: (bi, 0, 0)),
           _const_spec((1, N_MOD, d)),
           _const_spec((1, d)),
           _const_spec((d, n_in)),
           tab_spec, tab_spec, tab_spec,
           _const_spec((1, LANES)),
           _const_spec((1, LANES)),
           _const_spec((LANES, LANES))],
        out_specs=[pl.BlockSpec((1, N_QG, LANES, 2 * rows), lambda bi, j: (bi, 0, 0, j)),
                   pl.BlockSpec((1, N_KG, rows, LANES), lambda bi, j: (bi, 0, j, 0)),
                   pl.BlockSpec((1, N_KG, VT_ROWS, rows), lambda bi, j: (bi, 0, 0, j))],
        out_shape=[jax.ShapeDtypeStruct((b, N_QG, LANES, 2 * nk), BF16),
                   jax.ShapeDtypeStruct((b, N_KG, nk, LANES), BF16),
                   jax.ShapeDtypeStruct((b, N_KG, VT_ROWS, nk), BF16)],
        compiler_params=pltpu.CompilerParams(
            dimension_semantics=("parallel", "arbitrary"),
            vmem_limit_bytes=_vmem_limit(56 * 1024 * 1024)),
        name="proj",
    )(*([x] * ng), *([ctx] * ng), mod, modc, g, w_perm, cos_t, s1_t, s2_t, qg, kg, ones_bd)


def _attn_kernel(tq, tk, nsub, ncast, qt_ref, k_ref, vt_ref, lam_ref, subln_ref, *rest):
    w_refs, o_ref, wb_refs, bufs = rest[:ncast], rest[ncast], rest[ncast + 1:2 * ncast + 1], rest[2 * ncast + 1:]
    for w_ref, wb_ref in zip(w_refs, wb_refs):
        wb_ref[...] = w_ref[...].astype(BF16)
    nk = k_ref.shape[2]
    nc = nk // tk
    tq2 = 2 * tq
    row = lax.broadcasted_iota(jnp.int32, (LANES, tq), 0)
    lo_rows = row < HEAD_DIM
    lv = lam_ref[...]
    lam = (jnp.exp(jnp.sum(lv[0:1] * lv[1:2], axis=1, keepdims=True))
           - jnp.exp(jnp.sum(lv[2:3] * lv[3:4], axis=1, keepdims=True)) + LAM_INIT_L0)

    n_a = A_W // LANES

    s_bufs, p_bufs = bufs[:3], bufs[3:]

    def item(w):
        return divmod(w, N_QG)

    def score_chunk(w, c, m8):
        qi, u = item(w)
        kv = min(u, n_a)
        s_c = jnp.dot(k_ref[0, kv, c * tk:(c + 1) * tk, :], qt_ref[0, u, :, qi * tq2:(qi + 1) * tq2],
                      preferred_element_type=F32)
        s_bufs[w % 3][c * tk:(c + 1) * tk, :] = s_c
        mc = jnp.max(s_c.reshape(tk // ACC_ROWS, ACC_ROWS, tq2), axis=0)
        return mc if m8 is None else jnp.maximum(m8, mc)

    def exp_chunk(w, c, m):
        p_bufs[w % 2][c * tk:(c + 1) * tk, :] = jnp.exp2((s_bufs[w % 3][c * tk:(c + 1) * tk, :] - m).astype(BF16))

    def value_chunk(w, c, acc):
        if (c + 1) % VALUE_CHUNKS and c + 1 < nc:
            return acc
        c0 = c - c % VALUE_CHUNKS
        kv = min(item(w)[1], n_a)
        pv = jnp.dot(vt_ref[0, kv, :, c0 * tk:(c + 1) * tk], p_bufs[w % 2][c0 * tk:(c + 1) * tk, :],
                     preferred_element_type=F32)
        return pv if acc is None else acc + pv

    def finish(w, acc):
        qi, u = item(w)
        ot = acc[0:LANES] * (1.0 / acc[LANES:LANES + 1])
        o1, o2 = ot[:, :tq], ot[:, tq:]
        if u < n_a:
            ot = o1 - lam * o2
            ot = ot * lax.rsqrt(jnp.mean(ot * ot, axis=0, keepdims=True) + EPS)
            o_ref[0, u, qi * tq:(qi + 1) * tq, :] = (ot.T * (subln_ref[...] * (1.0 - LAM_INIT_L0))).astype(BF16)
        else:
            o_ref[0, u, qi * tq:(qi + 1) * tq, :] = jnp.where(lo_rows, o1, o2).T.astype(BF16)

    n_items = nsub * N_QG
    m8s = {}
    ms = {}
    accs = {}
    for t in range(n_items + 2):
        if 0 <= t - 1 < n_items:
            ms[t - 1] = jnp.max(m8s.pop(t - 1), axis=0, keepdims=True)
        hist = []
        for c in range(nc):
            hist.append(m8s.get(t) if t < n_items else accs.get(t - 2))
            pace = hist[c - PACE_LAG + 1] if c >= PACE_LAG else None
            pace = None if pace is None else pace[0:1]
            if 0 <= t - 2 < n_items:
                accs[t - 2] = value_chunk(t - 2, c, accs.get(t - 2))
            if t < n_items:
                m8s[t] = score_chunk(t, c, m8s.get(t))
            if 0 <= t - 1 < n_items:
                m = ms[t - 1]
                if pace is not None:
                    m = m + jnp.minimum(jnp.abs(pace), 0.0)
                exp_chunk(t - 1, c, m)
        if 0 <= t - 2 < n_items:
            finish(t - 2, accs.pop(t - 2))


def _cast_rows(n_rows, n_steps):
    best = None
    for r in range(16, n_rows + 1, 16):
        if n_rows % r == 0 and n_steps % (n_rows // r) == 0:
            best = r if best is None else min(best, r)
    return best


def _attn(qt, k, vt, lam_vecs, subln, s, tq, tk, nsub, weights):
    b = qt.shape[0]
    nk = k.shape[2]
    tqs = tq * nsub
    ni = s // tqs
    q0 = (nk - s) // tqs
    plans = [_cast_rows(w.shape[0], b * ni) for w in weights]
    riders = [(w, r) for w, r in zip(weights, plans) if r is not None]
    kern = functools.partial(_attn_kernel, tq, tk, nsub, len(riders))

    def w_spec(w, r):
        rep = (b * ni) // (w.shape[0] // r)
        return pl.BlockSpec((r, w.shape[1]), lambda bi, i: ((bi * ni + i) // rep, 0))

    outs = pl.pallas_call(
        kern,
        grid=(b, ni),
        in_specs=[pl.BlockSpec((1, N_QG, LANES, 2 * tqs), lambda bi, i: (bi, 0, 0, i + q0)),
                  pl.BlockSpec((1, N_KG, nk, LANES), lambda bi, i: (bi, 0, 0, 0)),
                  pl.BlockSpec((1, N_KG, VT_ROWS, nk), lambda bi, i: (bi, 0, 0, 0)),
                  _const_spec((4, HEAD_DIM)),
                  _const_spec((1, LANES))] + [w_spec(w, r) for w, r in riders],
        out_specs=[pl.BlockSpec((1, N_QG, tqs, LANES), lambda bi, i: (bi, 0, i, 0))]
        + [w_spec(w, r) for w, r in riders],
        out_shape=[jax.ShapeDtypeStruct((b, N_QG, s, LANES), BF16)]
        + [jax.ShapeDtypeStruct(w.shape, BF16) for w, _ in riders],
        scratch_shapes=[pltpu.VMEM((nk, 2 * tq), F32)] * 3 + [pltpu.VMEM((nk, 2 * tq), BF16)] * 2,
        compiler_params=pltpu.CompilerParams(
            dimension_semantics=("arbitrary", "arbitrary"),
            vmem_limit_bytes=_vmem_limit(48 * 1024 * 1024)),
        name="attn",
    )(qt, k, vt, lam_vecs, subln, *[w for w, _ in riders])
    cast = iter(outs[1:])
    return outs[0], [next(cast) if r is not None else w.astype(BF16) for w, r in zip(weights, plans)]


def _swiglu(x_all, mod_ref, gffn_ref, wgu_ref, wd_ref):
    shift = mod_ref[0, 3:4, :]
    gain = gffn_ref[...] * (1.0 + mod_ref[0, 4:5, :])
    gate = mod_ref[0, 5:6, :]
    n = x_all.shape[0]
    nh = n // FFN_SPLIT
    outs = []
    for r0 in range(0, n, nh):
        x1 = x_all[r0:r0 + nh]
        h = _rms(x1, gain) + shift
        gu = jnp.dot(h.astype(BF16), wgu_ref[...], preferred_element_type=F32)
        dff = gu.shape[1] // 2
        g = gu[:, :dff]
        u = gu[:, dff:]
        a = (g * jax.nn.sigmoid(g)) * u
        dn = jnp.dot(a.astype(BF16), wd_ref[...], preferred_element_type=F32)
        outs.append(x1 + gate * dn)
    return outs[0] if len(outs) == 1 else jnp.concatenate(outs, axis=0)


def _mixffn_kernel(o_ref, x_ref, mod_ref, wout_ref, gffn_ref, wgu_ref, wd_ref, out_ref, res_scr):
    o = jnp.concatenate([o_ref[0, u] for u in range(N_QG)], axis=1)
    y = jnp.dot(o, wout_ref[...], preferred_element_type=F32)
    x1 = x_ref[0] + mod_ref[0, 2:3, :] * y
    res = _swiglu(x1, mod_ref, gffn_ref, wgu_ref, wd_ref)
    n = res_scr.shape[1] // RADIX
    for sl in range(res_scr.shape[0]):
        res_scr[sl] = res[:, sl * LANES:(sl + 1) * LANES]
        for r in range(RADIX):
            out_ref[0, r, :, sl * LANES:(sl + 1) * LANES] = res_scr[sl, pl.ds(r, n, stride=RADIX), :]


def _ffn_final_kernel(x_ref, mod_ref, gffn_ref, wgu_ref, wd_ref, gfin_ref, out_ref, x_scr):
    n = x_scr.shape[1] // RADIX
    for sl in range(x_scr.shape[0]):
        for r in range(RADIX):
            x_scr[sl, pl.ds(r, n, stride=RADIX), :] = x_ref[0, r, :, sl * LANES:(sl + 1) * LANES]
    x1 = jnp.concatenate([x_scr[sl] for sl in range(x_scr.shape[0])], axis=1)
    x2 = _swiglu(x1, mod_ref, gffn_ref, wgu_ref, wd_ref)
    out_ref[0] = _rms(x2, gfin_ref[...])


def _mixffn(o, x, mod, wout, gffn, wgu, wd, tm):
    b, s, d = x.shape
    tok = lambda bi, i: (bi, i, 0)
    return pl.pallas_call(
        _mixffn_kernel,
        grid=(b, s // tm),
        in_specs=[pl.BlockSpec((1, N_QG, tm, LANES), lambda bi, i: (bi, 0, i, 0)),
                  pl.BlockSpec((1, tm, d), tok),
                  pl.BlockSpec((1, N_MOD, d), lambda bi, i: (bi, 0, 0)),
                  _const_spec(wout.shape), _const_spec((1, d)), _const_spec(wgu.shape), _const_spec(wd.shape)],
        out_specs=pl.BlockSpec((1, RADIX, tm // RADIX, d), lambda bi, i: (bi, 0, i, 0)),
        out_shape=jax.ShapeDtypeStruct((b, RADIX, s // RADIX, d), F32),
        scratch_shapes=[pltpu.VMEM((d // LANES, tm, LANES), F32)],
        compiler_params=pltpu.CompilerParams(
            dimension_semantics=("parallel", "arbitrary"),
            vmem_limit_bytes=_vmem_limit(56 * 1024 * 1024)),
        name="mixffn",
    )(o, x, mod, wout, gffn, wgu, wd)


def _ffn_final(xr, mod, gffn, wgu, wd, gfin, tm):
    b, _, nq, d = xr.shape
    s = nq * RADIX
    return pl.pallas_call(
        _ffn_final_kernel,
        grid=(b, s // tm),
        in_specs=[pl.BlockSpec((1, RADIX, tm // RADIX, d), lambda bi, i: (bi, 0, i, 0)),
                  pl.BlockSpec((1, N_MOD, d), lambda bi, i: (bi, 0, 0)),
                  _const_spec((1, d)), _const_spec(wgu.shape), _const_spec(wd.shape), _const_spec((1, d))],
        out_specs=pl.BlockSpec((1, tm, d), lambda bi, i: (bi, i, 0)),
        out_shape=jax.ShapeDtypeStruct((b, s, d), F32),
        scratch_shapes=[pltpu.VMEM((d // LANES, tm, LANES), F32)],
        compiler_params=pltpu.CompilerParams(
            dimension_semantics=("parallel", "arbitrary"),
            vmem_limit_bytes=_vmem_limit(56 * 1024 * 1024)),
        name="ffn",
    )(xr, mod, gffn, wgu, wd, gfin)


_QUARTER = ((1, 0), (0, 1), (-1, 0), (0, -1))


def _fproj_kernel(ng, *refs):
    x_refs = refs[:RADIX * ng]
    mod_ref, g_ref, w_ref, u_ref = refs[RADIX * ng:]
    d = x_refs[0].shape[2]
    ta = x_refs[0].shape[1]
    shift = mod_ref[0, 0:1, :]
    gain = g_ref[...] * (1.0 + mod_ref[0, 1:2, :])
    for t in range(ng):
        h = jnp.concatenate([(_rms(xr[0], gain) + shift).astype(BF16)
                             for xr in x_refs[RADIX * t:RADIX * (t + 1)]], axis=0)
        z = jnp.dot(h, w_ref[...], preferred_element_type=F32)
        zc = [z[m * ta:(m + 1) * ta, 0:d] for m in range(RADIX)]
        zs = [z[m * ta:(m + 1) * ta, d:2 * d] for m in range(RADIX)]
        for r in range(RADIX):
            ua = None
            ub = None
            for m in range(RADIX):
                cp, sp = _QUARTER[(r * m) % 4]
                ta_, sa = (zc[m], cp) if cp else (zs[m], -sp)
                tb_, sb = (zs[m], cp) if cp else (zc[m], sp)
                ua = sa * ta_ if ua is None else (ua + ta_ if sa > 0 else ua - ta_)
                ub = sb * tb_ if ub is None else (ub + tb_ if sb > 0 else ub - tb_)
            u_ref[0, r, 0, t * ta:(t + 1) * ta, :] = ua.astype(BF16)
            u_ref[0, r, 1, t * ta:(t + 1) * ta, :] = ub.astype(BF16)


def _fproj(x, mod, g, wcs, ta):
    b, s, d = x.shape
    nq = s // RADIX
    nqq = nq // RADIX
    na = nqq // ta
    nrb = RADIX * na
    ng = 2

    def x_spec(t, m):
        def index(bi, j):
            rb = j * ng + t
            return (bi, (rb // na) * (RADIX * na) + m * na + rb % na, 0)
        return pl.BlockSpec((1, ta, d), index)

    return pl.pallas_call(
        functools.partial(_fproj_kernel, ng),
        grid=(b, nrb // ng),
        in_specs=[x_spec(t, m) for t in range(ng) for m in range(RADIX)]
        + [pl.BlockSpec((1, N_MOD, d), lambda bi, j: (bi, 0, 0)),
           _const_spec((1, d)), _const_spec(wcs.shape)],
        out_specs=pl.BlockSpec((1, RADIX, 2, ng * ta, d), lambda bi, j: (bi, 0, 0, j, 0)),
        out_shape=jax.ShapeDtypeStruct((b, RADIX, 2, nq, d), BF16),
        compiler_params=pltpu.CompilerParams(
            dimension_semantics=("parallel", "arbitrary"),
            vmem_limit_bytes=_vmem_limit(48 * 1024 * 1024)),
        name="fproj",
    )(*([x] * (RADIX * ng)), mod, g, wcs)


def _fseq_kernel(u_ref, t_ref, x_ref, mod_ref, out_ref):
    nq, d = u_ref.shape[3], u_ref.shape[4]
    u = u_ref[0, 0].reshape(2 * nq, d)
    y = jnp.dot(t_ref[0], u, preferred_element_type=F32)
    out_ref[0] = x_ref[0] + mod_ref[0, 2:3, :] * y


def _seq_dft_tables(n):
    nq = n // RADIX
    nqq = nq // RADIX
    u = np.arange(nq)
    q = (RADIX * (u % nqq) + u // nqq)[None, None, :]
    p = np.arange(nq)[None, :, None]
    r = np.arange(RADIX)[:, None, None]
    ang = 2.0 * np.pi * (((RADIX * p + r) * q) % n) / n
    t = np.concatenate([np.cos(ang), -np.sin(ang)], axis=2) / np.sqrt(n)
    return jnp.asarray(t, F32)


def _fseq(u, t, x, mod, tmr):
    b, s, d = x.shape
    nq = s // RADIX
    nblk = nq // tmr
    return pl.pallas_call(
        _fseq_kernel,
        grid=(b, RADIX, nblk),
        in_specs=[pl.BlockSpec((1, 1, 2, nq, d), lambda bi, r, i: (bi, r, 0, 0, 0)),
                  pl.BlockSpec((1, tmr, 2 * nq), lambda bi, r, i: (r, i, 0)),
                  pl.BlockSpec((1, tmr, d), lambda bi, r, i: (bi, r * nblk + i, 0)),
                  pl.BlockSpec((1, N_MOD, d), lambda bi, r, i: (bi, 0, 0))],
        out_specs=pl.BlockSpec((1, tmr, d), lambda bi, r, i: (bi, r * nblk + i, 0)),
        out_shape=jax.ShapeDtypeStruct((b, s, d), F32),
        compiler_params=pltpu.CompilerParams(
            dimension_semantics=("parallel", "arbitrary", "arbitrary"),
            vmem_limit_bytes=_vmem_limit(40 * 1024 * 1024)),
        name="fseq",
    )(u, t, x, mod)


def _rope_lane_tables(n_ctx, n_tok):
    rows_count = n_tok // GRID_W
    row = np.repeat(np.arange(rows_count), GRID_W).astype(np.float32)
    col = np.tile(np.arange(GRID_W), rows_count).astype(np.float32)
    inv = np.float32(ROPE_THETA) ** (-np.arange(ROPE_PAIRS, dtype=np.float32) / np.float32(ROPE_PAIRS))
    lane = np.arange(LANES)
    dd = lane % HEAD_DIM
    axis = dd // (2 * ROPE_PAIRS)
    half = (dd % (2 * ROPE_PAIRS)) // ROPE_PAIRS
    pair = dd % ROPE_PAIRS
    pos = np.where((axis == 0)[None, :], row[:, None], col[:, None])
    ang = (pos * inv.astype(np.float32)[pair][None, :]).astype(np.float32).astype(np.float64)
    cosv = np.cos(ang)
    sinv = np.sin(ang)
    first = (half == 0)[None, :]
    s1 = np.where(first, -sinv, 0.0)
    s2 = np.where(first, 0.0, sinv)
    pad1 = np.ones((n_ctx, LANES))
    pad0 = np.zeros((n_ctx, LANES))
    return tuple(jnp.asarray(np.concatenate([pad, tab], 0), F32)
                 for pad, tab in ((pad1, cosv), (pad0, s1), (pad0, s2)))


def _permute_w_in(w_in):
    d = w_in.shape[0]
    off_ak = A_W
    off_av = 2 * A_W
    off_bq = 3 * A_W
    off_bk = off_bq + BQ_W
    off_bv = off_bk + BKV_W
    grp = GQA_HEADS // GQA_KV_HEADS
    bq = w_in[:, off_bq:off_bk].reshape(d, GQA_KV_HEADS, grp, HEAD_DIM).swapaxes(1, 2).reshape(d, BQ_W)
    return jnp.concatenate([w_in[:, 0:A_W], bq, w_in[:, off_ak:off_av], w_in[:, off_bk:off_bv],
                            w_in[:, off_av:off_bq], w_in[:, off_bv:off_bv + BKV_W]], axis=1)


def _permute_w_out(w_out):
    d = w_out.shape[1]
    grp = GQA_HEADS // GQA_KV_HEADS
    wb = w_out[A_W:].reshape(GQA_KV_HEADS, grp, HEAD_DIM, d).swapaxes(0, 1).reshape(BQ_W, d)
    return jnp.concatenate([w_out[:A_W], wb], axis=0)


def kernel(x, c, ctx, c_ctx, l0_ada_w, l0_ada_b, l0_norm_mix, l0_w_in, l0_lambda_q1, l0_lambda_k1, l0_lambda_q2, l0_lambda_k2, l0_subln, l0_q_norm, l0_k_norm, l0_w_out, l0_norm_ffn, l0_w_gate_up, l0_w_down, l1_ada_w, l1_ada_b, l1_norm_mix, l1_w_out, l1_norm_ffn, l1_w_gate_up, l1_w_down, final_norm):
    b, s, d = x.shape
    n_ctx = ctx.shape[1]
    assert d == D_MODEL and s % GRID_W == 0

    tt = min(256, n_ctx, s)
    tq = min(128, tt)
    nsub = 2 if n_ctx % (2 * tq) == 0 and s % (2 * tq) == 0 else 1
    tk = min(256, n_ctx)
    tm = min(512, s)
    tmr = min(512, s // RADIX)
    ta = min(128, s // RADIX ** 2)
    assert n_ctx % tt == 0 and s % tt == 0 and tt % tq == 0 and (n_ctx + s) % tk == 0
    assert s % tm == 0 and (s // RADIX) % tmr == 0 and (s // RADIX ** 2) % ta == 0

    n_rows = -(-(b + 1) // SUBLANES) * SUBLANES
    cond = jnp.concatenate([c, c_ctx[None, :], jnp.zeros((n_rows - b - 1, d), F32)], axis=0)
    m0, m1 = _ada(cond, l0_ada_w, l0_ada_b, l1_ada_w, l1_ada_b)
    mod0 = m0[:b].reshape(b, N_MOD, d)
    mod0c = m0[b:b + 1].reshape(1, N_MOD, d)
    mod1 = m1[:b].reshape(b, N_MOD, d)

    w_perm = _permute_w_in(l0_w_in)
    cos_t, s1_t, s2_t = _rope_lane_tables(n_ctx, s)
    qg = jnp.tile(l0_q_norm, LANES // HEAD_DIM).reshape(1, LANES)
    kg = jnp.tile(l0_k_norm, LANES // HEAD_DIM).reshape(1, LANES)
    li = np.arange(LANES)
    ones_bd = jnp.asarray((li[:, None] // HEAD_DIM) == (li[None, :] // HEAD_DIM), BF16)
    qt, k, vt = _proj(x, ctx, mod0, mod0c, l0_norm_mix.reshape(1, d), w_perm.astype(BF16),
                      cos_t, s1_t, s2_t, qg, kg, ones_bd, tt, tq)
    lam_vecs = jnp.stack([l0_lambda_q1, l0_lambda_k1, l0_lambda_q2, l0_lambda_k2]).astype(F32)
    o, (w_out0, w_gu0, w_dn0, w_gu1, w_dn1) = _attn(
        qt, k, vt, lam_vecs, l0_subln.reshape(1, LANES), s, tq, tk, nsub,
        [_permute_w_out(l0_w_out), l0_w_gate_up, l0_w_down, l1_w_gate_up, l1_w_down])
    xr = _mixffn(o, x, mod0, w_out0, l0_norm_ffn.reshape(1, d), w_gu0, w_dn0, tm)
    xr = xr.reshape(b, s, d)

    wcs = _fold_channel_dft(l1_w_out)
    u = _fproj(xr, mod1, l1_norm_mix.reshape(1, d), wcs, ta)
    xr = _fseq(u, _seq_dft_tables(s).astype(BF16), xr, mod1, tmr)
    return _ffn_final(xr.reshape(b, RADIX, s // RADIX, d), mod1, l1_norm_ffn.reshape(1, d),
                      w_gu1, w_dn1, final_norm.reshape(1, d), tm)
```

```python
import functools
import math

import numpy as np
import jax
import jax.numpy as jnp
from jax import lax
from jax.experimental import pallas as pl
from jax.experimental.pallas import tpu as pltpu

F32 = jnp.float32
BF16 = jnp.bfloat16

LANES = 128
SUBLANES = 8
V7X_VMEM_BYTES = 64 * 1024 * 1024

D_MODEL = 1024
GRID_W = 64
DIFF_HEADS = 4
DIFF_HEAD_DIM = 64
GQA_HEADS = 8
GQA_KV_HEADS = 2
GQA_HEAD_DIM = 64
ROPE_THETA = 10000.0
ROPE_PAIRS = GQA_HEAD_DIM // 4
FOURIER_GROUPS = 8
FOURIER_GROUP_DIM = D_MODEL // FOURIER_GROUPS
EPS = 1e-6
N_MOD = 6
HEAD_DIM = 64
A_W = DIFF_HEADS * 2 * DIFF_HEAD_DIM
BQ_W = GQA_HEADS * GQA_HEAD_DIM
BKV_W = GQA_KV_HEADS * GQA_HEAD_DIM
Q_W = A_W + BQ_W
K_W = A_W + BKV_W
N_QG = Q_W // LANES
N_KG = K_W // LANES
LAM_INIT_L0 = 0.8 - 0.6 * math.exp(-0.3 * 0)
Q_SCALE = HEAD_DIM ** -0.5 * math.log2(math.e)
RADIX = 4
ACC_ROWS = 32
PACE_LAG = 1
FFN_SPLIT = 2
VALUE_CHUNKS = 1
VT_ROWS = LANES + 16


def _vmem_limit(nbytes):
    return int(min(nbytes, V7X_VMEM_BYTES - 4 * 1024 * 1024))


def _rms(x, g):
    ms = jnp.mean(x * x, axis=-1, keepdims=True)
    return x * lax.rsqrt(ms + EPS) * g


def _const_spec(shape):
    nd = len(shape)
    return pl.BlockSpec(shape, lambda *_: (0,) * nd, pipeline_mode=pl.Buffered(1))


def _fold_kernel(cd_ref, sd_ref, w_ref, o_ref):
    w = w_ref[...]
    d = w.shape[1]
    w_hi = w.astype(BF16)
    w_lo = (w - w_hi.astype(F32)).astype(BF16)

    def dot3(m):
        m_hi = m.astype(BF16)
        m_lo = (m - m_hi.astype(F32)).astype(BF16)
        return (jnp.dot(m_hi, w_hi, preferred_element_type=F32) + jnp.dot(m_hi, w_lo, preferred_element_type=F32)
                + jnp.dot(m_lo, w_hi, preferred_element_type=F32))

    o_ref[:, 0:d] = dot3(cd_ref[...]).astype(BF16)
    o_ref[:, d:2 * d] = dot3(sd_ref[...]).astype(BF16)


def _ada_kernel(c_ref, w0_ref, b0_ref, w1_ref, b1_ref, cd_ref, sd_ref, wo_ref, o0_ref, o1_ref, wcs_ref):
    cv = c_ref[...]
    a = (cv * jax.nn.sigmoid(cv)).astype(BF16)
    o0_ref[...] = jnp.dot(a, w0_ref[...].astype(BF16), preferred_element_type=F32) + b0_ref[...]
    o1_ref[...] = jnp.dot(a, w1_ref[...].astype(BF16), preferred_element_type=F32) + b1_ref[...]
    _fold_kernel(cd_ref, sd_ref, wo_ref, wcs_ref)


def _ada_fold(cond, w0, b0, w1, b1, w_out):
    r, d = cond.shape
    n = w0.shape[1]
    bn = 512
    gd = FOURIER_GROUP_DIM
    ngr = w_out.shape[0] // gd
    assert n // bn >= ngr
    idx = np.arange(gd)
    ang = 2.0 * np.pi * ((idx[:, None] * idx[None, :]) % gd) / gd
    cd = jnp.asarray(np.cos(ang) / np.sqrt(gd), F32)
    sd = jnp.asarray(np.sin(ang) / np.sqrt(gd), F32)
    grp = lambda j: (jnp.minimum(j, ngr - 1), 0)
    return pl.pallas_call(
        _ada_kernel,
        grid=(n // bn,),
        in_specs=[pl.BlockSpec((r, d), lambda j: (0, 0)),
                  pl.BlockSpec((d, bn), lambda j: (0, j)),
                  pl.BlockSpec((1, bn), lambda j: (0, j)),
                  pl.BlockSpec((d, bn), lambda j: (0, j)),
                  pl.BlockSpec((1, bn), lambda j: (0, j)),
                  pl.BlockSpec((gd, gd), lambda j: (0, 0)),
                  pl.BlockSpec((gd, gd), lambda j: (0, 0)),
                  pl.BlockSpec((gd, w_out.shape[1]), grp)],
        out_specs=[pl.BlockSpec((r, bn), lambda j: (0, j)),
                   pl.BlockSpec((r, bn), lambda j: (0, j)),
                   pl.BlockSpec((gd, 2 * w_out.shape[1]), grp)],
        out_shape=[jax.ShapeDtypeStruct((r, n), F32), jax.ShapeDtypeStruct((r, n), F32),
                   jax.ShapeDtypeStruct((w_out.shape[0], 2 * w_out.shape[1]), BF16)],
        compiler_params=pltpu.CompilerParams(dimension_semantics=("arbitrary",)),
        name="ada",
    )(cond, w0, b0.reshape(1, n), w1, b1.reshape(1, n), cd, sd, w_out)


def _proj_kernel(nlat, nct, tq, ng, *refs):
    x_refs, ctx_refs = refs[:ng], refs[ng:2 * ng]
    (mod_ref, modc_ref, g_ref, w_ref, cos_ref, s1_ref, s2_ref,
     qg_ref, kg_ref, ones_ref, qt_ref, k_ref, vt_ref) = refs[2 * ng:]
    j = pl.program_id(1)
    gain_x = g_ref[...] * (1.0 + mod_ref[0, 1:2, :])
    gain_c = g_ref[...] * (1.0 + modc_ref[0, 1:2, :])
    hs = []
    for t in range(ng):
        if ng - 1 - t >= nct:
            xin, gain, shift = x_refs[t][0], gain_x, mod_ref[0, 0:1, :]
        else:
            is_ctx = j * ng + t >= nlat
            xin = jnp.where(is_ctx, ctx_refs[t][0], x_refs[t][0])
            gain = jnp.where(is_ctx, gain_c, gain_x)
            shift = jnp.where(is_ctx, modc_ref[0, 0:1, :], mod_ref[0, 0:1, :])
        ms = jnp.mean(xin * xin, axis=-1, keepdims=True)
        hs.append((xin * lax.rsqrt(ms + EPS) * gain + shift).astype(BF16))
    h = jnp.concatenate(hs, axis=0)

    tt = h.shape[0]
    cosv = cos_ref[...]
    s1 = s1_ref[...]
    s2 = s2_ref[...]
    row = lax.broadcasted_iota(jnp.int32, (LANES, tt), 0)
    lo_rows = row < HEAD_DIM

    def rope(xs):
        return xs * cosv + pltpu.roll(xs, LANES - ROPE_PAIRS, 1) * s1 + pltpu.roll(xs, ROPE_PAIRS, 1) * s2

    def headnorm(xs, gain):
        sq = xs * xs
        hi = sq.astype(BF16)
        lo = (sq - hi.astype(F32)).astype(BF16)
        ss = (jnp.dot(hi, ones_ref[...], preferred_element_type=F32)
              + jnp.dot(lo, ones_ref[...], preferred_element_type=F32))
        return xs * lax.rsqrt(ss * (1.0 / HEAD_DIM) + EPS) * gain

    def finish_slab(idx, xs):
        if idx < N_QG:
            s = idx
            if s >= A_W // LANES:
                xs = headnorm(xs, qg_ref[...])
            qst = (rope(xs) * Q_SCALE).T
            q_lo = jnp.where(lo_rows, qst, 0.0)
            q_hi = qst - q_lo
            for a in range(tt // tq):
                qt_ref[0, s, :, 2 * a * tq:(2 * a + 1) * tq] = q_lo[:, a * tq:(a + 1) * tq].astype(BF16)
                qt_ref[0, s, :, (2 * a + 1) * tq:(2 * a + 2) * tq] = q_hi[:, a * tq:(a + 1) * tq].astype(BF16)
        elif idx < N_QG + N_KG:
            s = idx - N_QG
            if s >= A_W // LANES:
                xs = headnorm(xs, kg_ref[...])
            k_ref[0, s] = rope(xs).astype(BF16)
        else:
            s = idx - N_QG - N_KG
            vt_ref[0, s, 0:LANES, :] = xs.T.astype(BF16)
            vt_ref[0, s, LANES:VT_ROWS, :] = jnp.ones((VT_ROWS - LANES, tt), BF16)

    gw = 2 * LANES
    n_groups = w_ref.shape[1] // gw
    pending = None
    for gi in range(n_groups + 1):
        z = None
        if gi < n_groups:
            z = jnp.dot(h, w_ref[:, gi * gw:(gi + 1) * gw], preferred_element_type=F32)
        if pending is not None:
            pg, pz = pending
            for half in range(gw // LANES):
                finish_slab(pg * (gw // LANES) + half, pz[:, half * LANES:(half + 1) * LANES])
        pending = (gi, z)


def _proj(x, ctx, mod, modc, g, w_perm, cos_t, s1_t, s2_t, qg, kg, ones_bd, tt, tq):
    b, s, d = x.shape
    nctx = ctx.shape[1]
    nct = nctx // tt
    nk = nctx + s
    nj = nk // tt
    ng = max(gsz for gsz in (1, 2, 3) if nj % gsz == 0)
    rows = ng * tt
    n_in = w_perm.shape[1]
    nlat = nj - nct
    kern = functools.partial(_proj_kernel, nlat, nct, tq, ng)

    def x_spec(t):
        return pl.BlockSpec((1, tt, d), lambda bi, j: (bi, jnp.minimum(j * ng + t, nlat - 1), 0))

    def ctx_spec(t):
        return pl.BlockSpec((1, tt, d), lambda bi, j: (bi, jnp.maximum(j * ng + t - nlat, 0), 0))

    tab_spec = pl.BlockSpec((rows, LANES), lambda bi, j: (j, 0))
    return pl.pallas_call(
        kern,
        grid=(b, nj // ng),
        in_specs=[x_spec(t) for t in range(ng)] + [ctx_spec(t) for t in range(ng)]
        + [pl.BlockSpec((1, N_MOD, d), lambda bi, j: (bi, 0, 0)),
           _const_spec((1, N_MOD, d)),
           _const_spec((1, d)),
           _const_spec((d, n_in)),
           tab_spec, tab_spec, tab_spec,
           _const_spec((1, LANES)),
           _const_spec((1, LANES)),
           _const_spec((LANES, LANES))],
        out_specs=[pl.BlockSpec((1, N_QG, LANES, 2 * rows), lambda bi, j: (bi, 0, 0, j)),
                   pl.BlockSpec((1, N_KG, rows, LANES), lambda bi, j: (bi, 0, j, 0)),
                   pl.BlockSpec((1, N_KG, VT_ROWS, rows), lambda bi, j: (bi, 0, 0, j))],
        out_shape=[jax.ShapeDtypeStruct((b, N_QG, LANES, 2 * nk), BF16),
                   jax.ShapeDtypeStruct((b, N_KG, nk, LANES), BF16),
                   jax.ShapeDtypeStruct((b, N_KG, VT_ROWS, nk), BF16)],
        compiler_params=pltpu.CompilerParams(
            dimension_semantics=("parallel", "arbitrary"),
            vmem_limit_bytes=_vmem_limit(56 * 1024 * 1024)),
        name="proj",
    )(*([x] * ng), *([ctx] * ng), mod, modc, g, w_perm, cos_t, s1_t, s2_t, qg, kg, ones_bd)


def _attn_kernel(tq, tk, nsub, ncast, qt_ref, k_ref, vt_ref, lam_ref, subln_ref, *rest):
    w_refs, o_ref, wb_refs, bufs = rest[:ncast], rest[ncast], rest[ncast + 1:2 * ncast + 1], rest[2 * ncast + 1:]
    for w_ref, wb_ref in zip(w_refs, wb_refs):
        wb_ref[...] = w_ref[...].astype(BF16)
    nk = k_ref.shape[2]
    nc = nk // tk
    tq2 = 2 * tq
    row = lax.broadcasted_iota(jnp.int32, (LANES, tq), 0)
    lo_rows = row < HEAD_DIM
    lv = lam_ref[...]
    lam = (jnp.exp(jnp.sum(lv[0:1] * lv[1:2], axis=1, keepdims=True))
           - jnp.exp(jnp.sum(lv[2:3] * lv[3:4], axis=1, keepdims=True)) + LAM_INIT_L0)

    n_a = A_W // LANES

    s_bufs, p_bufs = bufs[:3], bufs[3:]

    def item(w):
        return divmod(w, N_QG)

    def score_chunk(w, c, m8):
        qi, u = item(w)
        kv = min(u, n_a)
        s_c = jnp.dot(k_ref[0, kv, c * tk:(c + 1) * tk, :], qt_ref[0, u, :, qi * tq2:(qi + 1) * tq2],
                      preferred_element_type=F32)
        s_bufs[w % 3][c * tk:(c + 1) * tk, :] = s_c
        mc = jnp.max(s_c.reshape(tk // ACC_ROWS, ACC_ROWS, tq2), axis=0)
        return mc if m8 is None else jnp.maximum(m8, mc)

    def exp_chunk(w, c, m):
        p_bufs[w % 2][c * tk:(c + 1) * tk, :] = jnp.exp2((s_bufs[w % 3][c * tk:(c + 1) * tk, :] - m).astype(BF16))

    def value_chunk(w, c, acc):
        if (c + 1) % VALUE_CHUNKS and c + 1 < nc:
            return acc
        c0 = c - c % VALUE_CHUNKS
        kv = min(item(w)[1], n_a)
        pv = jnp.dot(vt_ref[0, kv, :, c0 * tk:(c + 1) * tk], p_bufs[w % 2][c0 * tk:(c + 1) * tk, :],
                     preferred_element_type=F32)
        return pv if acc is None else acc + pv

    def finish(w, acc):
        qi, u = item(w)
        ot = acc[0:LANES] * (1.0 / acc[LANES:LANES + 1])
        o1, o2 = ot[:, :tq], ot[:, tq:]
        if u < n_a:
            ot = o1 - lam * o2
            ot = ot * lax.rsqrt(jnp.mean(ot * ot, axis=0, keepdims=True) + EPS)
            o_ref[0, u, qi * tq:(qi + 1) * tq, :] = (ot.T * (subln_ref[...] * (1.0 - LAM_INIT_L0))).astype(BF16)
        else:
            o_ref[0, u, qi * tq:(qi + 1) * tq, :] = jnp.where(lo_rows, o1, o2).T.astype(BF16)

    n_items = nsub * N_QG
    m8s = {}
    ms = {}
    accs = {}
    for t in range(n_items + 2):
        if 0 <= t - 1 < n_items:
            ms[t - 1] = jnp.max(m8s.pop(t - 1), axis=0, keepdims=True)
        hist = []
        for c in range(nc):
            hist.append(m8s.get(t) if t < n_items else accs.get(t - 2))
            pace = hist[c - PACE_LAG + 1] if c >= PACE_LAG else None
            pace = None if pace is None else pace[0:1]
            if 0 <= t - 2 < n_items:
                accs[t - 2] = value_chunk(t - 2, c, accs.get(t - 2))
            if t < n_items:
                m8s[t] = score_chunk(t, c, m8s.get(t))
            if 0 <= t - 1 < n_items:
                m = ms[t - 1]
                if pace is not None:
                    m = m + jnp.minimum(jnp.abs(pace), 0.0)
                exp_chunk(t - 1, c, m)
        if 0 <= t - 2 < n_items:
            finish(t - 2, accs.pop(t - 2))


def _cast_rows(n_rows, n_steps):
    best = None
    for r in range(16, n_rows + 1, 16):
        if n_rows % r == 0 and n_steps % (n_rows // r) == 0:
            best = r if best is None else min(best, r)
    return best


def _attn(qt, k, vt, lam_vecs, subln, s, tq, tk, nsub, weights):
    b = qt.shape[0]
    nk = k.shape[2]
    tqs = tq * nsub
    ni = s // tqs
    plans = [_cast_rows(w.shape[0], b * ni) for w in weights]
    riders = [(w, r) for w, r in zip(weights, plans) if r is not None]
    kern = functools.partial(_attn_kernel, tq, tk, nsub, len(riders))

    def w_spec(w, r):
        rep = (b * ni) // (w.shape[0] // r)
        return pl.BlockSpec((r, w.shape[1]), lambda bi, i: ((bi * ni + i) // rep, 0))

    outs = pl.pallas_call(
        kern,
        grid=(b, ni),
        in_specs=[pl.BlockSpec((1, N_QG, LANES, 2 * tqs), lambda bi, i: (bi, 0, 0, i)),
                  pl.BlockSpec((1, N_KG, nk, LANES), lambda bi, i: (bi, 0, 0, 0)),
                  pl.BlockSpec((1, N_KG, VT_ROWS, nk), lambda bi, i: (bi, 0, 0, 0)),
                  _const_spec((4, HEAD_DIM)),
                  _const_spec((1, LANES))] + [w_spec(w, r) for w, r in riders],
        out_specs=[pl.BlockSpec((1, N_QG, tqs, LANES), lambda bi, i: (bi, 0, i, 0))]
        + [w_spec(w, r) for w, r in riders],
        out_shape=[jax.ShapeDtypeStruct((b, N_QG, s, LANES), BF16)]
        + [jax.ShapeDtypeStruct(w.shape, BF16) for w, _ in riders],
        scratch_shapes=[pltpu.VMEM((nk, 2 * tq), F32)] * 3 + [pltpu.VMEM((nk, 2 * tq), BF16)] * 2,
        compiler_params=pltpu.CompilerParams(
            dimension_semantics=("arbitrary", "arbitrary"),
            vmem_limit_bytes=_vmem_limit(48 * 1024 * 1024)),
        name="attn",
    )(qt, k, vt, lam_vecs, subln, *[w for w, _ in riders])
    cast = iter(outs[1:])
    return outs[0], [next(cast) if r is not None else w.astype(BF16) for w, r in zip(weights, plans)]


def _swiglu(x_all, mod_ref, gffn_ref, wgu_ref, wd_ref):
    shift = mod_ref[0, 3:4, :]
    gain = gffn_ref[...] * (1.0 + mod_ref[0, 4:5, :])
    gate = mod_ref[0, 5:6, :]
    n = x_all.shape[0]
    nh = n // FFN_SPLIT
    outs = []
    for r0 in range(0, n, nh):
        x1 = x_all[r0:r0 + nh]
        h = _rms(x1, gain) + shift
        gu = jnp.dot(h.astype(BF16), wgu_ref[...], preferred_element_type=F32)
        dff = gu.shape[1] // 2
        g = gu[:, :dff]
        u = gu[:, dff:]
        a = (g * jax.nn.sigmoid(g)) * u
        dn = jnp.dot(a.astype(BF16), wd_ref[...], preferred_element_type=F32)
        outs.append(x1 + gate * dn)
    return outs[0] if len(outs) == 1 else jnp.concatenate(outs, axis=0)


def _mixffn_kernel(o_ref, x_ref, mod_ref, wout_ref, gffn_ref, wgu_ref, wd_ref, out_ref, res_scr):
    o = jnp.concatenate([o_ref[0, u] for u in range(N_QG)], axis=1)
    y = jnp.dot(o, wout_ref[...], preferred_element_type=F32)
    x1 = x_ref[0] + mod_ref[0, 2:3, :] * y
    res = _swiglu(x1, mod_ref, gffn_ref, wgu_ref, wd_ref)
    n = res_scr.shape[1] // RADIX
    for sl in range(res_scr.shape[0]):
        res_scr[sl] = res[:, sl * LANES:(sl + 1) * LANES]
        for r in range(RADIX):
            out_ref[0, r, :, sl * LANES:(sl + 1) * LANES] = res_scr[sl, pl.ds(r, n, stride=RADIX), :]


def _ffn_final_kernel(x_ref, mod_ref, gffn_ref, wgu_ref, wd_ref, gfin_ref, out_ref, x_scr):
    n = x_scr.shape[1] // RADIX
    for sl in range(x_scr.shape[0]):
        for r in range(RADIX):
            x_scr[sl, pl.ds(r, n, stride=RADIX), :] = x_ref[0, r, :, sl * LANES:(sl + 1) * LANES]
    x1 = jnp.concatenate([x_scr[sl] for sl in range(x_scr.shape[0])], axis=1)
    x2 = _swiglu(x1, mod_ref, gffn_ref, wgu_ref, wd_ref)
    out_ref[0] = _rms(x2, gfin_ref[...])


def _mixffn(o, x, mod, wout, gffn, wgu, wd, tm):
    b, s, d = x.shape
    tok = lambda bi, i: (bi, i, 0)
    return pl.pallas_call(
        _mixffn_kernel,
        grid=(b, s // tm),
        in_specs=[pl.BlockSpec((1, N_QG, tm, LANES), lambda bi, i: (bi, 0, i, 0)),
                  pl.BlockSpec((1, tm, d), tok),
                  pl.BlockSpec((1, N_MOD, d), lambda bi, i: (bi, 0, 0)),
                  _const_spec(wout.shape), _const_spec((1, d)), _const_spec(wgu.shape), _const_spec(wd.shape)],
        out_specs=pl.BlockSpec((1, RADIX, tm // RADIX, d), lambda bi, i: (bi, 0, i, 0)),
        out_shape=jax.ShapeDtypeStruct((b, RADIX, s // RADIX, d), F32),
        scratch_shapes=[pltpu.VMEM((d // LANES, tm, LANES), F32)],
        compiler_params=pltpu.CompilerParams(
            dimension_semantics=("parallel", "arbitrary"),
            vmem_limit_bytes=_vmem_limit(56 * 1024 * 1024)),
        name="mixffn",
    )(o, x, mod, wout, gffn, wgu, wd)


def _ffn_final(xr, mod, gffn, wgu, wd, gfin, tm):
    b, _, nq, d = xr.shape
    s = nq * RADIX
    return pl.pallas_call(
        _ffn_final_kernel,
        grid=(b, s // tm),
        in_specs=[pl.BlockSpec((1, RADIX, tm // RADIX, d), lambda bi, i: (bi, 0, i, 0)),
                  pl.BlockSpec((1, N_MOD, d), lambda bi, i: (bi, 0, 0)),
                  _const_spec((1, d)), _const_spec(wgu.shape), _const_spec(wd.shape), _const_spec((1, d))],
        out_specs=pl.BlockSpec((1, tm, d), lambda bi, i: (bi, i, 0)),
        out_shape=jax.ShapeDtypeStruct((b, s, d), F32),
        scratch_shapes=[pltpu.VMEM((d // LANES, tm, LANES), F32)],
        compiler_params=pltpu.CompilerParams(
            dimension_semantics=("parallel", "arbitrary"),
            vmem_limit_bytes=_vmem_limit(56 * 1024 * 1024)),
        name="ffn",
    )(xr, mod, gffn, wgu, wd, gfin)


_QUARTER = ((1, 0), (0, 1), (-1, 0), (0, -1))


def _fproj_kernel(ng, *refs):
    x_refs = refs[:RADIX * ng]
    mod_ref, g_ref, w_ref, u_ref = refs[RADIX * ng:]
    d = x_refs[0].shape[2]
    ta = x_refs[0].shape[1]
    shift = mod_ref[0, 0:1, :]
    gain = g_ref[...] * (1.0 + mod_ref[0, 1:2, :])
    for t in range(ng):
        h = jnp.concatenate([(_rms(xr[0], gain) + shift).astype(BF16)
                             for xr in x_refs[RADIX * t:RADIX * (t + 1)]], axis=0)
        z = jnp.dot(h, w_ref[...], preferred_element_type=F32)
        zc = [z[m * ta:(m + 1) * ta, 0:d] for m in range(RADIX)]
        zs = [z[m * ta:(m + 1) * ta, d:2 * d] for m in range(RADIX)]
        for r in range(RADIX):
            ua = None
            ub = None
            for m in range(RADIX):
                cp, sp = _QUARTER[(r * m) % 4]
                ta_, sa = (zc[m], cp) if cp else (zs[m], -sp)
                tb_, sb = (zs[m], cp) if cp else (zc[m], sp)
                ua = sa * ta_ if ua is None else (ua + ta_ if sa > 0 else ua - ta_)
                ub = sb * tb_ if ub is None else (ub + tb_ if sb > 0 else ub - tb_)
            u_ref[0, r, 0, t * ta:(t + 1) * ta, :] = ua.astype(BF16)
            u_ref[0, r, 1, t * ta:(t + 1) * ta, :] = ub.astype(BF16)


def _fproj(x, mod, g, wcs, ta):
    b, s, d = x.shape
    nq = s // RADIX
    nqq = nq // RADIX
    na = nqq // ta
    nrb = RADIX * na
    ng = 2

    def x_spec(t, m):
        def index(bi, j):
            rb = j * ng + t
            return (bi, (rb // na) * (RADIX * na) + m * na + rb % na, 0)
        return pl.BlockSpec((1, ta, d), index)

    return pl.pallas_call(
        functools.partial(_fproj_kernel, ng),
        grid=(b, nrb // ng),
        in_specs=[x_spec(t, m) for t in range(ng) for m in range(RADIX)]
        + [pl.BlockSpec((1, N_MOD, d), lambda bi, j: (bi, 0, 0)),
           _const_spec((1, d)), _const_spec(wcs.shape)],
        out_specs=pl.BlockSpec((1, RADIX, 2, ng * ta, d), lambda bi, j: (bi, 0, 0, j, 0)),
        out_shape=jax.ShapeDtypeStruct((b, RADIX, 2, nq, d), BF16),
        compiler_params=pltpu.CompilerParams(
            dimension_semantics=("parallel", "arbitrary"),
            vmem_limit_bytes=_vmem_limit(48 * 1024 * 1024)),
        name="fproj",
    )(*([x] * (RADIX * ng)), mod, g, wcs)


def _fseq_kernel(u_ref, t_ref, x_ref, mod_ref, out_ref):
    nq, d = u_ref.shape[3], u_ref.shape[4]
    u = u_ref[0, 0].reshape(2 * nq, d)
    y = jnp.dot(t_ref[0], u, preferred_element_type=F32)
    out_ref[0] = x_ref[0] + mod_ref[0, 2:3, :] * y


def _seq_dft_tables(n):
    nq = n // RADIX
    nqq = nq // RADIX
    u = np.arange(nq)
    q = (RADIX * (u % nqq) + u // nqq)[None, None, :]
    p = np.arange(nq)[None, :, None]
    r = np.arange(RADIX)[:, None, None]
    ang = 2.0 * np.pi * (((RADIX * p + r) * q) % n) / n
    t = np.concatenate([np.cos(ang), -np.sin(ang)], axis=2) / np.sqrt(n)
    return jnp.asarray(t, F32)


def _fseq(u, t, x, mod, tmr):
    b, s, d = x.shape
    nq = s // RADIX
    nblk = nq // tmr
    return pl.pallas_call(
        _fseq_kernel,
        grid=(b, RADIX, nblk),
        in_specs=[pl.BlockSpec((1, 1, 2, nq, d), lambda bi, r, i: (bi, r, 0, 0, 0)),
                  pl.BlockSpec((1, tmr, 2 * nq), lambda bi, r, i: (r, i, 0)),
                  pl.BlockSpec((1, tmr, d), lambda bi, r, i: (bi, r * nblk + i, 0)),
                  pl.BlockSpec((1, N_MOD, d), lambda bi, r, i: (bi, 0, 0))],
        out_specs=pl.BlockSpec((1, tmr, d), lambda bi, r, i: (bi, r * nblk + i, 0)),
        out_shape=jax.ShapeDtypeStruct((b, s, d), F32),
        compiler_params=pltpu.CompilerParams(
            dimension_semantics=("parallel", "arbitrary", "arbitrary"),
            vmem_limit_bytes=_vmem_limit(40 * 1024 * 1024)),
        name="fseq",
    )(u, t, x, mod)


def _rope_lane_tables(n_ctx, n_tok):
    rows_count = n_tok // GRID_W
    row = np.repeat(np.arange(rows_count), GRID_W).astype(np.float32)
    col = np.tile(np.arange(GRID_W), rows_count).astype(np.float32)
    inv = np.float32(ROPE_THETA) ** (-np.arange(ROPE_PAIRS, dtype=np.float32) / np.float32(ROPE_PAIRS))
    lane = np.arange(LANES)
    dd = lane % HEAD_DIM
    axis = dd // (2 * ROPE_PAIRS)
    half = (dd % (2 * ROPE_PAIRS)) // ROPE_PAIRS
    pair = dd % ROPE_PAIRS
    pos = np.where((axis == 0)[None, :], row[:, None], col[:, None])
    ang = (pos * inv.astype(np.float32)[pair][None, :]).astype(np.float32).astype(np.float64)
    cosv = np.cos(ang)
    sinv = np.sin(ang)
    first = (half == 0)[None, :]
    s1 = np.where(first, -sinv, 0.0)
    s2 = np.where(first, 0.0, sinv)
    pad1 = np.ones((n_ctx, LANES))
    pad0 = np.zeros((n_ctx, LANES))
    return tuple(jnp.asarray(np.concatenate([tab, pad], 0), F32)
                 for pad, tab in ((pad1, cosv), (pad0, s1), (pad0, s2)))


def _permute_w_in(w_in):
    d = w_in.shape[0]
    off_ak = A_W
    off_av = 2 * A_W
    off_bq = 3 * A_W
    off_bk = off_bq + BQ_W
    off_bv = off_bk + BKV_W
    grp = GQA_HEADS // GQA_KV_HEADS
    bq = w_in[:, off_bq:off_bk].reshape(d, GQA_KV_HEADS, grp, HEAD_DIM).swapaxes(1, 2).reshape(d, BQ_W)
    return jnp.concatenate([w_in[:, 0:A_W], bq, w_in[:, off_ak:off_av], w_in[:, off_bk:off_bv],
                            w_in[:, off_av:off_bq], w_in[:, off_bv:off_bv + BKV_W]], axis=1)


def _permute_w_out(w_out):
    d = w_out.shape[1]
    grp = GQA_HEADS // GQA_KV_HEADS
    wb = w_out[A_W:].reshape(GQA_KV_HEADS, grp, HEAD_DIM, d).swapaxes(0, 1).reshape(BQ_W, d)
    return jnp.concatenate([w_out[:A_W], wb], axis=0)


def kernel(x, c, ctx, c_ctx, l0_ada_w, l0_ada_b, l0_norm_mix, l0_w_in, l0_lambda_q1, l0_lambda_k1, l0_lambda_q2, l0_lambda_k2, l0_subln, l0_q_norm, l0_k_norm, l0_w_out, l0_norm_ffn, l0_w_gate_up, l0_w_down, l1_ada_w, l1_ada_b, l1_norm_mix, l1_w_out, l1_norm_ffn, l1_w_gate_up, l1_w_down, final_norm):
    b, s, d = x.shape
    n_ctx = ctx.shape[1]
    assert d == D_MODEL and s % GRID_W == 0

    tt = min(256, n_ctx, s)
    tq = min(128, tt)
    nsub = max(n for n in (1, 2, 4) if s % (n * tq) == 0)
    tk = min(256, n_ctx)
    tm = min(512, s)
    tmr = min(512, s // RADIX)
    ta = min(128, s // RADIX ** 2)
    assert n_ctx % tt == 0 and s % tt == 0 and tt % tq == 0 and (n_ctx + s) % tk == 0
    assert s % tm == 0 and (s // RADIX) % tmr == 0 and (s // RADIX ** 2) % ta == 0

    n_rows = -(-(b + 1) // SUBLANES) * SUBLANES
    cond = jnp.concatenate([c, c_ctx[None, :], jnp.zeros((n_rows - b - 1, d), F32)], axis=0)
    m0, m1, wcs = _ada_fold(cond, l0_ada_w, l0_ada_b, l1_ada_w, l1_ada_b, l1_w_out)
    mod0 = m0[:b].reshape(b, N_MOD, d)
    mod0c = m0[b:b + 1].reshape(1, N_MOD, d)
    mod1 = m1[:b].reshape(b, N_MOD, d)

    w_perm = _permute_w_in(l0_w_in)
    cos_t, s1_t, s2_t = _rope_lane_tables(n_ctx, s)
    qg = jnp.tile(l0_q_norm, LANES // HEAD_DIM).reshape(1, LANES)
    kg = jnp.tile(l0_k_norm, LANES // HEAD_DIM).reshape(1, LANES)
    li = np.arange(LANES)
    ones_bd = jnp.asarray((li[:, None] // HEAD_DIM) == (li[None, :] // HEAD_DIM), BF16)
    qt, k, vt = _proj(x, ctx, mod0, mod0c, l0_norm_mix.reshape(1, d), w_perm.astype(BF16),
                      cos_t, s1_t, s2_t, qg, kg, ones_bd, tt, tq)
    lam_vecs = jnp.stack([l0_lambda_q1, l0_lambda_k1, l0_lambda_q2, l0_lambda_k2]).astype(F32)
    o, (w_out0, w_gu0, w_dn0, w_gu1, w_dn1) = _attn(
        qt, k, vt, lam_vecs, l0_subln.reshape(1, LANES), s, tq, tk, nsub,
        [_permute_w_out(l0_w_out), l0_w_gate_up, l0_w_down, l1_w_gate_up, l1_w_down])
    xr = _mixffn(o, x, mod0, w_out0, l0_norm_ffn.reshape(1, d), w_gu0, w_dn0, tm)
    xr = xr.reshape(b, s, d)

    u = _fproj(xr, mod1, l1_norm_mix.reshape(1, d), wcs, ta)
    xr = _fseq(u, _seq_dft_tables(s).astype(BF16), xr, mod1, tmr)
    return _ffn_final(xr.reshape(b, RADIX, s // RADIX, d), mod1, l1_norm_ffn.reshape(1, d),
                      w_gu1, w_dn1, final_norm.reshape(1, d), tm)
```

```python
import functools
import math

import numpy as np
import jax
import jax.numpy as jnp
from jax import lax
from jax.experimental import pallas as pl
from jax.experimental.pallas import tpu as pltpu

F32 = jnp.float32
BF16 = jnp.bfloat16

LANES = 128
SUBLANES = 8
BF16_SUBLANES = 16
V7X_VMEM_BYTES = 64 * 1024 * 1024

D_MODEL = 1024
GRID_W = 64
DIFF_HEADS = 4
DIFF_HEAD_DIM = 64
GQA_HEADS = 8
GQA_KV_HEADS = 2
GQA_HEAD_DIM = 64
ROPE_THETA = 10000.0
ROPE_PAIRS = GQA_HEAD_DIM // 4
FOURIER_GROUPS = 8
FOURIER_GROUP_DIM = D_MODEL // FOURIER_GROUPS
EPS = 1e-6
N_MOD = 6
HEAD_DIM = 64
A_W = DIFF_HEADS * 2 * DIFF_HEAD_DIM
BQ_W = GQA_HEADS * GQA_HEAD_DIM
BKV_W = GQA_KV_HEADS * GQA_HEAD_DIM
Q_W = A_W + BQ_W
K_W = A_W + BKV_W
N_QG = Q_W // LANES
N_KG = K_W // LANES
LAM_INIT_L0 = 0.8 - 0.6 * math.exp(-0.3 * 0)
Q_SCALE = HEAD_DIM ** -0.5 * math.log2(math.e)
RADIX = 4
ACC_ROWS = 32
FFN_SPLIT = 2
VT_ROWS = LANES + BF16_SUBLANES


def _vmem_limit(nbytes):
    return int(min(nbytes, V7X_VMEM_BYTES - 4 * 1024 * 1024))


def _rms(x, g):
    ms = jnp.mean(x * x, axis=-1, keepdims=True)
    return x * lax.rsqrt(ms + EPS) * g


def _const_spec(shape):
    nd = len(shape)
    return pl.BlockSpec(shape, lambda *_: (0,) * nd, pipeline_mode=pl.Buffered(1))


def _fold_kernel(cd_ref, sd_ref, w_ref, o_ref):
    w = w_ref[...]
    d = w.shape[1]
    w_hi = w.astype(BF16)
    w_lo = (w - w_hi.astype(F32)).astype(BF16)

    def dot3(m):
        m_hi = m.astype(BF16)
        m_lo = (m - m_hi.astype(F32)).astype(BF16)
        return (jnp.dot(m_hi, w_hi, preferred_element_type=F32) + jnp.dot(m_hi, w_lo, preferred_element_type=F32)
                + jnp.dot(m_lo, w_hi, preferred_element_type=F32))

    o_ref[:, 0:d] = dot3(cd_ref[...]).astype(BF16)
    o_ref[:, d:2 * d] = dot3(sd_ref[...]).astype(BF16)


def _ada_kernel(c_ref, w0_ref, b0_ref, w1_ref, b1_ref, cd_ref, sd_ref, wo_ref, o0_ref, o1_ref, wcs_ref):
    cv = c_ref[...]
    a = (cv * jax.nn.sigmoid(cv)).astype(BF16)
    o0_ref[...] = jnp.dot(a, w0_ref[...].astype(BF16), preferred_element_type=F32) + b0_ref[...]
    o1_ref[...] = jnp.dot(a, w1_ref[...].astype(BF16), preferred_element_type=F32) + b1_ref[...]
    _fold_kernel(cd_ref, sd_ref, wo_ref, wcs_ref)


def _ada_fold(cond, w0, b0, w1, b1, w_out):
    r, d = cond.shape
    n = w0.shape[1]
    bn = 512
    gd = FOURIER_GROUP_DIM
    ngr = w_out.shape[0] // gd
    assert n // bn >= ngr
    idx = np.arange(gd)
    ang = 2.0 * np.pi * ((idx[:, None] * idx[None, :]) % gd) / gd
    cd = jnp.asarray(np.cos(ang) / np.sqrt(gd), F32)
    sd = jnp.asarray(np.sin(ang) / np.sqrt(gd), F32)
    grp = lambda j: (jnp.minimum(j, ngr - 1), 0)
    return pl.pallas_call(
        _ada_kernel,
        grid=(n // bn,),
        in_specs=[pl.BlockSpec((r, d), lambda j: (0, 0)),
                  pl.BlockSpec((d, bn), lambda j: (0, j)),
                  pl.BlockSpec((1, bn), lambda j: (0, j)),
                  pl.BlockSpec((d, bn), lambda j: (0, j)),
                  pl.BlockSpec((1, bn), lambda j: (0, j)),
                  pl.BlockSpec((gd, gd), lambda j: (0, 0)),
                  pl.BlockSpec((gd, gd), lambda j: (0, 0)),
                  pl.BlockSpec((gd, w_out.shape[1]), grp)],
        out_specs=[pl.BlockSpec((r, bn), lambda j: (0, j)),
                   pl.BlockSpec((r, bn), lambda j: (0, j)),
                   pl.BlockSpec((gd, 2 * w_out.shape[1]), grp)],
        out_shape=[jax.ShapeDtypeStruct((r, n), F32), jax.ShapeDtypeStruct((r, n), F32),
                   jax.ShapeDtypeStruct((w_out.shape[0], 2 * w_out.shape[1]), BF16)],
        compiler_params=pltpu.CompilerParams(dimension_semantics=("arbitrary",)),
        name="ada",
    )(cond, w0, b0.reshape(1, n), w1, b1.reshape(1, n), cd, sd, w_out)


def _proj_kernel(nlat, nct, tq, ng, *refs):
    x_refs, ctx_refs = refs[:ng], refs[ng:2 * ng]
    (mod_ref, modc_ref, g_ref, w_ref, cos_ref, s1_ref, s2_ref,
     qg_ref, kg_ref, ones_ref, qt_ref, k_ref, vt_ref) = refs[2 * ng:]
    j = pl.program_id(1)
    gain_x = g_ref[...] * (1.0 + mod_ref[0, 1:2, :])
    gain_c = g_ref[...] * (1.0 + modc_ref[0, 1:2, :])
    hs = []
    for t in range(ng):
        if ng - 1 - t >= nct:
            xin, gain, shift = x_refs[t][0], gain_x, mod_ref[0, 0:1, :]
        else:
            is_ctx = j * ng + t >= nlat
            xin = jnp.where(is_ctx, ctx_refs[t][0], x_refs[t][0])
            gain = jnp.where(is_ctx, gain_c, gain_x)
            shift = jnp.where(is_ctx, modc_ref[0, 0:1, :], mod_ref[0, 0:1, :])
        ms = jnp.mean(xin * xin, axis=-1, keepdims=True)
        hs.append((xin * lax.rsqrt(ms + EPS) * gain + shift).astype(BF16))
    h = jnp.concatenate(hs, axis=0)

    tt = h.shape[0]
    cosv = cos_ref[...]
    s1 = s1_ref[...]
    s2 = s2_ref[...]
    row = lax.broadcasted_iota(jnp.int32, (LANES, tt), 0)
    lo_rows = row < HEAD_DIM

    def rope(xs):
        return xs * cosv + pltpu.roll(xs, LANES - ROPE_PAIRS, 1) * s1 + pltpu.roll(xs, ROPE_PAIRS, 1) * s2

    def headnorm(xs, gain):
        sq = xs * xs
        hi = sq.astype(BF16)
        lo = (sq - hi.astype(F32)).astype(BF16)
        ss = (jnp.dot(hi, ones_ref[...], preferred_element_type=F32)
              + jnp.dot(lo, ones_ref[...], preferred_element_type=F32))
        return xs * lax.rsqrt(ss * (1.0 / HEAD_DIM) + EPS) * gain

    def finish_slab(idx, xs):
        if idx < N_QG:
            s = idx
            if s >= A_W // LANES:
                xs = headnorm(xs, qg_ref[...])
            qst = (rope(xs) * Q_SCALE).T
            q_lo = jnp.where(lo_rows, qst, 0.0)
            q_hi = qst - q_lo
            for a in range(tt // tq):
                qt_ref[0, s, :, 2 * a * tq:(2 * a + 1) * tq] = q_lo[:, a * tq:(a + 1) * tq].astype(BF16)
                qt_ref[0, s, :, (2 * a + 1) * tq:(2 * a + 2) * tq] = q_hi[:, a * tq:(a + 1) * tq].astype(BF16)
        elif idx < N_QG + N_KG:
            s = idx - N_QG
            if s >= A_W // LANES:
                xs = headnorm(xs, kg_ref[...])
            k_ref[0, s] = rope(xs).astype(BF16)
        else:
            s = idx - N_QG - N_KG
            vt_ref[0, s, 0:LANES, :] = xs.T.astype(BF16)
            vt_ref[0, s, LANES:VT_ROWS, :] = jnp.ones((VT_ROWS - LANES, tt), BF16)

    gw = 2 * LANES
    n_groups = w_ref.shape[1] // gw
    pending = None
    for gi in range(n_groups + 1):
        z = None
        if gi < n_groups:
            z = jnp.dot(h, w_ref[:, gi * gw:(gi + 1) * gw], preferred_element_type=F32)
        if pending is not None:
            pg, pz = pending
            for half in range(gw // LANES):
                finish_slab(pg * (gw // LANES) + half, pz[:, half * LANES:(half + 1) * LANES])
        pending = (gi, z)


def _proj(x, ctx, mod, modc, g, w_perm, cos_t, s1_t, s2_t, qg, kg, ones_bd, tt, tq):
    b, s, d = x.shape
    nctx = ctx.shape[1]
    nct = nctx // tt
    nk = nctx + s
    nj = nk // tt
    ng = max(gsz for gsz in (1, 2, 3) if nj % gsz == 0)
    rows = ng * tt
    n_in = w_perm.shape[1]
    nlat = nj - nct
    kern = functools.partial(_proj_kernel, nlat, nct, tq, ng)

    def x_spec(t):
        return pl.BlockSpec((1, tt, d), lambda bi, j: (bi, jnp.minimum(j * ng + t, nlat - 1), 0))

    def ctx_spec(t):
        return pl.BlockSpec((1, tt, d), lambda bi, j: (bi, jnp.maximum(j * ng + t - nlat, 0), 0))

    tab_spec = pl.BlockSpec((rows, LANES), lambda bi, j: (j, 0))
    return pl.pallas_call(
        kern,
        grid=(b, nj // ng),
        in_specs=[x_spec(t) for t in range(ng)] + [ctx_spec(t) for t in range(ng)]
        + [pl.BlockSpec((1, N_MOD, d), lambda bi, j: (bi, 0, 0)),
           _const_spec((1, N_MOD, d)),
           _const_spec((1, d)),
           _const_spec((d, n_in)),
           tab_spec, tab_spec, tab_spec,
           _const_spec((1, LANES)),
           _const_spec((1, LANES)),
           _const_spec((LANES, LANES))],
        out_specs=[pl.BlockSpec((1, N_QG, LANES, 2 * rows), lambda bi, j: (bi, 0, 0, j)),
                   pl.BlockSpec((1, N_KG, rows, LANES), lambda bi, j: (bi, 0, j, 0)),
                   pl.BlockSpec((1, N_KG, VT_ROWS, rows), lambda bi, j: (bi, 0, 0, j))],
        out_shape=[jax.ShapeDtypeStruct((b, N_QG, LANES, 2 * nk), BF16),
                   jax.ShapeDtypeStruct((b, N_KG, nk, LANES), BF16),
                   jax.ShapeDtypeStruct((b, N_KG, VT_ROWS, nk), BF16)],
        compiler_params=pltpu.CompilerParams(
            dimension_semantics=("parallel", "arbitrary"),
            vmem_limit_bytes=_vmem_limit(56 * 1024 * 1024)),
        name="proj",
    )(*([x] * ng), *([ctx] * ng), mod, modc, g, w_perm, cos_t, s1_t, s2_t, qg, kg, ones_bd)


def _attn_kernel(tq, tk, nsub, ncast, qt_ref, k_ref, vt_ref, lam_ref, subln_ref, *rest):
    w_refs, o_ref, wb_refs, bufs = rest[:ncast], rest[ncast], rest[ncast + 1:2 * ncast + 1], rest[2 * ncast + 1:]
    for w_ref, wb_ref in zip(w_refs, wb_refs):
        wb_ref[...] = w_ref[...].astype(BF16)
    nk = k_ref.shape[2]
    nc = nk // tk
    tq2 = 2 * tq
    row = lax.broadcasted_iota(jnp.int32, (LANES, tq), 0)
    lo_rows = row < HEAD_DIM
    lv = lam_ref[...]
    lam = (jnp.exp(jnp.sum(lv[0:1] * lv[1:2], axis=1, keepdims=True))
           - jnp.exp(jnp.sum(lv[2:3] * lv[3:4], axis=1, keepdims=True)) + LAM_INIT_L0)

    n_a = A_W // LANES

    s_bufs, p_bufs = bufs[:3], bufs[3:]

    def item(w):
        return divmod(w, N_QG)

    def score_chunk(w, c, m8):
        qi, u = item(w)
        kv = min(u, n_a)
        s_c = jnp.dot(k_ref[0, kv, c * tk:(c + 1) * tk, :], qt_ref[0, u, :, qi * tq2:(qi + 1) * tq2],
                      preferred_element_type=F32)
        s_bufs[w % 3][c * tk:(c + 1) * tk, :] = s_c
        mc = jnp.max(s_c.reshape(tk // ACC_ROWS, ACC_ROWS, tq2), axis=0)
        return mc if m8 is None else jnp.maximum(m8, mc)

    def exp_chunk(w, c, m):
        p_bufs[w % 2][c * tk:(c + 1) * tk, :] = jnp.exp2((s_bufs[w % 3][c * tk:(c + 1) * tk, :] - m).astype(BF16))

    def value_chunk(w, c, acc):
        kv = min(item(w)[1], n_a)
        pv = jnp.dot(vt_ref[0, kv, :, c * tk:(c + 1) * tk], p_bufs[w % 2][c * tk:(c + 1) * tk, :],
                     preferred_element_type=F32)
        return pv if acc is None else acc + pv

    def finish(w, acc):
        qi, u = item(w)
        ot = acc[0:LANES] * (1.0 / acc[LANES:LANES + 1])
        o1, o2 = ot[:, :tq], ot[:, tq:]
        if u < n_a:
            ot = o1 - lam * o2
            ot = ot * lax.rsqrt(jnp.mean(ot * ot, axis=0, keepdims=True) + EPS)
            o_ref[0, u, qi * tq:(qi + 1) * tq, :] = (ot.T * (subln_ref[...] * (1.0 - LAM_INIT_L0))).astype(BF16)
        else:
            o_ref[0, u, qi * tq:(qi + 1) * tq, :] = jnp.where(lo_rows, o1, o2).T.astype(BF16)

    n_items = nsub * N_QG
    m8s = {}
    ms = {}
    accs = {}
    for t in range(n_items + 2):
        if 0 <= t - 1 < n_items:
            ms[t - 1] = jnp.max(m8s.pop(t - 1), axis=0, keepdims=True)
        for c in range(nc):
            pace = m8s.get(t) if t < n_items else accs.get(t - 2)
            pace = None if pace is None else pace[0:1]
            if 0 <= t - 2 < n_items:
                accs[t - 2] = value_chunk(t - 2, c, accs.get(t - 2))
            if t < n_items:
                m8s[t] = score_chunk(t, c, m8s.get(t))
            if 0 <= t - 1 < n_items:
                m = ms[t - 1]
                if pace is not None:
                    m = m + jnp.minimum(jnp.abs(pace), 0.0)
                exp_chunk(t - 1, c, m)
        if 0 <= t - 2 < n_items:
            finish(t - 2, accs.pop(t - 2))


def _cast_rows(n_rows, n_steps):
    best = None
    for r in range(BF16_SUBLANES, n_rows + 1, BF16_SUBLANES):
        if n_rows % r == 0 and n_steps % (n_rows // r) == 0:
            best = r if best is None else min(best, r)
    return best


def _attn(qt, k, vt, lam_vecs, subln, s, tq, tk, nsub, weights):
    b = qt.shape[0]
    nk = k.shape[2]
    tqs = tq * nsub
    ni = s // tqs
    plans = [_cast_rows(w.shape[0], b * ni) for w in weights]
    riders = [(w, r) for w, r in zip(weights, plans) if r is not None]
    kern = functools.partial(_attn_kernel, tq, tk, nsub, len(riders))

    def w_spec(w, r):
        rep = (b * ni) // (w.shape[0] // r)
        return pl.BlockSpec((r, w.shape[1]), lambda bi, i: ((bi * ni + i) // rep, 0))

    outs = pl.pallas_call(
        kern,
        grid=(b, ni),
        in_specs=[pl.BlockSpec((1, N_QG, LANES, 2 * tqs), lambda bi, i: (bi, 0, 0, i)),
                  pl.BlockSpec((1, N_KG, nk, LANES), lambda bi, i: (bi, 0, 0, 0)),
                  pl.BlockSpec((1, N_KG, VT_ROWS, nk), lambda bi, i: (bi, 0, 0, 0)),
                  _const_spec((4, HEAD_DIM)),
                  _const_spec((1, LANES))] + [w_spec(w, r) for w, r in riders],
        out_specs=[pl.BlockSpec((1, N_QG, tqs, LANES), lambda bi, i: (bi, 0, i, 0))]
        + [w_spec(w, r) for w, r in riders],
        out_shape=[jax.ShapeDtypeStruct((b, N_QG, s, LANES), BF16)]
        + [jax.ShapeDtypeStruct(w.shape, BF16) for w, _ in riders],
        scratch_shapes=[pltpu.VMEM((nk, 2 * tq), F32)] * 3 + [pltpu.VMEM((nk, 2 * tq), BF16)] * 2,
        compiler_params=pltpu.CompilerParams(
            dimension_semantics=("arbitrary", "arbitrary"),
            vmem_limit_bytes=_vmem_limit(48 * 1024 * 1024)),
        name="attn",
    )(qt, k, vt, lam_vecs, subln, *[w for w, _ in riders])
    cast = iter(outs[1:])
    return outs[0], [next(cast) if r is not None else w.astype(BF16) for w, r in zip(weights, plans)]


def _swiglu(x_all, mod_ref, gffn_ref, wgu_ref, wd_ref):
    shift = mod_ref[0, 3:4, :]
    gain = gffn_ref[...] * (1.0 + mod_ref[0, 4:5, :])
    gate = mod_ref[0, 5:6, :]
    n = x_all.shape[0]
    nh = n // FFN_SPLIT
    outs = []
    for r0 in range(0, n, nh):
        x1 = x_all[r0:r0 + nh]
        h = _rms(x1, gain) + shift
        gu = jnp.dot(h.astype(BF16), wgu_ref[...], preferred_element_type=F32)
        dff = gu.shape[1] // 2
        g = gu[:, :dff]
        u = gu[:, dff:]
        a = (g * jax.nn.sigmoid(g)) * u
        dn = jnp.dot(a.astype(BF16), wd_ref[...], preferred_element_type=F32)
        outs.append(x1 + gate * dn)
    return outs[0] if len(outs) == 1 else jnp.concatenate(outs, axis=0)


def _mixffn_kernel(o_ref, x_ref, mod_ref, wout_ref, gffn_ref, wgu_ref, wd_ref, out_ref, res_scr):
    o = jnp.concatenate([o_ref[0, u] for u in range(N_QG)], axis=1)
    y = jnp.dot(o, wout_ref[...], preferred_element_type=F32)
    x1 = x_ref[0] + mod_ref[0, 2:3, :] * y
    res = _swiglu(x1, mod_ref, gffn_ref, wgu_ref, wd_ref)
    n = res_scr.shape[1] // RADIX
    for sl in range(res_scr.shape[0]):
        res_scr[sl] = res[:, sl * LANES:(sl + 1) * LANES]
        for r in range(RADIX):
            out_ref[0, r, :, sl * LANES:(sl + 1) * LANES] = res_scr[sl, pl.ds(r, n, stride=RADIX), :]


def _ffn_final_kernel(x_ref, y_ref, mod_ref, gffn_ref, wgu_ref, wd_ref, gfin_ref, out_ref, x_scr):
    n = x_scr.shape[1] // RADIX
    gate = mod_ref[0, 2:3, :]
    for sl in range(x_scr.shape[0]):
        cols = slice(sl * LANES, (sl + 1) * LANES)
        for r in range(RADIX):
            x_scr[sl, pl.ds(r, n, stride=RADIX), :] = (x_ref[0, r, :, cols]
                                                       + gate[:, cols] * y_ref[0, r, :, cols].astype(F32))
    x1 = jnp.concatenate([x_scr[sl] for sl in range(x_scr.shape[0])], axis=1)
    x2 = _swiglu(x1, mod_ref, gffn_ref, wgu_ref, wd_ref)
    out_ref[0] = _rms(x2, gfin_ref[...])


def _mixffn(o, x, mod, wout, gffn, wgu, wd, tm):
    b, s, d = x.shape
    tok = lambda bi, i: (bi, i, 0)
    return pl.pallas_call(
        _mixffn_kernel,
        grid=(b, s // tm),
        in_specs=[pl.BlockSpec((1, N_QG, tm, LANES), lambda bi, i: (bi, 0, i, 0)),
                  pl.BlockSpec((1, tm, d), tok),
                  pl.BlockSpec((1, N_MOD, d), lambda bi, i: (bi, 0, 0)),
                  _const_spec(wout.shape), _const_spec((1, d)), _const_spec(wgu.shape), _const_spec(wd.shape)],
        out_specs=pl.BlockSpec((1, RADIX, tm // RADIX, d), lambda bi, i: (bi, 0, i, 0)),
        out_shape=jax.ShapeDtypeStruct((b, RADIX, s // RADIX, d), F32),
        scratch_shapes=[pltpu.VMEM((d // LANES, tm, LANES), F32)],
        compiler_params=pltpu.CompilerParams(
            dimension_semantics=("parallel", "arbitrary"),
            vmem_limit_bytes=_vmem_limit(56 * 1024 * 1024)),
        name="mixffn",
    )(o, x, mod, wout, gffn, wgu, wd)


def _ffn_final(xr, yr, mod, gffn, wgu, wd, gfin, tm):
    b, _, nq, d = xr.shape
    s = nq * RADIX
    res = pl.BlockSpec((1, RADIX, tm // RADIX, d), lambda bi, i: (bi, 0, i, 0))
    return pl.pallas_call(
        _ffn_final_kernel,
        grid=(b, s // tm),
        in_specs=[res, res,
                  pl.BlockSpec((1, N_MOD, d), lambda bi, i: (bi, 0, 0)),
                  _const_spec((1, d)), _const_spec(wgu.shape), _const_spec(wd.shape), _const_spec((1, d))],
        out_specs=pl.BlockSpec((1, tm, d), lambda bi, i: (bi, i, 0)),
        out_shape=jax.ShapeDtypeStruct((b, s, d), F32),
        scratch_shapes=[pltpu.VMEM((d // LANES, tm, LANES), F32)],
        compiler_params=pltpu.CompilerParams(
            dimension_semantics=("parallel", "arbitrary"),
            vmem_limit_bytes=_vmem_limit(56 * 1024 * 1024)),
        name="ffn",
    )(xr, yr, mod, gffn, wgu, wd, gfin)


_QUARTER = ((1, 0), (0, 1), (-1, 0), (0, -1))


def _fproj_kernel(ng, *refs):
    x_refs = refs[:RADIX * ng]
    mod_ref, g_ref, w_ref, u_ref = refs[RADIX * ng:]
    d = x_refs[0].shape[2]
    ta = x_refs[0].shape[1]
    shift = mod_ref[0, 0:1, :]
    gain = g_ref[...] * (1.0 + mod_ref[0, 1:2, :])
    for t in range(ng):
        h = jnp.concatenate([(_rms(xr[0], gain) + shift).astype(BF16)
                             for xr in x_refs[RADIX * t:RADIX * (t + 1)]], axis=0)
        z = jnp.dot(h, w_ref[...], preferred_element_type=F32)
        zc = [z[m * ta:(m + 1) * ta, 0:d] for m in range(RADIX)]
        zs = [z[m * ta:(m + 1) * ta, d:2 * d] for m in range(RADIX)]
        for r in range(RADIX):
            ua = None
            ub = None
            for m in range(RADIX):
                cp, sp = _QUARTER[(r * m) % 4]
                ta_, sa = (zc[m], cp) if cp else (zs[m], -sp)
                tb_, sb = (zs[m], cp) if cp else (zc[m], sp)
                ua = sa * ta_ if ua is None else (ua + ta_ if sa > 0 else ua - ta_)
                ub = sb * tb_ if ub is None else (ub + tb_ if sb > 0 else ub - tb_)
            u_ref[0, r, 0, t * ta:(t + 1) * ta, :] = ua.astype(BF16)
            u_ref[0, r, 1, t * ta:(t + 1) * ta, :] = ub.astype(BF16)


def _fproj(x, mod, g, wcs, ta):
    b, s, d = x.shape
    nq = s // RADIX
    nqq = nq // RADIX
    na = nqq // ta
    nrb = RADIX * na
    ng = 2

    def x_spec(t, m):
        def index(bi, j):
            rb = j * ng + t
            return (bi, (rb // na) * (RADIX * na) + m * na + rb % na, 0)
        return pl.BlockSpec((1, ta, d), index)

    return pl.pallas_call(
        functools.partial(_fproj_kernel, ng),
        grid=(b, nrb // ng),
        in_specs=[x_spec(t, m) for t in range(ng) for m in range(RADIX)]
        + [pl.BlockSpec((1, N_MOD, d), lambda bi, j: (bi, 0, 0)),
           _const_spec((1, d)), _const_spec(wcs.shape)],
        out_specs=pl.BlockSpec((1, RADIX, 2, ng * ta, d), lambda bi, j: (bi, 0, 0, j, 0)),
        out_shape=jax.ShapeDtypeStruct((b, RADIX, 2, nq, d), BF16),
        compiler_params=pltpu.CompilerParams(
            dimension_semantics=("parallel", "arbitrary"),
            vmem_limit_bytes=_vmem_limit(48 * 1024 * 1024)),
        name="fproj",
    )(*([x] * (RADIX * ng)), mod, g, wcs)


def _fseq_kernel(u_ref, t_ref, y_ref):
    nq, d = u_ref.shape[3], u_ref.shape[4]
    u = u_ref[0, 0].reshape(2 * nq, d)
    y_ref[0] = jnp.dot(t_ref[0], u, preferred_element_type=F32).astype(BF16)


def _seq_dft_tables(n):
    nq = n // RADIX
    nqq = nq // RADIX
    u = np.arange(nq)
    q = (RADIX * (u % nqq) + u // nqq)[None, None, :]
    p = np.arange(nq)[None, :, None]
    r = np.arange(RADIX)[:, None, None]
    ang = 2.0 * np.pi * (((RADIX * p + r) * q) % n) / n
    t = np.concatenate([np.cos(ang), -np.sin(ang)], axis=2) / np.sqrt(n)
    return jnp.asarray(t, F32)


def _fseq(u, t, tmr):
    b, _, _, nq, d = u.shape
    nblk = nq // tmr
    return pl.pallas_call(
        _fseq_kernel,
        grid=(b, RADIX, nblk),
        in_specs=[pl.BlockSpec((1, 1, 2, nq, d), lambda bi, r, i: (bi, r, 0, 0, 0)),
                  pl.BlockSpec((1, tmr, 2 * nq), lambda bi, r, i: (r, i, 0))],
        out_specs=pl.BlockSpec((1, tmr, d), lambda bi, r, i: (bi, r * nblk + i, 0)),
        out_shape=jax.ShapeDtypeStruct((b, RADIX * nq, d), BF16),
        compiler_params=pltpu.CompilerParams(
            dimension_semantics=("parallel", "arbitrary", "arbitrary"),
            vmem_limit_bytes=_vmem_limit(40 * 1024 * 1024)),
        name="fseq",
    )(u, t)


def _rope_lane_tables(n_ctx, n_tok):
    rows_count = n_tok // GRID_W
    row = np.repeat(np.arange(rows_count), GRID_W).astype(np.float32)
    col = np.tile(np.arange(GRID_W), rows_count).astype(np.float32)
    inv = np.float32(ROPE_THETA) ** (-np.arange(ROPE_PAIRS, dtype=np.float32) / np.float32(ROPE_PAIRS))
    lane = np.arange(LANES)
    dd = lane % HEAD_DIM
    axis = dd // (2 * ROPE_PAIRS)
    half = (dd % (2 * ROPE_PAIRS)) // ROPE_PAIRS
    pair = dd % ROPE_PAIRS
    pos = np.where((axis == 0)[None, :], row[:, None], col[:, None])
    ang = (pos * inv.astype(np.float32)[pair][None, :]).astype(np.float32).astype(np.float64)
    cosv = np.cos(ang)
    sinv = np.sin(ang)
    first = (half == 0)[None, :]
    s1 = np.where(first, -sinv, 0.0)
    s2 = np.where(first, 0.0, sinv)
    pad1 = np.ones((n_ctx, LANES))
    pad0 = np.zeros((n_ctx, LANES))
    return tuple(jnp.asarray(np.concatenate([tab, pad], 0), F32)
                 for pad, tab in ((pad1, cosv), (pad0, s1), (pad0, s2)))


def _permute_w_in(w_in):
    d = w_in.shape[0]
    off_ak = A_W
    off_av = 2 * A_W
    off_bq = 3 * A_W
    off_bk = off_bq + BQ_W
    off_bv = off_bk + BKV_W
    grp = GQA_HEADS // GQA_KV_HEADS
    bq = w_in[:, off_bq:off_bk].reshape(d, GQA_KV_HEADS, grp, HEAD_DIM).swapaxes(1, 2).reshape(d, BQ_W)
    return jnp.concatenate([w_in[:, 0:A_W], bq, w_in[:, off_ak:off_av], w_in[:, off_bk:off_bv],
                            w_in[:, off_av:off_bq], w_in[:, off_bv:off_bv + BKV_W]], axis=1)


def _permute_w_out(w_out):
    d = w_out.shape[1]
    grp = GQA_HEADS // GQA_KV_HEADS
    wb = w_out[A_W:].reshape(GQA_KV_HEADS, grp, HEAD_DIM, d).swapaxes(0, 1).reshape(BQ_W, d)
    return jnp.concatenate([w_out[:A_W], wb], axis=0)


def kernel(x, c, ctx, c_ctx, l0_ada_w, l0_ada_b, l0_norm_mix, l0_w_in, l0_lambda_q1, l0_lambda_k1, l0_lambda_q2, l0_lambda_k2, l0_subln, l0_q_norm, l0_k_norm, l0_w_out, l0_norm_ffn, l0_w_gate_up, l0_w_down, l1_ada_w, l1_ada_b, l1_norm_mix, l1_w_out, l1_norm_ffn, l1_w_gate_up, l1_w_down, final_norm):
    b, s, d = x.shape
    n_ctx = ctx.shape[1]
    assert d == D_MODEL and s % GRID_W == 0

    tt = min(256, n_ctx, s)
    tq = min(128, tt)
    nsub = max(n for n in (1, 2, 4) if s % (n * tq) == 0)
    tk = min(256, n_ctx)
    tm = min(512, s)
    tmr = min(512, s // RADIX)
    ta = min(128, s // RADIX ** 2)
    assert n_ctx % tt == 0 and s % tt == 0 and tt % tq == 0 and (n_ctx + s) % tk == 0
    assert s % tm == 0 and (s // RADIX) % tmr == 0 and (s // RADIX ** 2) % ta == 0

    n_rows = -(-(b + 1) // SUBLANES) * SUBLANES
    cond = jnp.concatenate([c, c_ctx[None, :], jnp.zeros((n_rows - b - 1, d), F32)], axis=0)
    m0, m1, wcs = _ada_fold(cond, l0_ada_w, l0_ada_b, l1_ada_w, l1_ada_b, l1_w_out)
    mod0 = m0[:b].reshape(b, N_MOD, d)
    mod0c = m0[b:b + 1].reshape(1, N_MOD, d)
    mod1 = m1[:b].reshape(b, N_MOD, d)

    w_perm = _permute_w_in(l0_w_in)
    cos_t, s1_t, s2_t = _rope_lane_tables(n_ctx, s)
    qg = jnp.tile(l0_q_norm, LANES // HEAD_DIM).reshape(1, LANES)
    kg = jnp.tile(l0_k_norm, LANES // HEAD_DIM).reshape(1, LANES)
    li = np.arange(LANES)
    ones_bd = jnp.asarray((li[:, None] // HEAD_DIM) == (li[None, :] // HEAD_DIM), BF16)
    qt, k, vt = _proj(x, ctx, mod0, mod0c, l0_norm_mix.reshape(1, d), w_perm.astype(BF16),
                      cos_t, s1_t, s2_t, qg, kg, ones_bd, tt, tq)
    lam_vecs = jnp.stack([l0_lambda_q1, l0_lambda_k1, l0_lambda_q2, l0_lambda_k2]).astype(F32)
    o, (w_out0, w_gu0, w_dn0, w_gu1, w_dn1) = _attn(
        qt, k, vt, lam_vecs, l0_subln.reshape(1, LANES), s, tq, tk, nsub,
        [_permute_w_out(l0_w_out), l0_w_gate_up, l0_w_down, l1_w_gate_up, l1_w_down])
    xr = _mixffn(o, x, mod0, w_out0, l0_norm_ffn.reshape(1, d), w_gu0, w_dn0, tm)
    xr = xr.reshape(b, s, d)

    u = _fproj(xr, mod1, l1_norm_mix.reshape(1, d), wcs, ta)
    yr = _fseq(u, _seq_dft_tables(s).astype(BF16), tmr)
    res4 = (b, RADIX, s // RADIX, d)
    return _ffn_final(xr.reshape(res4), yr.reshape(res4), mod1, l1_norm_ffn.reshape(1, d),
                      w_gu1, w_dn1, final_norm.reshape(1, d), tm)
```

```python
import functools
import math

import numpy as np
import jax
import jax.numpy as jnp
from jax import lax
from jax.experimental import pallas as pl
from jax.experimental.pallas import tpu as pltpu

F32 = jnp.float32
BF16 = jnp.bfloat16

LANES = 128
SUBLANES = 8
BF16_SUBLANES = 16
V7X_VMEM_BYTES = 64 * 1024 * 1024

D_MODEL = 1024
GRID_W = 64
DIFF_HEADS = 4
DIFF_HEAD_DIM = 64
GQA_HEADS = 8
GQA_KV_HEADS = 2
GQA_HEAD_DIM = 64
ROPE_THETA = 10000.0
ROPE_PAIRS = GQA_HEAD_DIM // 4
FOURIER_GROUPS = 8
FOURIER_GROUP_DIM = D_MODEL // FOURIER_GROUPS
EPS = 1e-6
N_MOD = 6
HEAD_DIM = 64
A_W = DIFF_HEADS * 2 * DIFF_HEAD_DIM
BQ_W = GQA_HEADS * GQA_HEAD_DIM
BKV_W = GQA_KV_HEADS * GQA_HEAD_DIM
Q_W = A_W + BQ_W
K_W = A_W + BKV_W
N_QG = Q_W // LANES
N_KG = K_W // LANES
LAM_INIT_L0 = 0.8 - 0.6 * math.exp(-0.3 * 0)
Q_SCALE = HEAD_DIM ** -0.5 * math.log2(math.e)
RADIX = 4
ACC_ROWS = 32
FFN_SPLIT = 4
VT_ROWS = LANES + BF16_SUBLANES


def _vmem_limit(nbytes):
    return int(min(nbytes, V7X_VMEM_BYTES - 4 * 1024 * 1024))


def _rms(x, g):
    ms = jnp.mean(x * x, axis=-1, keepdims=True)
    return x * lax.rsqrt(ms + EPS) * g


def _const_spec(shape):
    nd = len(shape)
    return pl.BlockSpec(shape, lambda *_: (0,) * nd, pipeline_mode=pl.Buffered(1))


def _fold_kernel(cd_ref, sd_ref, w_ref, o_ref):
    w = w_ref[...]
    d = w.shape[1]
    w_hi = w.astype(BF16)
    w_lo = (w - w_hi.astype(F32)).astype(BF16)

    def dot3(m):
        m_hi = m.astype(BF16)
        m_lo = (m - m_hi.astype(F32)).astype(BF16)
        return (jnp.dot(m_hi, w_hi, preferred_element_type=F32) + jnp.dot(m_hi, w_lo, preferred_element_type=F32)
                + jnp.dot(m_lo, w_hi, preferred_element_type=F32))

    o_ref[:, 0:d] = dot3(cd_ref[...]).astype(BF16)
    o_ref[:, d:2 * d] = dot3(sd_ref[...]).astype(BF16)


def _ada_kernel(c_ref, w0_ref, b0_ref, w1_ref, b1_ref, cd_ref, sd_ref, wo_ref, o0_ref, o1_ref, wcs_ref):
    cv = c_ref[...]
    a = (cv * jax.nn.sigmoid(cv)).astype(BF16)
    o0_ref[...] = jnp.dot(a, w0_ref[...].astype(BF16), preferred_element_type=F32) + b0_ref[...]
    o1_ref[...] = jnp.dot(a, w1_ref[...].astype(BF16), preferred_element_type=F32) + b1_ref[...]
    _fold_kernel(cd_ref, sd_ref, wo_ref, wcs_ref)


def _ada_fold(cond, w0, b0, w1, b1, w_out):
    r, d = cond.shape
    n = w0.shape[1]
    bn = 512
    gd = FOURIER_GROUP_DIM
    ngr = w_out.shape[0] // gd
    assert n // bn >= ngr
    idx = np.arange(gd)
    ang = 2.0 * np.pi * ((idx[:, None] * idx[None, :]) % gd) / gd
    cd = jnp.asarray(np.cos(ang) / np.sqrt(gd), F32)
    sd = jnp.asarray(np.sin(ang) / np.sqrt(gd), F32)
    grp = lambda j: (jnp.minimum(j, ngr - 1), 0)
    return pl.pallas_call(
        _ada_kernel,
        grid=(n // bn,),
        in_specs=[pl.BlockSpec((r, d), lambda j: (0, 0)),
                  pl.BlockSpec((d, bn), lambda j: (0, j)),
                  pl.BlockSpec((1, bn), lambda j: (0, j)),
                  pl.BlockSpec((d, bn), lambda j: (0, j)),
                  pl.BlockSpec((1, bn), lambda j: (0, j)),
                  pl.BlockSpec((gd, gd), lambda j: (0, 0)),
                  pl.BlockSpec((gd, gd), lambda j: (0, 0)),
                  pl.BlockSpec((gd, w_out.shape[1]), grp)],
        out_specs=[pl.BlockSpec((r, bn), lambda j: (0, j)),
                   pl.BlockSpec((r, bn), lambda j: (0, j)),
                   pl.BlockSpec((gd, 2 * w_out.shape[1]), grp)],
        out_shape=[jax.ShapeDtypeStruct((r, n), F32), jax.ShapeDtypeStruct((r, n), F32),
                   jax.ShapeDtypeStruct((w_out.shape[0], 2 * w_out.shape[1]), BF16)],
        compiler_params=pltpu.CompilerParams(dimension_semantics=("arbitrary",)),
        name="ada",
    )(cond, w0, b0.reshape(1, n), w1, b1.reshape(1, n), cd, sd, w_out)


def _proj_kernel(nlat, nct, tq, ng, *refs):
    x_refs, ctx_refs = refs[:ng], refs[ng:2 * ng]
    (mod_ref, modc_ref, g_ref, w_ref, cos_ref, s1_ref, s2_ref,
     qg_ref, kg_ref, ones_ref, qt_ref, k_ref, vt_ref) = refs[2 * ng:]
    j = pl.program_id(1)
    gain_x = g_ref[...] * (1.0 + mod_ref[0, 1:2, :])
    gain_c = g_ref[...] * (1.0 + modc_ref[0, 1:2, :])
    hs = []
    for t in range(ng):
        if ng - 1 - t >= nct:
            xin, gain, shift = x_refs[t][0], gain_x, mod_ref[0, 0:1, :]
        else:
            is_ctx = j * ng + t >= nlat
            xin = jnp.where(is_ctx, ctx_refs[t][0], x_refs[t][0])
            gain = jnp.where(is_ctx, gain_c, gain_x)
            shift = jnp.where(is_ctx, modc_ref[0, 0:1, :], mod_ref[0, 0:1, :])
        ms = jnp.mean(xin * xin, axis=-1, keepdims=True)
        hs.append((xin * lax.rsqrt(ms + EPS) * gain + shift).astype(BF16))
    h = jnp.concatenate(hs, axis=0)

    tt = h.shape[0]
    cosv = cos_ref[...]
    s1 = s1_ref[...]
    s2 = s2_ref[...]
    row = lax.broadcasted_iota(jnp.int32, (LANES, tt), 0)
    lo_rows = row < HEAD_DIM

    def rope(xs):
        return xs * cosv + pltpu.roll(xs, LANES - ROPE_PAIRS, 1) * s1 + pltpu.roll(xs, ROPE_PAIRS, 1) * s2

    def headnorm(xs, gain):
        sq = xs * xs
        hi = sq.astype(BF16)
        lo = (sq - hi.astype(F32)).astype(BF16)
        ss = (jnp.dot(hi, ones_ref[...], preferred_element_type=F32)
              + jnp.dot(lo, ones_ref[...], preferred_element_type=F32))
        return xs * lax.rsqrt(ss * (1.0 / HEAD_DIM) + EPS) * gain

    def finish_slab(idx, xs):
        if idx < N_QG:
            s = idx
            if s >= A_W // LANES:
                xs = headnorm(xs, qg_ref[...])
            qst = (rope(xs) * Q_SCALE).T
            q_lo = jnp.where(lo_rows, qst, 0.0)
            q_hi = qst - q_lo
            for a in range(tt // tq):
                qt_ref[0, s, :, 2 * a * tq:(2 * a + 1) * tq] = q_lo[:, a * tq:(a + 1) * tq].astype(BF16)
                qt_ref[0, s, :, (2 * a + 1) * tq:(2 * a + 2) * tq] = q_hi[:, a * tq:(a + 1) * tq].astype(BF16)
        elif idx < N_QG + N_KG:
            s = idx - N_QG
            if s >= A_W // LANES:
                xs = headnorm(xs, kg_ref[...])
            k_ref[0, s] = rope(xs).astype(BF16)
        else:
            s = idx - N_QG - N_KG
            vt_ref[0, s, 0:LANES, :] = xs.T.astype(BF16)
            vt_ref[0, s, LANES:VT_ROWS, :] = jnp.ones((VT_ROWS - LANES, tt), BF16)

    gw = 2 * LANES
    n_groups = w_ref.shape[1] // gw
    pending = None
    for gi in range(n_groups + 1):
        z = None
        if gi < n_groups:
            z = jnp.dot(h, w_ref[:, gi * gw:(gi + 1) * gw], preferred_element_type=F32)
        if pending is not None:
            pg, pz = pending
            for half in range(gw // LANES):
                finish_slab(pg * (gw // LANES) + half, pz[:, half * LANES:(half + 1) * LANES])
        pending = (gi, z)


def _proj(x, ctx, mod, modc, g, w_perm, cos_t, s1_t, s2_t, qg, kg, ones_bd, tt, tq):
    b, s, d = x.shape
    nctx = ctx.shape[1]
    nct = nctx // tt
    nk = nctx + s
    nj = nk // tt
    ng = max(gsz for gsz in (1, 2, 3) if nj % gsz == 0)
    rows = ng * tt
    n_in = w_perm.shape[1]
    nlat = nj - nct
    kern = functools.partial(_proj_kernel, nlat, nct, tq, ng)

    def x_spec(t):
        return pl.BlockSpec((1, tt, d), lambda bi, j: (bi, jnp.minimum(j * ng + t, nlat - 1), 0))

    def ctx_spec(t):
        return pl.BlockSpec((1, tt, d), lambda bi, j: (bi, jnp.maximum(j * ng + t - nlat, 0), 0))

    tab_spec = pl.BlockSpec((rows, LANES), lambda bi, j: (j, 0))
    return pl.pallas_call(
        kern,
        grid=(b, nj // ng),
        in_specs=[x_spec(t) for t in range(ng)] + [ctx_spec(t) for t in range(ng)]
        + [pl.BlockSpec((1, N_MOD, d), lambda bi, j: (bi, 0, 0)),
           _const_spec((1, N_MOD, d)),
           _const_spec((1, d)),
           _const_spec((d, n_in)),
           tab_spec, tab_spec, tab_spec,
           _const_spec((1, LANES)),
           _const_spec((1, LANES)),
           _const_spec((LANES, LANES))],
        out_specs=[pl.BlockSpec((1, N_QG, LANES, 2 * rows), lambda bi, j: (bi, 0, 0, j)),
                   pl.BlockSpec((1, N_KG, rows, LANES), lambda bi, j: (bi, 0, j, 0)),
                   pl.BlockSpec((1, N_KG, VT_ROWS, rows), lambda bi, j: (bi, 0, 0, j))],
        out_shape=[jax.ShapeDtypeStruct((b, N_QG, LANES, 2 * nk), BF16),
                   jax.ShapeDtypeStruct((b, N_KG, nk, LANES), BF16),
                   jax.ShapeDtypeStruct((b, N_KG, VT_ROWS, nk), BF16)],
        compiler_params=pltpu.CompilerParams(
            dimension_semantics=("parallel", "arbitrary"),
            vmem_limit_bytes=_vmem_limit(56 * 1024 * 1024)),
        name="proj",
    )(*([x] * ng), *([ctx] * ng), mod, modc, g, w_perm, cos_t, s1_t, s2_t, qg, kg, ones_bd)


def _attn_kernel(tq, tk, nsub, ncast, qt_ref, k_ref, vt_ref, lam_ref, subln_ref, *rest):
    w_refs, o_ref, wb_refs, bufs = rest[:ncast], rest[ncast], rest[ncast + 1:2 * ncast + 1], rest[2 * ncast + 1:]
    for w_ref, wb_ref in zip(w_refs, wb_refs):
        wb_ref[...] = w_ref[...].astype(BF16)
    nk = k_ref.shape[2]
    nc = nk // tk
    tq2 = 2 * tq
    row = lax.broadcasted_iota(jnp.int32, (LANES, tq), 0)
    lo_rows = row < HEAD_DIM
    lv = lam_ref[...]
    lam = (jnp.exp(jnp.sum(lv[0:1] * lv[1:2], axis=1, keepdims=True))
           - jnp.exp(jnp.sum(lv[2:3] * lv[3:4], axis=1, keepdims=True)) + LAM_INIT_L0)

    n_a = A_W // LANES

    s_bufs, p_bufs = bufs[:3], bufs[3:]

    def item(w):
        return divmod(w, N_QG)

    def score_chunk(w, c, m8):
        qi, u = item(w)
        kv = min(u, n_a)
        s_c = jnp.dot(k_ref[0, kv, c * tk:(c + 1) * tk, :], qt_ref[0, u, :, qi * tq2:(qi + 1) * tq2],
                      preferred_element_type=F32)
        s_bufs[w % 3][c * tk:(c + 1) * tk, :] = s_c
        mc = jnp.max(s_c.reshape(tk // ACC_ROWS, ACC_ROWS, tq2), axis=0)
        return mc if m8 is None else jnp.maximum(m8, mc)

    def exp_chunk(w, c, m):
        p_bufs[w % 2][c * tk:(c + 1) * tk, :] = jnp.exp2((s_bufs[w % 3][c * tk:(c + 1) * tk, :] - m).astype(BF16))

    def value_chunk(w, c, acc):
        kv = min(item(w)[1], n_a)
        pv = jnp.dot(vt_ref[0, kv, :, c * tk:(c + 1) * tk], p_bufs[w % 2][c * tk:(c + 1) * tk, :],
                     preferred_element_type=F32)
        return pv if acc is None else acc + pv

    def finish(w, acc):
        qi, u = item(w)
        ot = acc[0:LANES] * (1.0 / acc[LANES:LANES + 1])
        o1, o2 = ot[:, :tq], ot[:, tq:]
        if u < n_a:
            ot = o1 - lam * o2
            ot = ot * lax.rsqrt(jnp.mean(ot * ot, axis=0, keepdims=True) + EPS)
            o_ref[0, u, qi * tq:(qi + 1) * tq, :] = (ot.T * (subln_ref[...] * (1.0 - LAM_INIT_L0))).astype(BF16)
        else:
            o_ref[0, u, qi * tq:(qi + 1) * tq, :] = jnp.where(lo_rows, o1, o2).T.astype(BF16)

    n_items = nsub * N_QG
    m8s = {}
    ms = {}
    accs = {}
    for t in range(n_items + 2):
        if 0 <= t - 1 < n_items:
            ms[t - 1] = jnp.max(m8s.pop(t - 1), axis=0, keepdims=True)
        for c in range(nc):
            pace = m8s.get(t) if t < n_items else accs.get(t - 2)
            pace = None if pace is None else pace[0:1]
            if 0 <= t - 2 < n_items:
                accs[t - 2] = value_chunk(t - 2, c, accs.get(t - 2))
            if t < n_items:
                m8s[t] = score_chunk(t, c, m8s.get(t))
            if 0 <= t - 1 < n_items:
                m = ms[t - 1]
                if pace is not None:
                    m = m + jnp.minimum(jnp.abs(pace), 0.0)
                exp_chunk(t - 1, c, m)
        if 0 <= t - 2 < n_items:
            finish(t - 2, accs.pop(t - 2))


def _cast_rows(n_rows, n_steps):
    best = None
    for r in range(BF16_SUBLANES, n_rows + 1, BF16_SUBLANES):
        if n_rows % r == 0 and n_steps % (n_rows // r) == 0:
            best = r if best is None else min(best, r)
    return best


def _attn(qt, k, vt, lam_vecs, subln, s, tq, tk, nsub, weights):
    b = qt.shape[0]
    nk = k.shape[2]
    tqs = tq * nsub
    ni = s // tqs
    plans = [_cast_rows(w.shape[0], b * ni) for w in weights]
    riders = [(w, r) for w, r in zip(weights, plans) if r is not None]
    kern = functools.partial(_attn_kernel, tq, tk, nsub, len(riders))

    def w_spec(w, r):
        rep = (b * ni) // (w.shape[0] // r)
        return pl.BlockSpec((r, w.shape[1]), lambda bi, i: ((bi * ni + i) // rep, 0))

    outs = pl.pallas_call(
        kern,
        grid=(b, ni),
        in_specs=[pl.BlockSpec((1, N_QG, LANES, 2 * tqs), lambda bi, i: (bi, 0, 0, i)),
                  pl.BlockSpec((1, N_KG, nk, LANES), lambda bi, i: (bi, 0, 0, 0)),
                  pl.BlockSpec((1, N_KG, VT_ROWS, nk), lambda bi, i: (bi, 0, 0, 0)),
                  _const_spec((4, HEAD_DIM)),
                  _const_spec((1, LANES))] + [w_spec(w, r) for w, r in riders],
        out_specs=[pl.BlockSpec((1, N_QG, tqs, LANES), lambda bi, i: (bi, 0, i, 0))]
        + [w_spec(w, r) for w, r in riders],
        out_shape=[jax.ShapeDtypeStruct((b, N_QG, s, LANES), BF16)]
        + [jax.ShapeDtypeStruct(w.shape, BF16) for w, _ in riders],
        scratch_shapes=[pltpu.VMEM((nk, 2 * tq), F32)] * 3 + [pltpu.VMEM((nk, 2 * tq), BF16)] * 2,
        compiler_params=pltpu.CompilerParams(
            dimension_semantics=("arbitrary", "arbitrary"),
            vmem_limit_bytes=_vmem_limit(48 * 1024 * 1024)),
        name="attn",
    )(qt, k, vt, lam_vecs, subln, *[w for w, _ in riders])
    cast = iter(outs[1:])
    return outs[0], [next(cast) if r is not None else w.astype(BF16) for w, r in zip(weights, plans)]


def _swiglu(x_all, mod_ref, gffn_ref, wgu_ref, wd_ref):
    shift = mod_ref[0, 3:4, :]
    gain = gffn_ref[...] * (1.0 + mod_ref[0, 4:5, :])
    gate = mod_ref[0, 5:6, :]
    n = x_all.shape[0]
    nh = n // FFN_SPLIT
    outs = []
    for r0 in range(0, n, nh):
        x1 = x_all[r0:r0 + nh]
        h = _rms(x1, gain) + shift
        gu = jnp.dot(h.astype(BF16), wgu_ref[...], preferred_element_type=F32)
        dff = gu.shape[1] // 2
        g = gu[:, :dff]
        u = gu[:, dff:]
        a = (g * jax.nn.sigmoid(g)) * u
        dn = jnp.dot(a.astype(BF16), wd_ref[...], preferred_element_type=F32)
        outs.append(x1 + gate * dn)
    return outs[0] if len(outs) == 1 else jnp.concatenate(outs, axis=0)


def _mixffn_kernel(o_ref, x_ref, mod_ref, wout_ref, gffn_ref, wgu_ref, wd_ref, out_ref, res_scr):
    o = jnp.concatenate([o_ref[0, u] for u in range(N_QG)], axis=1)
    y = jnp.dot(o, wout_ref[...], preferred_element_type=F32)
    x1 = x_ref[0] + mod_ref[0, 2:3, :] * y
    res = _swiglu(x1, mod_ref, gffn_ref, wgu_ref, wd_ref)
    n = res_scr.shape[1] // RADIX
    for sl in range(res_scr.shape[0]):
        res_scr[sl] = res[:, sl * LANES:(sl + 1) * LANES]
        for r in range(RADIX):
            out_ref[0, r, :, sl * LANES:(sl + 1) * LANES] = res_scr[sl, pl.ds(r, n, stride=RADIX), :]


def _ffn_final_kernel(x_ref, y_ref, mod_ref, gffn_ref, wgu_ref, wd_ref, gfin_ref, out_ref, x_scr):
    n = x_scr.shape[1] // RADIX
    gate = mod_ref[0, 2:3, :]
    for sl in range(x_scr.shape[0]):
        cols = slice(sl * LANES, (sl + 1) * LANES)
        for r in range(RADIX):
            x_scr[sl, pl.ds(r, n, stride=RADIX), :] = (x_ref[0, r, :, cols]
                                                       + gate[:, cols] * y_ref[0, r, :, cols].astype(F32))
    x1 = jnp.concatenate([x_scr[sl] for sl in range(x_scr.shape[0])], axis=1)
    x2 = _swiglu(x1, mod_ref, gffn_ref, wgu_ref, wd_ref)
    out_ref[0] = _rms(x2, gfin_ref[...])


def _mixffn(o, x, mod, wout, gffn, wgu, wd, tm):
    b, s, d = x.shape
    tok = lambda bi, i: (bi, i, 0)
    return pl.pallas_call(
        _mixffn_kernel,
        grid=(b, s // tm),
        in_specs=[pl.BlockSpec((1, N_QG, tm, LANES), lambda bi, i: (bi, 0, i, 0)),
                  pl.BlockSpec((1, tm, d), tok),
                  pl.BlockSpec((1, N_MOD, d), lambda bi, i: (bi, 0, 0)),
                  _const_spec(wout.shape), _const_spec((1, d)), _const_spec(wgu.shape), _const_spec(wd.shape)],
        out_specs=pl.BlockSpec((1, RADIX, tm // RADIX, d), lambda bi, i: (bi, 0, i, 0)),
        out_shape=jax.ShapeDtypeStruct((b, RADIX, s // RADIX, d), F32),
        scratch_shapes=[pltpu.VMEM((d // LANES, tm, LANES), F32)],
        compiler_params=pltpu.CompilerParams(
            dimension_semantics=("parallel", "arbitrary"),
            vmem_limit_bytes=_vmem_limit(56 * 1024 * 1024)),
        name="mixffn",
    )(o, x, mod, wout, gffn, wgu, wd)


def _ffn_final(xr, yr, mod, gffn, wgu, wd, gfin, tm):
    b, _, nq, d = xr.shape
    s = nq * RADIX
    res = pl.BlockSpec((1, RADIX, tm // RADIX, d), lambda bi, i: (bi, 0, i, 0))
    return pl.pallas_call(
        _ffn_final_kernel,
        grid=(b, s // tm),
        in_specs=[res, res,
                  pl.BlockSpec((1, N_MOD, d), lambda bi, i: (bi, 0, 0)),
                  _const_spec((1, d)), _const_spec(wgu.shape), _const_spec(wd.shape), _const_spec((1, d))],
        out_specs=pl.BlockSpec((1, tm, d), lambda bi, i: (bi, i, 0)),
        out_shape=jax.ShapeDtypeStruct((b, s, d), F32),
        scratch_shapes=[pltpu.VMEM((d // LANES, tm, LANES), F32)],
        compiler_params=pltpu.CompilerParams(
            dimension_semantics=("parallel", "arbitrary"),
            vmem_limit_bytes=_vmem_limit(56 * 1024 * 1024)),
        name="ffn",
    )(xr, yr, mod, gffn, wgu, wd, gfin)


_QUARTER = ((1, 0), (0, 1), (-1, 0), (0, -1))


def _fproj_kernel(ng, *refs):
    x_refs = refs[:RADIX * ng]
    mod_ref, g_ref, w_ref, u_ref = refs[RADIX * ng:]
    d = x_refs[0].shape[2]
    ta = x_refs[0].shape[1]
    shift = mod_ref[0, 0:1, :]
    gain = g_ref[...] * (1.0 + mod_ref[0, 1:2, :])
    for t in range(ng):
        h = jnp.concatenate([(_rms(xr[0], gain) + shift).astype(BF16)
                             for xr in x_refs[RADIX * t:RADIX * (t + 1)]], axis=0)
        z = jnp.dot(h, w_ref[...], preferred_element_type=F32)
        zc = [z[m * ta:(m + 1) * ta, 0:d] for m in range(RADIX)]
        zs = [z[m * ta:(m + 1) * ta, d:2 * d] for m in range(RADIX)]
        for r in range(RADIX):
            ua = None
            ub = None
            for m in range(RADIX):
                cp, sp = _QUARTER[(r * m) % 4]
                ta_, sa = (zc[m], cp) if cp else (zs[m], -sp)
                tb_, sb = (zs[m], cp) if cp else (zc[m], sp)
                ua = sa * ta_ if ua is None else (ua + ta_ if sa > 0 else ua - ta_)
                ub = sb * tb_ if ub is None else (ub + tb_ if sb > 0 else ub - tb_)
            u_ref[0, r, 0, t * ta:(t + 1) * ta, :] = ua.astype(BF16)
            u_ref[0, r, 1, t * ta:(t + 1) * ta, :] = ub.astype(BF16)


def _fproj(x, mod, g, wcs, ta):
    b, s, d = x.shape
    nq = s // RADIX
    nqq = nq // RADIX
    na = nqq // ta
    nrb = RADIX * na
    ng = 2

    def x_spec(t, m):
        def index(bi, j):
            rb = j * ng + t
            return (bi, (rb // na) * (RADIX * na) + m * na + rb % na, 0)
        return pl.BlockSpec((1, ta, d), index)

    return pl.pallas_call(
        functools.partial(_fproj_kernel, ng),
        grid=(b, nrb // ng),
        in_specs=[x_spec(t, m) for t in range(ng) for m in range(RADIX)]
        + [pl.BlockSpec((1, N_MOD, d), lambda bi, j: (bi, 0, 0)),
           _const_spec((1, d)), _const_spec(wcs.shape)],
        out_specs=pl.BlockSpec((1, RADIX, 2, ng * ta, d), lambda bi, j: (bi, 0, 0, j, 0)),
        out_shape=jax.ShapeDtypeStruct((b, RADIX, 2, nq, d), BF16),
        compiler_params=pltpu.CompilerParams(
            dimension_semantics=("parallel", "arbitrary"),
            vmem_limit_bytes=_vmem_limit(48 * 1024 * 1024)),
        name="fproj",
    )(*([x] * (RADIX * ng)), mod, g, wcs)


def _fseq_kernel(u_ref, t_ref, y_ref):
    nq, d = u_ref.shape[3], u_ref.shape[4]
    u = u_ref[0, 0].reshape(2 * nq, d)
    y_ref[0] = jnp.dot(t_ref[0], u, preferred_element_type=F32).astype(BF16)


def _seq_dft_tables(n):
    nq = n // RADIX
    nqq = nq // RADIX
    u = np.arange(nq)
    q = (RADIX * (u % nqq) + u // nqq)[None, None, :]
    p = np.arange(nq)[None, :, None]
    r = np.arange(RADIX)[:, None, None]
    ang = 2.0 * np.pi * (((RADIX * p + r) * q) % n) / n
    t = np.concatenate([np.cos(ang), -np.sin(ang)], axis=2) / np.sqrt(n)
    return jnp.asarray(t, F32)


def _fseq(u, t, tmr):
    b, _, _, nq, d = u.shape
    nblk = nq // tmr
    return pl.pallas_call(
        _fseq_kernel,
        grid=(b, RADIX, nblk),
        in_specs=[pl.BlockSpec((1, 1, 2, nq, d), lambda bi, r, i: (bi, r, 0, 0, 0)),
                  pl.BlockSpec((1, tmr, 2 * nq), lambda bi, r, i: (r, i, 0))],
        out_specs=pl.BlockSpec((1, tmr, d), lambda bi, r, i: (bi, r * nblk + i, 0)),
        out_shape=jax.ShapeDtypeStruct((b, RADIX * nq, d), BF16),
        compiler_params=pltpu.CompilerParams(
            dimension_semantics=("parallel", "arbitrary", "arbitrary"),
            vmem_limit_bytes=_vmem_limit(40 * 1024 * 1024)),
        name="fseq",
    )(u, t)


def _rope_lane_tables(n_ctx, n_tok):
    rows_count = n_tok // GRID_W
    row = np.repeat(np.arange(rows_count), GRID_W).astype(np.float32)
    col = np.tile(np.arange(GRID_W), rows_count).astype(np.float32)
    inv = np.float32(ROPE_THETA) ** (-np.arange(ROPE_PAIRS, dtype=np.float32) / np.float32(ROPE_PAIRS))
    lane = np.arange(LANES)
    dd = lane % HEAD_DIM
    axis = dd // (2 * ROPE_PAIRS)
    half = (dd % (2 * ROPE_PAIRS)) // ROPE_PAIRS
    pair = dd % ROPE_PAIRS
    pos = np.where((axis == 0)[None, :], row[:, None], col[:, None])
    ang = (pos * inv.astype(np.float32)[pair][None, :]).astype(np.float32).astype(np.float64)
    cosv = np.cos(ang)
    sinv = np.sin(ang)
    first = (half == 0)[None, :]
    s1 = np.where(first, -sinv, 0.0)
    s2 = np.where(first, 0.0, sinv)
    pad1 = np.ones((n_ctx, LANES))
    pad0 = np.zeros((n_ctx, LANES))
    return tuple(jnp.asarray(np.concatenate([tab, pad], 0), F32)
                 for pad, tab in ((pad1, cosv), (pad0, s1), (pad0, s2)))


def _permute_w_in(w_in):
    d = w_in.shape[0]
    off_ak = A_W
    off_av = 2 * A_W
    off_bq = 3 * A_W
    off_bk = off_bq + BQ_W
    off_bv = off_bk + BKV_W
    grp = GQA_HEADS // GQA_KV_HEADS
    bq = w_in[:, off_bq:off_bk].reshape(d, GQA_KV_HEADS, grp, HEAD_DIM).swapaxes(1, 2).reshape(d, BQ_W)
    return jnp.concatenate([w_in[:, 0:A_W], bq, w_in[:, off_ak:off_av], w_in[:, off_bk:off_bv],
                            w_in[:, off_av:off_bq], w_in[:, off_bv:off_bv + BKV_W]], axis=1)


def _permute_w_out(w_out):
    d = w_out.shape[1]
    grp = GQA_HEADS // GQA_KV_HEADS
    wb = w_out[A_W:].reshape(GQA_KV_HEADS, grp, HEAD_DIM, d).swapaxes(0, 1).reshape(BQ_W, d)
    return jnp.concatenate([w_out[:A_W], wb], axis=0)


def kernel(x, c, ctx, c_ctx, l0_ada_w, l0_ada_b, l0_norm_mix, l0_w_in, l0_lambda_q1, l0_lambda_k1, l0_lambda_q2, l0_lambda_k2, l0_subln, l0_q_norm, l0_k_norm, l0_w_out, l0_norm_ffn, l0_w_gate_up, l0_w_down, l1_ada_w, l1_ada_b, l1_norm_mix, l1_w_out, l1_norm_ffn, l1_w_gate_up, l1_w_down, final_norm):
    b, s, d = x.shape
    n_ctx = ctx.shape[1]
    assert d == D_MODEL and s % GRID_W == 0

    tt = min(256, n_ctx, s)
    tq = min(128, tt)
    nsub = max(n for n in (1, 2, 4) if s % (n * tq) == 0)
    tk = min(256, n_ctx)
    tm = min(1024, s)
    tmr = min(512, s // RADIX)
    ta = min(128, s // RADIX ** 2)
    assert n_ctx % tt == 0 and s % tt == 0 and tt % tq == 0 and (n_ctx + s) % tk == 0
    assert s % tm == 0 and (s // RADIX) % tmr == 0 and (s // RADIX ** 2) % ta == 0

    n_rows = -(-(b + 1) // SUBLANES) * SUBLANES
    cond = jnp.concatenate([c, c_ctx[None, :], jnp.zeros((n_rows - b - 1, d), F32)], axis=0)
    m0, m1, wcs = _ada_fold(cond, l0_ada_w, l0_ada_b, l1_ada_w, l1_ada_b, l1_w_out)
    mod0 = m0[:b].reshape(b, N_MOD, d)
    mod0c = m0[b:b + 1].reshape(1, N_MOD, d)
    mod1 = m1[:b].reshape(b, N_MOD, d)

    w_perm = _permute_w_in(l0_w_in)
    cos_t, s1_t, s2_t = _rope_lane_tables(n_ctx, s)
    qg = jnp.tile(l0_q_norm, LANES // HEAD_DIM).reshape(1, LANES)
    kg = jnp.tile(l0_k_norm, LANES // HEAD_DIM).reshape(1, LANES)
    li = np.arange(LANES)
    ones_bd = jnp.asarray((li[:, None] // HEAD_DIM) == (li[None, :] // HEAD_DIM), BF16)
    qt, k, vt = _proj(x, ctx, mod0, mod0c, l0_norm_mix.reshape(1, d), w_perm.astype(BF16),
                      cos_t, s1_t, s2_t, qg, kg, ones_bd, tt, tq)
    lam_vecs = jnp.stack([l0_lambda_q1, l0_lambda_k1, l0_lambda_q2, l0_lambda_k2]).astype(F32)
    o, (w_out0, w_gu0, w_dn0, w_gu1, w_dn1) = _attn(
        qt, k, vt, lam_vecs, l0_subln.reshape(1, LANES), s, tq, tk, nsub,
        [_permute_w_out(l0_w_out), l0_w_gate_up, l0_w_down, l1_w_gate_up, l1_w_down])
    xr = _mixffn(o, x, mod0, w_out0, l0_norm_ffn.reshape(1, d), w_gu0, w_dn0, tm)
    xr = xr.reshape(b, s, d)

    u = _fproj(xr, mod1, l1_norm_mix.reshape(1, d), wcs, ta)
    yr = _fseq(u, _seq_dft_tables(s).astype(BF16), tmr)
    res4 = (b, RADIX, s // RADIX, d)
    return _ffn_final(xr.reshape(res4), yr.reshape(res4), mod1, l1_norm_ffn.reshape(1, d),
                      w_gu1, w_dn1, final_norm.reshape(1, d), tm)
```

```python
import functools
import math

import numpy as np
import jax
import jax.numpy as jnp
from jax import lax
from jax.experimental import pallas as pl
from jax.experimental.pallas import tpu as pltpu

F32 = jnp.float32
BF16 = jnp.bfloat16

LANES = 128
SUBLANES = 8
BF16_SUBLANES = 16
V7X_VMEM_BYTES = 64 * 1024 * 1024

D_MODEL = 1024
GRID_W = 64
DIFF_HEADS = 4
DIFF_HEAD_DIM = 64
GQA_HEADS = 8
GQA_KV_HEADS = 2
GQA_HEAD_DIM = 64
ROPE_THETA = 10000.0
ROPE_PAIRS = GQA_HEAD_DIM // 4
FOURIER_GROUPS = 8
FOURIER_GROUP_DIM = D_MODEL // FOURIER_GROUPS
EPS = 1e-6
N_MOD = 6
HEAD_DIM = 64
A_W = DIFF_HEADS * 2 * DIFF_HEAD_DIM
BQ_W = GQA_HEADS * GQA_HEAD_DIM
BKV_W = GQA_KV_HEADS * GQA_HEAD_DIM
Q_W = A_W + BQ_W
K_W = A_W + BKV_W
N_QG = Q_W // LANES
N_KG = K_W // LANES
LAM_INIT_L0 = 0.8 - 0.6 * math.exp(-0.3 * 0)
Q_SCALE = HEAD_DIM ** -0.5 * math.log2(math.e)
RADIX = 4
ACC_ROWS = 32
FFN_SPLIT = 4


def _vmem_limit(nbytes):
    return int(min(nbytes, V7X_VMEM_BYTES - 4 * 1024 * 1024))


def _rms(x, g):
    ms = jnp.mean(x * x, axis=-1, keepdims=True)
    return x * lax.rsqrt(ms + EPS) * g


def _const_spec(shape):
    nd = len(shape)
    return pl.BlockSpec(shape, lambda *_: (0,) * nd, pipeline_mode=pl.Buffered(1))


def _fold_kernel(cd_ref, sd_ref, w_ref, o_ref):
    w = w_ref[...]
    d = w.shape[1]
    w_hi = w.astype(BF16)
    w_lo = (w - w_hi.astype(F32)).astype(BF16)

    def dot3(m):
        m_hi = m.astype(BF16)
        m_lo = (m - m_hi.astype(F32)).astype(BF16)
        return (jnp.dot(m_hi, w_hi, preferred_element_type=F32) + jnp.dot(m_hi, w_lo, preferred_element_type=F32)
                + jnp.dot(m_lo, w_hi, preferred_element_type=F32))

    o_ref[:, 0:d] = dot3(cd_ref[...]).astype(BF16)
    o_ref[:, d:2 * d] = dot3(sd_ref[...]).astype(BF16)


def _ada_kernel(c_ref, w0_ref, b0_ref, w1_ref, b1_ref, cd_ref, sd_ref, wo_ref, o0_ref, o1_ref, wcs_ref):
    cv = c_ref[...]
    a = (cv * jax.nn.sigmoid(cv)).astype(BF16)
    o0_ref[...] = jnp.dot(a, w0_ref[...].astype(BF16), preferred_element_type=F32) + b0_ref[...]
    o1_ref[...] = jnp.dot(a, w1_ref[...].astype(BF16), preferred_element_type=F32) + b1_ref[...]
    _fold_kernel(cd_ref, sd_ref, wo_ref, wcs_ref)


def _ada_fold(cond, w0, b0, w1, b1, w_out):
    r, d = cond.shape
    n = w0.shape[1]
    bn = 512
    gd = FOURIER_GROUP_DIM
    ngr = w_out.shape[0] // gd
    assert n // bn >= ngr
    idx = np.arange(gd)
    ang = 2.0 * np.pi * ((idx[:, None] * idx[None, :]) % gd) / gd
    cd = jnp.asarray(np.cos(ang) / np.sqrt(gd), F32)
    sd = jnp.asarray(np.sin(ang) / np.sqrt(gd), F32)
    grp = lambda j: (jnp.minimum(j, ngr - 1), 0)
    return pl.pallas_call(
        _ada_kernel,
        grid=(n // bn,),
        in_specs=[pl.BlockSpec((r, d), lambda j: (0, 0)),
                  pl.BlockSpec((d, bn), lambda j: (0, j)),
                  pl.BlockSpec((1, bn), lambda j: (0, j)),
                  pl.BlockSpec((d, bn), lambda j: (0, j)),
                  pl.BlockSpec((1, bn), lambda j: (0, j)),
                  pl.BlockSpec((gd, gd), lambda j: (0, 0)),
                  pl.BlockSpec((gd, gd), lambda j: (0, 0)),
                  pl.BlockSpec((gd, w_out.shape[1]), grp)],
        out_specs=[pl.BlockSpec((r, bn), lambda j: (0, j)),
                   pl.BlockSpec((r, bn), lambda j: (0, j)),
                   pl.BlockSpec((gd, 2 * w_out.shape[1]), grp)],
        out_shape=[jax.ShapeDtypeStruct((r, n), F32), jax.ShapeDtypeStruct((r, n), F32),
                   jax.ShapeDtypeStruct((w_out.shape[0], 2 * w_out.shape[1]), BF16)],
        compiler_params=pltpu.CompilerParams(dimension_semantics=("arbitrary",)),
        name="ada",
    )(cond, w0, b0.reshape(1, n), w1, b1.reshape(1, n), cd, sd, w_out)


def _proj_kernel(nlat, nct, tq, ng, *refs):
    x_refs, ctx_refs = refs[:ng], refs[ng:2 * ng]
    (mod_ref, modc_ref, g_ref, w_ref, cos_ref, s1_ref, s2_ref,
     qg_ref, kg_ref, ones_ref, qt_ref, k_ref, vt_ref) = refs[2 * ng:]
    j = pl.program_id(1)
    gain_x = g_ref[...] * (1.0 + mod_ref[0, 1:2, :])
    gain_c = g_ref[...] * (1.0 + modc_ref[0, 1:2, :])
    hs = []
    for t in range(ng):
        if ng - 1 - t >= nct:
            xin, gain, shift = x_refs[t][0], gain_x, mod_ref[0, 0:1, :]
        else:
            is_ctx = j * ng + t >= nlat
            xin = jnp.where(is_ctx, ctx_refs[t][0], x_refs[t][0])
            gain = jnp.where(is_ctx, gain_c, gain_x)
            shift = jnp.where(is_ctx, modc_ref[0, 0:1, :], mod_ref[0, 0:1, :])
        ms = jnp.mean(xin * xin, axis=-1, keepdims=True)
        hs.append((xin * lax.rsqrt(ms + EPS) * gain + shift).astype(BF16))
    h = jnp.concatenate(hs, axis=0)

    tt = h.shape[0]
    cosv = cos_ref[...]
    s1 = s1_ref[...]
    s2 = s2_ref[...]
    row = lax.broadcasted_iota(jnp.int32, (LANES, tt), 0)
    lo_rows = row < HEAD_DIM

    def rope(xs):
        return xs * cosv + pltpu.roll(xs, LANES - ROPE_PAIRS, 1) * s1 + pltpu.roll(xs, ROPE_PAIRS, 1) * s2

    def headnorm(xs, gain):
        sq = xs * xs
        hi = sq.astype(BF16)
        lo = (sq - hi.astype(F32)).astype(BF16)
        ss = (jnp.dot(hi, ones_ref[...], preferred_element_type=F32)
              + jnp.dot(lo, ones_ref[...], preferred_element_type=F32))
        return xs * lax.rsqrt(ss * (1.0 / HEAD_DIM) + EPS) * gain

    def finish_slab(idx, xs):
        if idx < N_QG:
            s = idx
            if s >= A_W // LANES:
                xs = headnorm(xs, qg_ref[...])
            qst = (rope(xs) * Q_SCALE).T
            q_lo = jnp.where(lo_rows, qst, 0.0)
            q_hi = qst - q_lo
            for a in range(tt // tq):
                qt_ref[0, s, :, 2 * a * tq:(2 * a + 1) * tq] = q_lo[:, a * tq:(a + 1) * tq].astype(BF16)
                qt_ref[0, s, :, (2 * a + 1) * tq:(2 * a + 2) * tq] = q_hi[:, a * tq:(a + 1) * tq].astype(BF16)
        elif idx < N_QG + N_KG:
            s = idx - N_QG
            if s >= A_W // LANES:
                xs = headnorm(xs, kg_ref[...])
            k_ref[0, s] = rope(xs).astype(BF16)
        else:
            s = idx - N_QG - N_KG
            vt_ref[0, s] = xs.T.astype(BF16)

    gw = 2 * LANES
    n_groups = w_ref.shape[1] // gw
    pending = None
    for gi in range(n_groups + 1):
        z = None
        if gi < n_groups:
            z = jnp.dot(h, w_ref[:, gi * gw:(gi + 1) * gw], preferred_element_type=F32)
        if pending is not None:
            pg, pz = pending
            for half in range(gw // LANES):
                finish_slab(pg * (gw // LANES) + half, pz[:, half * LANES:(half + 1) * LANES])
        pending = (gi, z)


def _proj(x, ctx, mod, modc, g, w_perm, cos_t, s1_t, s2_t, qg, kg, ones_bd, tt, tq):
    b, s, d = x.shape
    nctx = ctx.shape[1]
    nct = nctx // tt
    nk = nctx + s
    nj = nk // tt
    ng = max(gsz for gsz in (1, 2, 3) if nj % gsz == 0)
    rows = ng * tt
    n_in = w_perm.shape[1]
    nlat = nj - nct
    kern = functools.partial(_proj_kernel, nlat, nct, tq, ng)

    def x_spec(t):
        return pl.BlockSpec((1, tt, d), lambda bi, j: (bi, jnp.minimum(j * ng + t, nlat - 1), 0))

    def ctx_spec(t):
        return pl.BlockSpec((1, tt, d), lambda bi, j: (bi, jnp.maximum(j * ng + t - nlat, 0), 0))

    tab_spec = pl.BlockSpec((rows, LANES), lambda bi, j: (j, 0))
    return pl.pallas_call(
        kern,
        grid=(b, nj // ng),
        in_specs=[x_spec(t) for t in range(ng)] + [ctx_spec(t) for t in range(ng)]
        + [pl.BlockSpec((1, N_MOD, d), lambda bi, j: (bi, 0, 0)),
           _const_spec((1, N_MOD, d)),
           _const_spec((1, d)),
           _const_spec((d, n_in)),
           tab_spec, tab_spec, tab_spec,
           _const_spec((1, LANES)),
           _const_spec((1, LANES)),
           _const_spec((LANES, LANES))],
        out_specs=[pl.BlockSpec((1, N_QG, LANES, 2 * rows), lambda bi, j: (bi, 0, 0, j)),
                   pl.BlockSpec((1, N_KG, rows, LANES), lambda bi, j: (bi, 0, j, 0)),
                   pl.BlockSpec((1, N_KG, LANES, rows), lambda bi, j: (bi, 0, 0, j))],
        out_shape=[jax.ShapeDtypeStruct((b, N_QG, LANES, 2 * nk), BF16),
                   jax.ShapeDtypeStruct((b, N_KG, nk, LANES), BF16),
                   jax.ShapeDtypeStruct((b, N_KG, LANES, nk), BF16)],
        compiler_params=pltpu.CompilerParams(
            dimension_semantics=("parallel", "arbitrary"),
            vmem_limit_bytes=_vmem_limit(56 * 1024 * 1024)),
        name="proj",
    )(*([x] * ng), *([ctx] * ng), mod, modc, g, w_perm, cos_t, s1_t, s2_t, qg, kg, ones_bd)


def _attn_kernel(tq, tk, nsub, ncast, qt_ref, k_ref, vt_ref, lam_ref, subln_ref, *rest):
    w_refs, o_ref, wb_refs, bufs = rest[:ncast], rest[ncast], rest[ncast + 1:2 * ncast + 1], rest[2 * ncast + 1:]
    for w_ref, wb_ref in zip(w_refs, wb_refs):
        wb_ref[...] = w_ref[...].astype(BF16)
    nk = k_ref.shape[2]
    nc = nk // tk
    tq2 = 2 * tq
    row = lax.broadcasted_iota(jnp.int32, (LANES, tq), 0)
    lo_rows = row < HEAD_DIM
    lv = lam_ref[...]
    lam = (jnp.exp(jnp.sum(lv[0:1] * lv[1:2], axis=1, keepdims=True))
           - jnp.exp(jnp.sum(lv[2:3] * lv[3:4], axis=1, keepdims=True)) + LAM_INIT_L0)

    n_a = A_W // LANES

    s_bufs, p_bufs = bufs[:3], bufs[3:]

    def item(w):
        return divmod(w, N_QG)

    def score_chunk(w, c, m8):
        qi, u = item(w)
        kv = min(u, n_a)
        s_c = jnp.dot(k_ref[0, kv, c * tk:(c + 1) * tk, :], qt_ref[0, u, :, qi * tq2:(qi + 1) * tq2],
                      preferred_element_type=F32)
        s_bufs[w % 3][c * tk:(c + 1) * tk, :] = s_c
        mc = jnp.max(s_c.reshape(tk // ACC_ROWS, ACC_ROWS, tq2), axis=0)
        return mc if m8 is None else jnp.maximum(m8, mc)

    def exp_chunk(w, c, m, l32):
        p = jnp.exp2(s_bufs[w % 3][c * tk:(c + 1) * tk, :] - m)
        p_bufs[w % 2][c * tk:(c + 1) * tk, :] = p.astype(BF16)
        pc = jnp.sum(p.reshape(tk // ACC_ROWS, ACC_ROWS, tq2), axis=0)
        return pc if l32 is None else l32 + pc

    def value_chunk(w, c, acc):
        kv = min(item(w)[1], n_a)
        pv = jnp.dot(vt_ref[0, kv, :, c * tk:(c + 1) * tk], p_bufs[w % 2][c * tk:(c + 1) * tk, :],
                     preferred_element_type=F32)
        return pv if acc is None else acc + pv

    def finish(w, acc, l32):
        qi, u = item(w)
        ot = acc * (1.0 / jnp.sum(l32, axis=0, keepdims=True))
        o1, o2 = ot[:, :tq], ot[:, tq:]
        if u < n_a:
            ot = o1 - lam * o2
            ot = ot * lax.rsqrt(jnp.mean(ot * ot, axis=0, keepdims=True) + EPS)
            o_ref[0, u, qi * tq:(qi + 1) * tq, :] = (ot.T * (subln_ref[...] * (1.0 - LAM_INIT_L0))).astype(BF16)
        else:
            o_ref[0, u, qi * tq:(qi + 1) * tq, :] = jnp.where(lo_rows, o1, o2).T.astype(BF16)

    n_items = nsub * N_QG
    m8s = {}
    ms = {}
    ls = {}
    accs = {}
    for t in range(n_items + 2):
        if 0 <= t - 1 < n_items:
            ms[t - 1] = jnp.max(m8s.pop(t - 1), axis=0, keepdims=True)
        for c in range(nc):
            pace = m8s.get(t) if t < n_items else accs.get(t - 2)
            pace = None if pace is None else pace[0:1]
            if 0 <= t - 2 < n_items:
                accs[t - 2] = value_chunk(t - 2, c, accs.get(t - 2))
            if t < n_items:
                m8s[t] = score_chunk(t, c, m8s.get(t))
            if 0 <= t - 1 < n_items:
                m = ms[t - 1]
                if pace is not None:
                    m = m + jnp.minimum(jnp.abs(pace), 0.0)
                ls[t - 1] = exp_chunk(t - 1, c, m, ls.get(t - 1))
        if 0 <= t - 2 < n_items:
            finish(t - 2, accs.pop(t - 2), ls.pop(t - 2))


def _cast_rows(n_rows, n_steps):
    best = None
    for r in range(BF16_SUBLANES, n_rows + 1, BF16_SUBLANES):
        if n_rows % r == 0 and n_steps % (n_rows // r) == 0:
            best = r if best is None else min(best, r)
    return best


def _attn(qt, k, vt, lam_vecs, subln, s, tq, tk, nsub, weights):
    b = qt.shape[0]
    nk = k.shape[2]
    tqs = tq * nsub
    ni = s // tqs
    plans = [_cast_rows(w.shape[0], b * ni) for w in weights]
    riders = [(w, r) for w, r in zip(weights, plans) if r is not None]
    kern = functools.partial(_attn_kernel, tq, tk, nsub, len(riders))

    def w_spec(w, r):
        rep = (b * ni) // (w.shape[0] // r)
        return pl.BlockSpec((r, w.shape[1]), lambda bi, i: ((bi * ni + i) // rep, 0))

    outs = pl.pallas_call(
        kern,
        grid=(b, ni),
        in_specs=[pl.BlockSpec((1, N_QG, LANES, 2 * tqs), lambda bi, i: (bi, 0, 0, i)),
                  pl.BlockSpec((1, N_KG, nk, LANES), lambda bi, i: (bi, 0, 0, 0)),
                  pl.BlockSpec((1, N_KG, LANES, nk), lambda bi, i: (bi, 0, 0, 0)),
                  _const_spec((4, HEAD_DIM)),
                  _const_spec((1, LANES))] + [w_spec(w, r) for w, r in riders],
        out_specs=[pl.BlockSpec((1, N_QG, tqs, LANES), lambda bi, i: (bi, 0, i, 0))]
        + [w_spec(w, r) for w, r in riders],
        out_shape=[jax.ShapeDtypeStruct((b, N_QG, s, LANES), BF16)]
        + [jax.ShapeDtypeStruct(w.shape, BF16) for w, _ in riders],
        scratch_shapes=[pltpu.VMEM((nk, 2 * tq), F32)] * 3 + [pltpu.VMEM((nk, 2 * tq), BF16)] * 2,
        compiler_params=pltpu.CompilerParams(
            dimension_semantics=("arbitrary", "arbitrary"),
            vmem_limit_bytes=_vmem_limit(48 * 1024 * 1024)),
        name="attn",
    )(qt, k, vt, lam_vecs, subln, *[w for w, _ in riders])
    cast = iter(outs[1:])
    return outs[0], [next(cast) if r is not None else w.astype(BF16) for w, r in zip(weights, plans)]


def _swiglu(x_all, mod_ref, gffn_ref, wgu_ref, wd_ref):
    shift = mod_ref[0, 3:4, :]
    gain = gffn_ref[...] * (1.0 + mod_ref[0, 4:5, :])
    gate = mod_ref[0, 5:6, :]
    n = x_all.shape[0]
    nh = n // FFN_SPLIT
    outs = []
    for r0 in range(0, n, nh):
        x1 = x_all[r0:r0 + nh]
        h = _rms(x1, gain) + shift
        gu = jnp.dot(h.astype(BF16), wgu_ref[...], preferred_element_type=F32)
        dff = gu.shape[1] // 2
        g = gu[:, :dff]
        u = gu[:, dff:]
        a = (g * jax.nn.sigmoid(g)) * u
        dn = jnp.dot(a.astype(BF16), wd_ref[...], preferred_element_type=F32)
        outs.append(x1 + gate * dn)
    return outs[0] if len(outs) == 1 else jnp.concatenate(outs, axis=0)


def _mixffn_kernel(o_ref, x_ref, mod_ref, wout_ref, gffn_ref, wgu_ref, wd_ref, out_ref, res_scr):
    o = jnp.concatenate([o_ref[0, u] for u in range(N_QG)], axis=1)
    y = jnp.dot(o, wout_ref[...], preferred_element_type=F32)
    x1 = x_ref[0] + mod_ref[0, 2:3, :] * y
    res = _swiglu(x1, mod_ref, gffn_ref, wgu_ref, wd_ref)
    n = res_scr.shape[1] // RADIX
    for sl in range(res_scr.shape[0]):
        res_scr[sl] = res[:, sl * LANES:(sl + 1) * LANES]
        for r in range(RADIX):
            out_ref[0, r, :, sl * LANES:(sl + 1) * LANES] = res_scr[sl, pl.ds(r, n, stride=RADIX), :]


def _ffn_final_kernel(x_ref, y_ref, mod_ref, gffn_ref, wgu_ref, wd_ref, gfin_ref, out_ref, x_scr):
    n = x_scr.shape[1] // RADIX
    gate = mod_ref[0, 2:3, :]
    for sl in range(x_scr.shape[0]):
        cols = slice(sl * LANES, (sl + 1) * LANES)
        for r in range(RADIX):
            x_scr[sl, pl.ds(r, n, stride=RADIX), :] = (x_ref[0, r, :, cols]
                                                       + gate[:, cols] * y_ref[0, r, :, cols].astype(F32))
    x1 = jnp.concatenate([x_scr[sl] for sl in range(x_scr.shape[0])], axis=1)
    x2 = _swiglu(x1, mod_ref, gffn_ref, wgu_ref, wd_ref)
    out_ref[0] = _rms(x2, gfin_ref[...])


def _mixffn(o, x, mod, wout, gffn, wgu, wd, tm):
    b, s, d = x.shape
    tok = lambda bi, i: (bi, i, 0)
    return pl.pallas_call(
        _mixffn_kernel,
        grid=(b, s // tm),
        in_specs=[pl.BlockSpec((1, N_QG, tm, LANES), lambda bi, i: (bi, 0, i, 0)),
                  pl.BlockSpec((1, tm, d), tok),
                  pl.BlockSpec((1, N_MOD, d), lambda bi, i: (bi, 0, 0)),
                  _const_spec(wout.shape), _const_spec((1, d)), _const_spec(wgu.shape), _const_spec(wd.shape)],
        out_specs=pl.BlockSpec((1, RADIX, tm // RADIX, d), lambda bi, i: (bi, 0, i, 0)),
        out_shape=jax.ShapeDtypeStruct((b, RADIX, s // RADIX, d), F32),
        scratch_shapes=[pltpu.VMEM((d // LANES, tm, LANES), F32)],
        compiler_params=pltpu.CompilerParams(
            dimension_semantics=("parallel", "arbitrary"),
            vmem_limit_bytes=_vmem_limit(56 * 1024 * 1024)),
        name="mixffn",
    )(o, x, mod, wout, gffn, wgu, wd)


def _ffn_final(xr, yr, mod, gffn, wgu, wd, gfin, tm):
    b, _, nq, d = xr.shape
    s = nq * RADIX
    res = pl.BlockSpec((1, RADIX, tm // RADIX, d), lambda bi, i: (bi, 0, i, 0))
    return pl.pallas_call(
        _ffn_final_kernel,
        grid=(b, s // tm),
        in_specs=[res, res,
                  pl.BlockSpec((1, N_MOD, d), lambda bi, i: (bi, 0, 0)),
                  _const_spec((1, d)), _const_spec(wgu.shape), _const_spec(wd.shape), _const_spec((1, d))],
        out_specs=pl.BlockSpec((1, tm, d), lambda bi, i: (bi, i, 0)),
        out_shape=jax.ShapeDtypeStruct((b, s, d), F32),
        scratch_shapes=[pltpu.VMEM((d // LANES, tm, LANES), F32)],
        compiler_params=pltpu.CompilerParams(
            dimension_semantics=("parallel", "arbitrary"),
            vmem_limit_bytes=_vmem_limit(56 * 1024 * 1024)),
        name="ffn",
    )(xr, yr, mod, gffn, wgu, wd, gfin)


_QUARTER = ((1, 0), (0, 1), (-1, 0), (0, -1))


def _fproj_kernel(ng, *refs):
    x_refs = refs[:RADIX * ng]
    mod_ref, g_ref, w_ref, u_ref = refs[RADIX * ng:]
    d = x_refs[0].shape[2]
    ta = x_refs[0].shape[1]
    shift = mod_ref[0, 0:1, :]
    gain = g_ref[...] * (1.0 + mod_ref[0, 1:2, :])
    for t in range(ng):
        h = jnp.concatenate([(_rms(xr[0], gain) + shift).astype(BF16)
                             for xr in x_refs[RADIX * t:RADIX * (t + 1)]], axis=0)
        z = jnp.dot(h, w_ref[...], preferred_element_type=F32)
        zc = [z[m * ta:(m + 1) * ta, 0:d] for m in range(RADIX)]
        zs = [z[m * ta:(m + 1) * ta, d:2 * d] for m in range(RADIX)]
        for r in range(RADIX):
            ua = None
            ub = None
            for m in range(RADIX):
                cp, sp = _QUARTER[(r * m) % 4]
                ta_, sa = (zc[m], cp) if cp else (zs[m], -sp)
                tb_, sb = (zs[m], cp) if cp else (zc[m], sp)
                ua = sa * ta_ if ua is None else (ua + ta_ if sa > 0 else ua - ta_)
                ub = sb * tb_ if ub is None else (ub + tb_ if sb > 0 else ub - tb_)
            u_ref[0, r, 0, t * ta:(t + 1) * ta, :] = ua.astype(BF16)
            u_ref[0, r, 1, t * ta:(t + 1) * ta, :] = ub.astype(BF16)


def _fproj(x, mod, g, wcs, ta):
    b, s, d = x.shape
    nq = s // RADIX
    nqq = nq // RADIX
    na = nqq // ta
    nrb = RADIX * na
    ng = 2

    def x_spec(t, m):
        def index(bi, j):
            rb = j * ng + t
            return (bi, (rb // na) * (RADIX * na) + m * na + rb % na, 0)
        return pl.BlockSpec((1, ta, d), index)

    return pl.pallas_call(
        functools.partial(_fproj_kernel, ng),
        grid=(b, nrb // ng),
        in_specs=[x_spec(t, m) for t in range(ng) for m in range(RADIX)]
        + [pl.BlockSpec((1, N_MOD, d), lambda bi, j: (bi, 0, 0)),
           _const_spec((1, d)), _const_spec(wcs.shape)],
        out_specs=pl.BlockSpec((1, RADIX, 2, ng * ta, d), lambda bi, j: (bi, 0, 0, j, 0)),
        out_shape=jax.ShapeDtypeStruct((b, RADIX, 2, nq, d), BF16),
        compiler_params=pltpu.CompilerParams(
            dimension_semantics=("parallel", "arbitrary"),
            vmem_limit_bytes=_vmem_limit(48 * 1024 * 1024)),
        name="fproj",
    )(*([x] * (RADIX * ng)), mod, g, wcs)


def _fseq_kernel(u_ref, t_ref, y_ref):
    nq, d = u_ref.shape[3], u_ref.shape[4]
    u = u_ref[0, 0].reshape(2 * nq, d)
    y_ref[0] = jnp.dot(t_ref[0], u, preferred_element_type=F32).astype(BF16)


def _seq_dft_tables(n):
    nq = n // RADIX
    nqq = nq // RADIX
    u = np.arange(nq)
    q = (RADIX * (u % nqq) + u // nqq)[None, None, :]
    p = np.arange(nq)[None, :, None]
    r = np.arange(RADIX)[:, None, None]
    ang = 2.0 * np.pi * (((RADIX * p + r) * q) % n) / n
    t = np.concatenate([np.cos(ang), -np.sin(ang)], axis=2) / np.sqrt(n)
    return jnp.asarray(t, F32)


def _fseq(u, t, tmr):
    b, _, _, nq, d = u.shape
    nblk = nq // tmr
    return pl.pallas_call(
        _fseq_kernel,
        grid=(b, RADIX, nblk),
        in_specs=[pl.BlockSpec((1, 1, 2, nq, d), lambda bi, r, i: (bi, r, 0, 0, 0)),
                  pl.BlockSpec((1, tmr, 2 * nq), lambda bi, r, i: (r, i, 0))],
        out_specs=pl.BlockSpec((1, tmr, d), lambda bi, r, i: (bi, r * nblk + i, 0)),
        out_shape=jax.ShapeDtypeStruct((b, RADIX * nq, d), BF16),
        compiler_params=pltpu.CompilerParams(
            dimension_semantics=("parallel", "arbitrary", "arbitrary"),
            vmem_limit_bytes=_vmem_limit(40 * 1024 * 1024)),
        name="fseq",
    )(u, t)


def _rope_lane_tables(n_ctx, n_tok):
    rows_count = n_tok // GRID_W
    row = np.repeat(np.arange(rows_count), GRID_W).astype(np.float32)
    col = np.tile(np.arange(GRID_W), rows_count).astype(np.float32)
    inv = np.float32(ROPE_THETA) ** (-np.arange(ROPE_PAIRS, dtype=np.float32) / np.float32(ROPE_PAIRS))
    lane = np.arange(LANES)
    dd = lane % HEAD_DIM
    axis = dd // (2 * ROPE_PAIRS)
    half = (dd % (2 * ROPE_PAIRS)) // ROPE_PAIRS
    pair = dd % ROPE_PAIRS
    pos = np.where((axis == 0)[None, :], row[:, None], col[:, None])
    ang = (pos * inv.astype(np.float32)[pair][None, :]).astype(np.float32).astype(np.float64)
    cosv = np.cos(ang)
    sinv = np.sin(ang)
    first = (half == 0)[None, :]
    s1 = np.where(first, -sinv, 0.0)
    s2 = np.where(first, 0.0, sinv)
    pad1 = np.ones((n_ctx, LANES))
    pad0 = np.zeros((n_ctx, LANES))
    return tuple(jnp.asarray(np.concatenate([tab, pad], 0), F32)
                 for pad, tab in ((pad1, cosv), (pad0, s1), (pad0, s2)))


def _permute_w_in(w_in):
    d = w_in.shape[0]
    off_ak = A_W
    off_av = 2 * A_W
    off_bq = 3 * A_W
    off_bk = off_bq + BQ_W
    off_bv = off_bk + BKV_W
    grp = GQA_HEADS // GQA_KV_HEADS
    bq = w_in[:, off_bq:off_bk].reshape(d, GQA_KV_HEADS, grp, HEAD_DIM).swapaxes(1, 2).reshape(d, BQ_W)
    return jnp.concatenate([w_in[:, 0:A_W], bq, w_in[:, off_ak:off_av], w_in[:, off_bk:off_bv],
                            w_in[:, off_av:off_bq], w_in[:, off_bv:off_bv + BKV_W]], axis=1)


def _permute_w_out(w_out):
    d = w_out.shape[1]
    grp = GQA_HEADS // GQA_KV_HEADS
    wb = w_out[A_W:].reshape(GQA_KV_HEADS, grp, HEAD_DIM, d).swapaxes(0, 1).reshape(BQ_W, d)
    return jnp.concatenate([w_out[:A_W], wb], axis=0)


def kernel(x, c, ctx, c_ctx, l0_ada_w, l0_ada_b, l0_norm_mix, l0_w_in, l0_lambda_q1, l0_lambda_k1, l0_lambda_q2, l0_lambda_k2, l0_subln, l0_q_norm, l0_k_norm, l0_w_out, l0_norm_ffn, l0_w_gate_up, l0_w_down, l1_ada_w, l1_ada_b, l1_norm_mix, l1_w_out, l1_norm_ffn, l1_w_gate_up, l1_w_down, final_norm):
    b, s, d = x.shape
    n_ctx = ctx.shape[1]
    assert d == D_MODEL and s % GRID_W == 0

    tt = min(256, n_ctx, s)
    tq = min(128, tt)
    nsub = max(n for n in (1, 2, 4) if s % (n * tq) == 0)
    tk = min(256, n_ctx)
    tm = min(1024, s)
    tmr = min(512, s // RADIX)
    ta = min(128, s // RADIX ** 2)
    assert n_ctx % tt == 0 and s % tt == 0 and tt % tq == 0 and (n_ctx + s) % tk == 0
    assert s % tm == 0 and (s // RADIX) % tmr == 0 and (s // RADIX ** 2) % ta == 0

    n_rows = -(-(b + 1) // SUBLANES) * SUBLANES
    cond = jnp.concatenate([c, c_ctx[None, :], jnp.zeros((n_rows - b - 1, d), F32)], axis=0)
    m0, m1, wcs = _ada_fold(cond, l0_ada_w, l0_ada_b, l1_ada_w, l1_ada_b, l1_w_out)
    mod0 = m0[:b].reshape(b, N_MOD, d)
    mod0c = m0[b:b + 1].reshape(1, N_MOD, d)
    mod1 = m1[:b].reshape(b, N_MOD, d)

    w_perm = _permute_w_in(l0_w_in)
    cos_t, s1_t, s2_t = _rope_lane_tables(n_ctx, s)
    qg = jnp.tile(l0_q_norm, LANES // HEAD_DIM).reshape(1, LANES)
    kg = jnp.tile(l0_k_norm, LANES // HEAD_DIM).reshape(1, LANES)
    li = np.arange(LANES)
    ones_bd = jnp.asarray((li[:, None] // HEAD_DIM) == (li[None, :] // HEAD_DIM), BF16)
    qt, k, vt = _proj(x, ctx, mod0, mod0c, l0_norm_mix.reshape(1, d), w_perm.astype(BF16),
                      cos_t, s1_t, s2_t, qg, kg, ones_bd, tt, tq)
    lam_vecs = jnp.stack([l0_lambda_q1, l0_lambda_k1, l0_lambda_q2, l0_lambda_k2]).astype(F32)
    o, (w_out0, w_gu0, w_dn0, w_gu1, w_dn1) = _attn(
        qt, k, vt, lam_vecs, l0_subln.reshape(1, LANES), s, tq, tk, nsub,
        [_permute_w_out(l0_w_out), l0_w_gate_up, l0_w_down, l1_w_gate_up, l1_w_down])
    xr = _mixffn(o, x, mod0, w_out0, l0_norm_ffn.reshape(1, d), w_gu0, w_dn0, tm)
    xr = xr.reshape(b, s, d)

    u = _fproj(xr, mod1, l1_norm_mix.reshape(1, d), wcs, ta)
    yr = _fseq(u, _seq_dft_tables(s).astype(BF16), tmr)
    res4 = (b, RADIX, s // RADIX, d)
    return _ffn_final(xr.reshape(res4), yr.reshape(res4), mod1, l1_norm_ffn.reshape(1, d),
                      w_gu1, w_dn1, final_norm.reshape(1, d), tm)
```

```python
import functools
import math

import numpy as np
import jax
import jax.numpy as jnp
from jax import lax
from jax.experimental import pallas as pl
from jax.experimental.pallas import tpu as pltpu

F32 = jnp.float32
BF16 = jnp.bfloat16

LANES = 128
SUBLANES = 8
BF16_SUBLANES = 16
V7X_VMEM_BYTES = 64 * 1024 * 1024

D_MODEL = 1024
GRID_W = 64
DIFF_HEADS = 4
DIFF_HEAD_DIM = 64
GQA_HEADS = 8
GQA_KV_HEADS = 2
GQA_HEAD_DIM = 64
ROPE_THETA = 10000.0
ROPE_PAIRS = GQA_HEAD_DIM // 4
FOURIER_GROUPS = 8
FOURIER_GROUP_DIM = D_MODEL // FOURIER_GROUPS
EPS = 1e-6
N_MOD = 6
HEAD_DIM = 64
A_W = DIFF_HEADS * 2 * DIFF_HEAD_DIM
BQ_W = GQA_HEADS * GQA_HEAD_DIM
BKV_W = GQA_KV_HEADS * GQA_HEAD_DIM
Q_W = A_W + BQ_W
K_W = A_W + BKV_W
N_QG = Q_W // LANES
N_KG = K_W // LANES
LAM_INIT_L0 = 0.8 - 0.6 * math.exp(-0.3 * 0)
Q_SCALE = HEAD_DIM ** -0.5 * math.log2(math.e)
RADIX = 4
ACC_ROWS = 32
FFN_SPLIT = 4
VT_ROWS = LANES + BF16_SUBLANES


def _vmem_limit(nbytes):
    return int(min(nbytes, V7X_VMEM_BYTES - 4 * 1024 * 1024))


def _rms(x, g):
    ms = jnp.mean(x * x, axis=-1, keepdims=True)
    return x * lax.rsqrt(ms + EPS) * g


def _const_spec(shape):
    nd = len(shape)
    return pl.BlockSpec(shape, lambda *_: (0,) * nd, pipeline_mode=pl.Buffered(1))


def _fold_kernel(cd_ref, sd_ref, w_ref, o_ref):
    w = w_ref[...]
    d = w.shape[1]
    w_hi = w.astype(BF16)
    w_lo = (w - w_hi.astype(F32)).astype(BF16)

    def dot3(m):
        m_hi = m.astype(BF16)
        m_lo = (m - m_hi.astype(F32)).astype(BF16)
        return (jnp.dot(m_hi, w_hi, preferred_element_type=F32) + jnp.dot(m_hi, w_lo, preferred_element_type=F32)
                + jnp.dot(m_lo, w_hi, preferred_element_type=F32))

    o_ref[:, 0:d] = dot3(cd_ref[...]).astype(BF16)
    o_ref[:, d:2 * d] = dot3(sd_ref[...]).astype(BF16)


def _ada_kernel(c_ref, w0_ref, b0_ref, w1_ref, b1_ref, cd_ref, sd_ref, wo_ref, o0_ref, o1_ref, wcs_ref):
    @pl.when(pl.program_id(0) == 0)
    def _():
        o0_ref[...] = jnp.broadcast_to(b0_ref[...], o0_ref.shape)
        o1_ref[...] = jnp.broadcast_to(b1_ref[...], o1_ref.shape)

    cv = c_ref[...]
    a = (cv * jax.nn.sigmoid(cv)).astype(BF16)
    o0_ref[...] += jnp.dot(a, w0_ref[...].astype(BF16), preferred_element_type=F32)
    o1_ref[...] += jnp.dot(a, w1_ref[...].astype(BF16), preferred_element_type=F32)
    _fold_kernel(cd_ref, sd_ref, wo_ref, wcs_ref)


def _ada_fold(cond, w0, b0, w1, b1, w_out):
    r, d = cond.shape
    n = w0.shape[1]
    gd = FOURIER_GROUP_DIM
    ngr = w_out.shape[0] // gd
    bk = d // ngr
    idx = np.arange(gd)
    ang = 2.0 * np.pi * ((idx[:, None] * idx[None, :]) % gd) / gd
    cd = jnp.asarray(np.cos(ang) / np.sqrt(gd), F32)
    sd = jnp.asarray(np.sin(ang) / np.sqrt(gd), F32)
    return pl.pallas_call(
        _ada_kernel,
        grid=(ngr,),
        in_specs=[pl.BlockSpec((r, bk), lambda j: (0, j)),
                  pl.BlockSpec((bk, n), lambda j: (j, 0)),
                  pl.BlockSpec((1, n), lambda j: (0, 0)),
                  pl.BlockSpec((bk, n), lambda j: (j, 0)),
                  pl.BlockSpec((1, n), lambda j: (0, 0)),
                  pl.BlockSpec((gd, gd), lambda j: (0, 0)),
                  pl.BlockSpec((gd, gd), lambda j: (0, 0)),
                  pl.BlockSpec((gd, w_out.shape[1]), lambda j: (j, 0))],
        out_specs=[pl.BlockSpec((r, n), lambda j: (0, 0)),
                   pl.BlockSpec((r, n), lambda j: (0, 0)),
                   pl.BlockSpec((gd, 2 * w_out.shape[1]), lambda j: (j, 0))],
        out_shape=[jax.ShapeDtypeStruct((r, n), F32), jax.ShapeDtypeStruct((r, n), F32),
                   jax.ShapeDtypeStruct((w_out.shape[0], 2 * w_out.shape[1]), BF16)],
        compiler_params=pltpu.CompilerParams(dimension_semantics=("arbitrary",)),
        name="ada",
    )(cond, w0, b0.reshape(1, n), w1, b1.reshape(1, n), cd, sd, w_out)


def _proj_kernel(nlat, nct, tq, ng, *refs):
    x_refs, ctx_refs = refs[:ng], refs[ng:2 * ng]
    (mod_ref, modc_ref, g_ref, w_ref, cos_ref, s1_ref, s2_ref,
     qg_ref, kg_ref, ones_ref, qt_ref, k_ref, vt_ref) = refs[2 * ng:]
    j = pl.program_id(1)
    gain_x = g_ref[...] * (1.0 + mod_ref[0, 1:2, :])
    gain_c = g_ref[...] * (1.0 + modc_ref[0, 1:2, :])
    hs = []
    for t in range(ng):
        if ng - 1 - t >= nct:
            xin, gain, shift = x_refs[t][0], gain_x, mod_ref[0, 0:1, :]
        else:
            is_ctx = j * ng + t >= nlat
            xin = jnp.where(is_ctx, ctx_refs[t][0], x_refs[t][0])
            gain = jnp.where(is_ctx, gain_c, gain_x)
            shift = jnp.where(is_ctx, modc_ref[0, 0:1, :], mod_ref[0, 0:1, :])
        ms = jnp.mean(xin * xin, axis=-1, keepdims=True)
        hs.append((xin * lax.rsqrt(ms + EPS) * gain + shift).astype(BF16))
    h = jnp.concatenate(hs, axis=0)

    tt = h.shape[0]
    cosv = cos_ref[...]
    s1 = s1_ref[...]
    s2 = s2_ref[...]
    row = lax.broadcasted_iota(jnp.int32, (LANES, tt), 0)
    lo_rows = row < HEAD_DIM

    def rope(xs):
        return xs * cosv + pltpu.roll(xs, LANES - ROPE_PAIRS, 1) * s1 + pltpu.roll(xs, ROPE_PAIRS, 1) * s2

    def headnorm(xs, gain):
        sq = xs * xs
        hi = sq.astype(BF16)
        lo = (sq - hi.astype(F32)).astype(BF16)
        ss = (jnp.dot(hi, ones_ref[...], preferred_element_type=F32)
              + jnp.dot(lo, ones_ref[...], preferred_element_type=F32))
        return xs * lax.rsqrt(ss * (1.0 / HEAD_DIM) + EPS) * gain

    def finish_slab(idx, xs):
        if idx < N_QG:
            s = idx
            if s >= A_W // LANES:
                xs = headnorm(xs, qg_ref[...])
            qst = (rope(xs) * Q_SCALE).T
            q_lo = jnp.where(lo_rows, qst, 0.0)
            q_hi = qst - q_lo
            for a in range(tt // tq):
                qt_ref[0, s, :, 2 * a * tq:(2 * a + 1) * tq] = q_lo[:, a * tq:(a + 1) * tq].astype(BF16)
                qt_ref[0, s, :, (2 * a + 1) * tq:(2 * a + 2) * tq] = q_hi[:, a * tq:(a + 1) * tq].astype(BF16)
        elif idx < N_QG + N_KG:
            s = idx - N_QG
            if s >= A_W // LANES:
                xs = headnorm(xs, kg_ref[...])
            k_ref[0, s] = rope(xs).astype(BF16)
        else:
            s = idx - N_QG - N_KG
            vt_ref[0, s, 0:LANES, :] = xs.T.astype(BF16)
            vt_ref[0, s, LANES:VT_ROWS, :] = jnp.ones((VT_ROWS - LANES, tt), BF16)

    gw = 2 * LANES
    n_groups = w_ref.shape[1] // gw
    pending = None
    for gi in range(n_groups + 1):
        z = None
        if gi < n_groups:
            z = jnp.dot(h, w_ref[:, gi * gw:(gi + 1) * gw], preferred_element_type=F32)
        if pending is not None:
            pg, pz = pending
            for half in range(gw // LANES):
                finish_slab(pg * (gw // LANES) + half, pz[:, half * LANES:(half + 1) * LANES])
        pending = (gi, z)


def _proj(x, ctx, mod, modc, g, w_perm, cos_t, s1_t, s2_t, qg, kg, ones_bd, tt, tq):
    b, s, d = x.shape
    nctx = ctx.shape[1]
    nct = nctx // tt
    nk = nctx + s
    nj = nk // tt
    ng = max(gsz for gsz in (1, 2, 3) if nj % gsz == 0)
    rows = ng * tt
    n_in = w_perm.shape[1]
    nlat = nj - nct
    kern = functools.partial(_proj_kernel, nlat, nct, tq, ng)

    def x_spec(t):
        return pl.BlockSpec((1, tt, d), lambda bi, j: (bi, jnp.minimum(j * ng + t, nlat - 1), 0))

    def ctx_spec(t):
        return pl.BlockSpec((1, tt, d), lambda bi, j: (bi, jnp.maximum(j * ng + t - nlat, 0), 0))

    tab_spec = pl.BlockSpec((rows, LANES), lambda bi, j: (j, 0))
    return pl.pallas_call(
        kern,
        grid=(b, nj // ng),
        in_specs=[x_spec(t) for t in range(ng)] + [ctx_spec(t) for t in range(ng)]
        + [pl.BlockSpec((1, N_MOD, d), lambda bi, j: (bi, 0, 0)),
           _const_spec((1, N_MOD, d)),
           _const_spec((1, d)),
           _const_spec((d, n_in)),
           tab_spec, tab_spec, tab_spec,
           _const_spec((1, LANES)),
           _const_spec((1, LANES)),
           _const_spec((LANES, LANES))],
        out_specs=[pl.BlockSpec((1, N_QG, LANES, 2 * rows), lambda bi, j: (bi, 0, 0, j)),
                   pl.BlockSpec((1, N_KG, rows, LANES), lambda bi, j: (bi, 0, j, 0)),
                   pl.BlockSpec((1, N_KG, VT_ROWS, rows), lambda bi, j: (bi, 0, 0, j))],
        out_shape=[jax.ShapeDtypeStruct((b, N_QG, LANES, 2 * nk), BF16),
                   jax.ShapeDtypeStruct((b, N_KG, nk, LANES), BF16),
                   jax.ShapeDtypeStruct((b, N_KG, VT_ROWS, nk), BF16)],
        compiler_params=pltpu.CompilerParams(
            dimension_semantics=("parallel", "arbitrary"),
            vmem_limit_bytes=_vmem_limit(56 * 1024 * 1024)),
        name="proj",
    )(*([x] * ng), *([ctx] * ng), mod, modc, g, w_perm, cos_t, s1_t, s2_t, qg, kg, ones_bd)


def _attn_kernel(tq, tk, nsub, ncast, qt_ref, k_ref, vt_ref, lam_ref, subln_ref, *rest):
    w_refs, o_ref, wb_refs, bufs = rest[:ncast], rest[ncast], rest[ncast + 1:2 * ncast + 1], rest[2 * ncast + 1:]
    for w_ref, wb_ref in zip(w_refs, wb_refs):
        wb_ref[...] = w_ref[...].astype(BF16)
    nk = k_ref.shape[2]
    nc = nk // tk
    tq2 = 2 * tq
    row = lax.broadcasted_iota(jnp.int32, (LANES, tq), 0)
    lo_rows = row < HEAD_DIM
    lv = lam_ref[...]
    lam = (jnp.exp(jnp.sum(lv[0:1] * lv[1:2], axis=1, keepdims=True))
           - jnp.exp(jnp.sum(lv[2:3] * lv[3:4], axis=1, keepdims=True)) + LAM_INIT_L0)

    n_a = A_W // LANES

    s_bufs, p_bufs = bufs[:3], bufs[3:]

    def item(w):
        return divmod(w, N_QG)

    def score_chunk(w, c, m8):
        qi, u = item(w)
        kv = min(u, n_a)
        s_c = jnp.dot(k_ref[0, kv, c * tk:(c + 1) * tk, :], qt_ref[0, u, :, qi * tq2:(qi + 1) * tq2],
                      preferred_element_type=F32)
        s_bufs[w % 3][c * tk:(c + 1) * tk, :] = s_c
        mc = jnp.max(s_c.reshape(tk // ACC_ROWS, ACC_ROWS, tq2), axis=0)
        return mc if m8 is None else jnp.maximum(m8, mc)

    def exp_chunk(w, c, m):
        p_bufs[w % 2][c * tk:(c + 1) * tk, :] = jnp.exp2((s_bufs[w % 3][c * tk:(c + 1) * tk, :] - m).astype(BF16))

    def value_chunk(w, c, acc):
        kv = min(item(w)[1], n_a)
        pv = jnp.dot(vt_ref[0, kv, :, c * tk:(c + 1) * tk], p_bufs[w % 2][c * tk:(c + 1) * tk, :],
                     preferred_element_type=F32)
        return pv if acc is None else acc + pv

    def finish(w, acc):
        qi, u = item(w)
        ot = acc[0:LANES] * (1.0 / acc[LANES:LANES + 1])
        o1, o2 = ot[:, :tq], ot[:, tq:]
        if u < n_a:
            ot = o1 - lam * o2
            ot = ot * lax.rsqrt(jnp.mean(ot * ot, axis=0, keepdims=True) + EPS)
            o_ref[0, u, qi * tq:(qi + 1) * tq, :] = (ot.T * (subln_ref[...] * (1.0 - LAM_INIT_L0))).astype(BF16)
        else:
            o_ref[0, u, qi * tq:(qi + 1) * tq, :] = jnp.where(lo_rows, o1, o2).T.astype(BF16)

    n_items = nsub * N_QG
    m8s = {}
    ms = {}
    accs = {}
    for t in range(n_items + 2):
        if 0 <= t - 1 < n_items:
            ms[t - 1] = jnp.max(m8s.pop(t - 1), axis=0, keepdims=True)
        for c in range(nc):
            pace = m8s.get(t) if t < n_items else accs.get(t - 2)
            pace = None if pace is None else pace[0:1]
            if 0 <= t - 2 < n_items:
                accs[t - 2] = value_chunk(t - 2, c, accs.get(t - 2))
            if t < n_items:
                m8s[t] = score_chunk(t, c, m8s.get(t))
            if 0 <= t - 1 < n_items:
                m = ms[t - 1]
                if pace is not None:
                    m = m + jnp.minimum(jnp.abs(pace), 0.0)
                exp_chunk(t - 1, c, m)
        if 0 <= t - 2 < n_items:
            finish(t - 2, accs.pop(t - 2))


def _cast_rows(n_rows, n_steps):
    best = None
    for r in range(BF16_SUBLANES, n_rows + 1, BF16_SUBLANES):
        if n_rows % r == 0 and n_steps % (n_rows // r) == 0:
            best = r if best is None else min(best, r)
    return best


def _attn(qt, k, vt, lam_vecs, subln, s, tq, tk, nsub, weights):
    b = qt.shape[0]
    nk = k.shape[2]
    tqs = tq * nsub
    ni = s // tqs
    plans = [_cast_rows(w.shape[0], b * ni) for w in weights]
    riders = [(w, r) for w, r in zip(weights, plans) if r is not None]
    kern = functools.partial(_attn_kernel, tq, tk, nsub, len(riders))

    def w_spec(w, r):
        rep = (b * ni) // (w.shape[0] // r)
        return pl.BlockSpec((r, w.shape[1]), lambda bi, i: ((bi * ni + i) // rep, 0))

    outs = pl.pallas_call(
        kern,
        grid=(b, ni),
        in_specs=[pl.BlockSpec((1, N_QG, LANES, 2 * tqs), lambda bi, i: (bi, 0, 0, i)),
                  pl.BlockSpec((1, N_KG, nk, LANES), lambda bi, i: (bi, 0, 0, 0)),
                  pl.BlockSpec((1, N_KG, VT_ROWS, nk), lambda bi, i: (bi, 0, 0, 0)),
                  _const_spec((4, HEAD_DIM)),
                  _const_spec((1, LANES))] + [w_spec(w, r) for w, r in riders],
        out_specs=[pl.BlockSpec((1, N_QG, tqs, LANES), lambda bi, i: (bi, 0, i, 0))]
        + [w_spec(w, r) for w, r in riders],
        out_shape=[jax.ShapeDtypeStruct((b, N_QG, s, LANES), BF16)]
        + [jax.ShapeDtypeStruct(w.shape, BF16) for w, _ in riders],
        scratch_shapes=[pltpu.VMEM((nk, 2 * tq), F32)] * 3 + [pltpu.VMEM((nk, 2 * tq), BF16)] * 2,
        compiler_params=pltpu.CompilerParams(
            dimension_semantics=("arbitrary", "arbitrary"),
            vmem_limit_bytes=_vmem_limit(48 * 1024 * 1024)),
        name="attn",
    )(qt, k, vt, lam_vecs, subln, *[w for w, _ in riders])
    cast = iter(outs[1:])
    return outs[0], [next(cast) if r is not None else w.astype(BF16) for w, r in zip(weights, plans)]


def _swiglu(x_all, mod_ref, gffn_ref, wgu_ref, wd_ref):
    shift = mod_ref[0, 3:4, :]
    gain = gffn_ref[...] * (1.0 + mod_ref[0, 4:5, :])
    gate = mod_ref[0, 5:6, :]
    n = x_all.shape[0]
    nh = n // FFN_SPLIT
    outs = []
    for r0 in range(0, n, nh):
        x1 = x_all[r0:r0 + nh]
        h = _rms(x1, gain) + shift
        gu = jnp.dot(h.astype(BF16), wgu_ref[...], preferred_element_type=F32)
        dff = gu.shape[1] // 2
        g = gu[:, :dff]
        u = gu[:, dff:]
        a = (g * jax.nn.sigmoid(g)) * u
        dn = jnp.dot(a.astype(BF16), wd_ref[...], preferred_element_type=F32)
        outs.append(x1 + gate * dn)
    return outs[0] if len(outs) == 1 else jnp.concatenate(outs, axis=0)


def _mixffn_kernel(o_ref, x_ref, mod_ref, wout_ref, gffn_ref, wgu_ref, wd_ref, out_ref, res_scr):
    o = jnp.concatenate([o_ref[0, u] for u in range(N_QG)], axis=1)
    y = jnp.dot(o, wout_ref[...], preferred_element_type=F32)
    x1 = x_ref[0] + mod_ref[0, 2:3, :] * y
    res = _swiglu(x1, mod_ref, gffn_ref, wgu_ref, wd_ref)
    n = res_scr.shape[1] // RADIX
    for sl in range(res_scr.shape[0]):
        res_scr[sl] = res[:, sl * LANES:(sl + 1) * LANES]
        for r in range(RADIX):
            out_ref[0, r, :, sl * LANES:(sl + 1) * LANES] = res_scr[sl, pl.ds(r, n, stride=RADIX), :]


def _ffn_final_kernel(x_ref, y_ref, mod_ref, gffn_ref, wgu_ref, wd_ref, gfin_ref, out_ref, x_scr):
    n = x_scr.shape[1] // RADIX
    gate = mod_ref[0, 2:3, :]
    for sl in range(x_scr.shape[0]):
        cols = slice(sl * LANES, (sl + 1) * LANES)
        for r in range(RADIX):
            x_scr[sl, pl.ds(r, n, stride=RADIX), :] = (x_ref[0, r, :, cols]
                                                       + gate[:, cols] * y_ref[0, r, :, cols].astype(F32))
    x1 = jnp.concatenate([x_scr[sl] for sl in range(x_scr.shape[0])], axis=1)
    x2 = _swiglu(x1, mod_ref, gffn_ref, wgu_ref, wd_ref)
    out_ref[0] = _rms(x2, gfin_ref[...])


def _mixffn(o, x, mod, wout, gffn, wgu, wd, tm):
    b, s, d = x.shape
    tok = lambda bi, i: (bi, i, 0)
    return pl.pallas_call(
        _mixffn_kernel,
        grid=(b, s // tm),
        in_specs=[pl.BlockSpec((1, N_QG, tm, LANES), lambda bi, i: (bi, 0, i, 0)),
                  pl.BlockSpec((1, tm, d), tok),
                  pl.BlockSpec((1, N_MOD, d), lambda bi, i: (bi, 0, 0)),
                  _const_spec(wout.shape), _const_spec((1, d)), _const_spec(wgu.shape), _const_spec(wd.shape)],
        out_specs=pl.BlockSpec((1, RADIX, tm // RADIX, d), lambda bi, i: (bi, 0, i, 0)),
        out_shape=jax.ShapeDtypeStruct((b, RADIX, s // RADIX, d), F32),
        scratch_shapes=[pltpu.VMEM((d // LANES, tm, LANES), F32)],
        compiler_params=pltpu.CompilerParams(
            dimension_semantics=("parallel", "arbitrary"),
            vmem_limit_bytes=_vmem_limit(56 * 1024 * 1024)),
        name="mixffn",
    )(o, x, mod, wout, gffn, wgu, wd)


def _ffn_final(xr, yr, mod, gffn, wgu, wd, gfin, tm):
    b, _, nq, d = xr.shape
    s = nq * RADIX
    res = pl.BlockSpec((1, RADIX, tm // RADIX, d), lambda bi, i: (bi, 0, i, 0))
    return pl.pallas_call(
        _ffn_final_kernel,
        grid=(b, s // tm),
        in_specs=[res, res,
                  pl.BlockSpec((1, N_MOD, d), lambda bi, i: (bi, 0, 0)),
                  _const_spec((1, d)), _const_spec(wgu.shape), _const_spec(wd.shape), _const_spec((1, d))],
        out_specs=pl.BlockSpec((1, tm, d), lambda bi, i: (bi, i, 0)),
        out_shape=jax.ShapeDtypeStruct((b, s, d), F32),
        scratch_shapes=[pltpu.VMEM((d // LANES, tm, LANES), F32)],
        compiler_params=pltpu.CompilerParams(
            dimension_semantics=("parallel", "arbitrary"),
            vmem_limit_bytes=_vmem_limit(56 * 1024 * 1024)),
        name="ffn",
    )(xr, yr, mod, gffn, wgu, wd, gfin)


_QUARTER = ((1, 0), (0, 1), (-1, 0), (0, -1))


def _fproj_kernel(ng, *refs):
    x_refs = refs[:RADIX * ng]
    mod_ref, g_ref, w_ref, u_ref = refs[RADIX * ng:]
    d = x_refs[0].shape[2]
    ta = x_refs[0].shape[1]
    shift = mod_ref[0, 0:1, :]
    gain = g_ref[...] * (1.0 + mod_ref[0, 1:2, :])
    for t in range(ng):
        h = jnp.concatenate([(_rms(xr[0], gain) + shift).astype(BF16)
                             for xr in x_refs[RADIX * t:RADIX * (t + 1)]], axis=0)
        z = jnp.dot(h, w_ref[...], preferred_element_type=F32)
        zc = [z[m * ta:(m + 1) * ta, 0:d] for m in range(RADIX)]
        zs = [z[m * ta:(m + 1) * ta, d:2 * d] for m in range(RADIX)]
        for r in range(RADIX):
            ua = None
            ub = None
            for m in range(RADIX):
                cp, sp = _QUARTER[(r * m) % 4]
                ta_, sa = (zc[m], cp) if cp else (zs[m], -sp)
                tb_, sb = (zs[m], cp) if cp else (zc[m], sp)
                ua = sa * ta_ if ua is None else (ua + ta_ if sa > 0 else ua - ta_)
                ub = sb * tb_ if ub is None else (ub + tb_ if sb > 0 else ub - tb_)
            u_ref[0, r, 0, t * ta:(t + 1) * ta, :] = ua.astype(BF16)
            u_ref[0, r, 1, t * ta:(t + 1) * ta, :] = ub.astype(BF16)


def _fproj(x, mod, g, wcs, ta):
    b, s, d = x.shape
    nq = s // RADIX
    nqq = nq // RADIX
    na = nqq // ta
    nrb = RADIX * na
    ng = 2

    def x_spec(t, m):
        def index(bi, j):
            rb = j * ng + t
            return (bi, (rb // na) * (RADIX * na) + m * na + rb % na, 0)
        return pl.BlockSpec((1, ta, d), index)

    return pl.pallas_call(
        functools.partial(_fproj_kernel, ng),
        grid=(b, nrb // ng),
        in_specs=[x_spec(t, m) for t in range(ng) for m in range(RADIX)]
        + [pl.BlockSpec((1, N_MOD, d), lambda bi, j: (bi, 0, 0)),
           _const_spec((1, d)), _const_spec(wcs.shape)],
        out_specs=pl.BlockSpec((1, RADIX, 2, ng * ta, d), lambda bi, j: (bi, 0, 0, j, 0)),
        out_shape=jax.ShapeDtypeStruct((b, RADIX, 2, nq, d), BF16),
        compiler_params=pltpu.CompilerParams(
            dimension_semantics=("parallel", "arbitrary"),
            vmem_limit_bytes=_vmem_limit(48 * 1024 * 1024)),
        name="fproj",
    )(*([x] * (RADIX * ng)), mod, g, wcs)


def _fseq_kernel(u_ref, t_ref, y_ref):
    nq, d = u_ref.shape[3], u_ref.shape[4]
    u = u_ref[0, 0].reshape(2 * nq, d)
    y_ref[0] = jnp.dot(t_ref[0], u, preferred_element_type=F32).astype(BF16)


def _seq_dft_tables(n):
    nq = n // RADIX
    nqq = nq // RADIX
    u = np.arange(nq)
    q = (RADIX * (u % nqq) + u // nqq)[None, None, :]
    p = np.arange(nq)[None, :, None]
    r = np.arange(RADIX)[:, None, None]
    ang = 2.0 * np.pi * (((RADIX * p + r) * q) % n) / n
    t = np.concatenate([np.cos(ang), -np.sin(ang)], axis=2) / np.sqrt(n)
    return jnp.asarray(t, F32)


def _fseq(u, t, tmr):
    b, _, _, nq, d = u.shape
    nblk = nq // tmr
    return pl.pallas_call(
        _fseq_kernel,
        grid=(b, RADIX, nblk),
        in_specs=[pl.BlockSpec((1, 1, 2, nq, d), lambda bi, r, i: (bi, r, 0, 0, 0)),
                  pl.BlockSpec((1, tmr, 2 * nq), lambda bi, r, i: (r, i, 0))],
        out_specs=pl.BlockSpec((1, tmr, d), lambda bi, r, i: (bi, r * nblk + i, 0)),
        out_shape=jax.ShapeDtypeStruct((b, RADIX * nq, d), BF16),
        compiler_params=pltpu.CompilerParams(
            dimension_semantics=("parallel", "arbitrary", "arbitrary"),
            vmem_limit_bytes=_vmem_limit(40 * 1024 * 1024)),
        name="fseq",
    )(u, t)


def _rope_lane_tables(n_ctx, n_tok):
    rows_count = n_tok // GRID_W
    row = np.repeat(np.arange(rows_count), GRID_W).astype(np.float32)
    col = np.tile(np.arange(GRID_W), rows_count).astype(np.float32)
    inv = np.float32(ROPE_THETA) ** (-np.arange(ROPE_PAIRS, dtype=np.float32) / np.float32(ROPE_PAIRS))
    lane = np.arange(LANES)
    dd = lane % HEAD_DIM
    axis = dd // (2 * ROPE_PAIRS)
    half = (dd % (2 * ROPE_PAIRS)) // ROPE_PAIRS
    pair = dd % ROPE_PAIRS
    pos = np.where((axis == 0)[None, :], row[:, None], col[:, None])
    ang = (pos * inv.astype(np.float32)[pair][None, :]).astype(np.float32).astype(np.float64)
    cosv = np.cos(ang)
    sinv = np.sin(ang)
    first = (half == 0)[None, :]
    s1 = np.where(first, -sinv, 0.0)
    s2 = np.where(first, 0.0, sinv)
    pad1 = np.ones((n_ctx, LANES))
    pad0 = np.zeros((n_ctx, LANES))
    return tuple(jnp.asarray(np.concatenate([tab, pad], 0), F32)
                 for pad, tab in ((pad1, cosv), (pad0, s1), (pad0, s2)))


def _permute_w_in(w_in):
    d = w_in.shape[0]
    off_ak = A_W
    off_av = 2 * A_W
    off_bq = 3 * A_W
    off_bk = off_bq + BQ_W
    off_bv = off_bk + BKV_W
    grp = GQA_HEADS // GQA_KV_HEADS
    bq = w_in[:, off_bq:off_bk].reshape(d, GQA_KV_HEADS, grp, HEAD_DIM).swapaxes(1, 2).reshape(d, BQ_W)
    return jnp.concatenate([w_in[:, 0:A_W], bq, w_in[:, off_ak:off_av], w_in[:, off_bk:off_bv],
                            w_in[:, off_av:off_bq], w_in[:, off_bv:off_bv + BKV_W]], axis=1)


def _permute_w_out(w_out):
    d = w_out.shape[1]
    grp = GQA_HEADS // GQA_KV_HEADS
    wb = w_out[A_W:].reshape(GQA_KV_HEADS, grp, HEAD_DIM, d).swapaxes(0, 1).reshape(BQ_W, d)
    return jnp.concatenate([w_out[:A_W], wb], axis=0)


def kernel(x, c, ctx, c_ctx, l0_ada_w, l0_ada_b, l0_norm_mix, l0_w_in, l0_lambda_q1, l0_lambda_k1, l0_lambda_q2, l0_lambda_k2, l0_subln, l0_q_norm, l0_k_norm, l0_w_out, l0_norm_ffn, l0_w_gate_up, l0_w_down, l1_ada_w, l1_ada_b, l1_norm_mix, l1_w_out, l1_norm_ffn, l1_w_gate_up, l1_w_down, final_norm):
    b, s, d = x.shape
    n_ctx = ctx.shape[1]
    assert d == D_MODEL and s % GRID_W == 0

    tt = min(256, n_ctx, s)
    tq = min(128, tt)
    nsub = max(n for n in (1, 2, 4) if s % (n * tq) == 0)
    tk = min(256, n_ctx)
    tm = min(1024, s)
    tmr = min(512, s // RADIX)
    ta = min(128, s // RADIX ** 2)
    assert n_ctx % tt == 0 and s % tt == 0 and tt % tq == 0 and (n_ctx + s) % tk == 0
    assert s % tm == 0 and (s // RADIX) % tmr == 0 and (s // RADIX ** 2) % ta == 0

    n_rows = -(-(b + 1) // SUBLANES) * SUBLANES
    cond = jnp.concatenate([c, c_ctx[None, :], jnp.zeros((n_rows - b - 1, d), F32)], axis=0)
    m0, m1, wcs = _ada_fold(cond, l0_ada_w, l0_ada_b, l1_ada_w, l1_ada_b, l1_w_out)
    mod0 = m0[:b].reshape(b, N_MOD, d)
    mod0c = m0[b:b + 1].reshape(1, N_MOD, d)
    mod1 = m1[:b].reshape(b, N_MOD, d)

    w_perm = _permute_w_in(l0_w_in)
    cos_t, s1_t, s2_t = _rope_lane_tables(n_ctx, s)
    qg = jnp.tile(l0_q_norm, LANES // HEAD_DIM).reshape(1, LANES)
    kg = jnp.tile(l0_k_norm, LANES // HEAD_DIM).reshape(1, LANES)
    li = np.arange(LANES)
    ones_bd = jnp.asarray((li[:, None] // HEAD_DIM) == (li[None, :] // HEAD_DIM), BF16)
    qt, k, vt = _proj(x, ctx, mod0, mod0c, l0_norm_mix.reshape(1, d), w_perm.astype(BF16),
                      cos_t, s1_t, s2_t, qg, kg, ones_bd, tt, tq)
    lam_vecs = jnp.stack([l0_lambda_q1, l0_lambda_k1, l0_lambda_q2, l0_lambda_k2]).astype(F32)
    o, (w_out0, w_gu0, w_dn0, w_gu1, w_dn1) = _attn(
        qt, k, vt, lam_vecs, l0_subln.reshape(1, LANES), s, tq, tk, nsub,
        [_permute_w_out(l0_w_out), l0_w_gate_up, l0_w_down, l1_w_gate_up, l1_w_down])
    xr = _mixffn(o, x, mod0, w_out0, l0_norm_ffn.reshape(1, d), w_gu0, w_dn0, tm)
    xr = xr.reshape(b, s, d)

    u = _fproj(xr, mod1, l1_norm_mix.reshape(1, d), wcs, ta)
    yr = _fseq(u, _seq_dft_tables(s).astype(BF16), tmr)
    res4 = (b, RADIX, s // RADIX, d)
    return _ffn_final(xr.reshape(res4), yr.reshape(res4), mod1, l1_norm_ffn.reshape(1, d),
                      w_gu1, w_dn1, final_norm.reshape(1, d), tm)
```

```python
import functools
import math

import numpy as np
import jax
import jax.numpy as jnp
from jax import lax
from jax.experimental import pallas as pl
from jax.experimental.pallas import tpu as pltpu

F32 = jnp.float32
BF16 = jnp.bfloat16

LANES = 128
SUBLANES = 8
BF16_SUBLANES = 16
V7X_VMEM_BYTES = 64 * 1024 * 1024

D_MODEL = 1024
GRID_W = 64
DIFF_HEADS = 4
DIFF_HEAD_DIM = 64
GQA_HEADS = 8
GQA_KV_HEADS = 2
GQA_HEAD_DIM = 64
ROPE_THETA = 10000.0
ROPE_PAIRS = GQA_HEAD_DIM // 4
FOURIER_GROUPS = 8
FOURIER_GROUP_DIM = D_MODEL // FOURIER_GROUPS
EPS = 1e-6
N_MOD = 6
HEAD_DIM = 64
A_W = DIFF_HEADS * 2 * DIFF_HEAD_DIM
BQ_W = GQA_HEADS * GQA_HEAD_DIM
BKV_W = GQA_KV_HEADS * GQA_HEAD_DIM
Q_W = A_W + BQ_W
K_W = A_W + BKV_W
N_QG = Q_W // LANES
N_KG = K_W // LANES
LAM_INIT_L0 = 0.8 - 0.6 * math.exp(-0.3 * 0)
Q_SCALE = HEAD_DIM ** -0.5 * math.log2(math.e)
RADIX = 4
ACC_ROWS = 32
FFN_SPLIT = 4
VT_ROWS = LANES + BF16_SUBLANES


def _vmem_limit(nbytes):
    return int(min(nbytes, V7X_VMEM_BYTES - 4 * 1024 * 1024))


def _rms(x, g):
    ms = jnp.mean(x * x, axis=-1, keepdims=True)
    return x * lax.rsqrt(ms + EPS) * g


def _const_spec(shape):
    nd = len(shape)
    return pl.BlockSpec(shape, lambda *_: (0,) * nd, pipeline_mode=pl.Buffered(1))


def _fold_kernel(cd_ref, sd_ref, w_ref, o_ref):
    w = w_ref[...]
    d = w.shape[1]
    w_hi = w.astype(BF16)
    w_lo = (w - w_hi.astype(F32)).astype(BF16)

    def dot3(m):
        m_hi = m.astype(BF16)
        m_lo = (m - m_hi.astype(F32)).astype(BF16)
        return (jnp.dot(m_hi, w_hi, preferred_element_type=F32) + jnp.dot(m_hi, w_lo, preferred_element_type=F32)
                + jnp.dot(m_lo, w_hi, preferred_element_type=F32))

    o_ref[:, 0:d] = dot3(cd_ref[...]).astype(BF16)
    o_ref[:, d:2 * d] = dot3(sd_ref[...]).astype(BF16)


def _ada_kernel(c_ref, w0_ref, b0_ref, w1_ref, b1_ref, cd_ref, sd_ref, wo_ref, o0_ref, o1_ref, wcs_ref):
    @pl.when(pl.program_id(0) == 0)
    def _():
        o0_ref[...] = jnp.broadcast_to(b0_ref[...], o0_ref.shape)
        o1_ref[...] = jnp.broadcast_to(b1_ref[...], o1_ref.shape)

    cv = c_ref[...]
    a = (cv * jax.nn.sigmoid(cv)).astype(BF16)
    o0_ref[...] += jnp.dot(a, w0_ref[...].astype(BF16), preferred_element_type=F32)
    o1_ref[...] += jnp.dot(a, w1_ref[...].astype(BF16), preferred_element_type=F32)
    _fold_kernel(cd_ref, sd_ref, wo_ref, wcs_ref)


def _ada_fold(cond, w0, b0, w1, b1, w_out):
    r, d = cond.shape
    n = w0.shape[1]
    gd = FOURIER_GROUP_DIM
    ngr = w_out.shape[0] // gd
    bk = d // ngr
    idx = np.arange(gd)
    ang = 2.0 * np.pi * ((idx[:, None] * idx[None, :]) % gd) / gd
    cd = jnp.asarray(np.cos(ang) / np.sqrt(gd), F32)
    sd = jnp.asarray(np.sin(ang) / np.sqrt(gd), F32)
    return pl.pallas_call(
        _ada_kernel,
        grid=(ngr,),
        in_specs=[pl.BlockSpec((r, bk), lambda j: (0, j)),
                  pl.BlockSpec((bk, n), lambda j: (j, 0)),
                  pl.BlockSpec((1, n), lambda j: (0, 0)),
                  pl.BlockSpec((bk, n), lambda j: (j, 0)),
                  pl.BlockSpec((1, n), lambda j: (0, 0)),
                  pl.BlockSpec((gd, gd), lambda j: (0, 0)),
                  pl.BlockSpec((gd, gd), lambda j: (0, 0)),
                  pl.BlockSpec((gd, w_out.shape[1]), lambda j: (j, 0))],
        out_specs=[pl.BlockSpec((r, n), lambda j: (0, 0)),
                   pl.BlockSpec((r, n), lambda j: (0, 0)),
                   pl.BlockSpec((gd, 2 * w_out.shape[1]), lambda j: (j, 0))],
        out_shape=[jax.ShapeDtypeStruct((r, n), F32), jax.ShapeDtypeStruct((r, n), F32),
                   jax.ShapeDtypeStruct((w_out.shape[0], 2 * w_out.shape[1]), BF16)],
        compiler_params=pltpu.CompilerParams(dimension_semantics=("arbitrary",)),
        name="ada",
    )(cond, w0, b0.reshape(1, n), w1, b1.reshape(1, n), cd, sd, w_out)


def _proj_kernel(nlat, nct, tq, ng, *refs):
    x_refs, ctx_refs = refs[:ng], refs[ng:2 * ng]
    (mod_ref, modc_ref, g_ref, w_ref, cos_ref, s1_ref, s2_ref,
     qg_ref, kg_ref, ones_ref, qt_ref, k_ref, vt_ref) = refs[2 * ng:]
    j = pl.program_id(1)
    gain_x = g_ref[...] * (1.0 + mod_ref[0, 1:2, :])
    gain_c = g_ref[...] * (1.0 + modc_ref[0, 1:2, :])
    hs = []
    for t in range(ng):
        if ng - 1 - t >= nct:
            xin, gain, shift = x_refs[t][0], gain_x, mod_ref[0, 0:1, :]
        else:
            is_ctx = j * ng + t >= nlat
            xin = jnp.where(is_ctx, ctx_refs[t][0], x_refs[t][0])
            gain = jnp.where(is_ctx, gain_c, gain_x)
            shift = jnp.where(is_ctx, modc_ref[0, 0:1, :], mod_ref[0, 0:1, :])
        ms = jnp.mean(xin * xin, axis=-1, keepdims=True)
        hs.append((xin * lax.rsqrt(ms + EPS) * gain + shift).astype(BF16))
    h = jnp.concatenate(hs, axis=0)

    tt = h.shape[0]
    cosv = cos_ref[...]
    s1 = s1_ref[...]
    s2 = s2_ref[...]
    row = lax.broadcasted_iota(jnp.int32, (LANES, tt), 0)
    lo_rows = row < HEAD_DIM

    def rope(xs):
        return xs * cosv + pltpu.roll(xs, LANES - ROPE_PAIRS, 1) * s1 + pltpu.roll(xs, ROPE_PAIRS, 1) * s2

    def headnorm(xs, gain):
        sq = xs * xs
        hi = sq.astype(BF16)
        lo = (sq - hi.astype(F32)).astype(BF16)
        ss = (jnp.dot(hi, ones_ref[...], preferred_element_type=F32)
              + jnp.dot(lo, ones_ref[...], preferred_element_type=F32))
        return xs * lax.rsqrt(ss * (1.0 / HEAD_DIM) + EPS) * gain

    def finish_slab(idx, xs):
        if idx < N_QG:
            s = idx
            if s >= A_W // LANES:
                xs = headnorm(xs, qg_ref[...])
            qst = (rope(xs) * Q_SCALE).T
            q_lo = jnp.where(lo_rows, qst, 0.0)
            q_hi = qst - q_lo
            for a in range(tt // tq):
                qt_ref[0, s, :, 2 * a * tq:(2 * a + 1) * tq] = q_lo[:, a * tq:(a + 1) * tq].astype(BF16)
                qt_ref[0, s, :, (2 * a + 1) * tq:(2 * a + 2) * tq] = q_hi[:, a * tq:(a + 1) * tq].astype(BF16)
        elif idx < N_QG + N_KG:
            s = idx - N_QG
            if s >= A_W // LANES:
                xs = headnorm(xs, kg_ref[...])
            k_ref[0, s] = rope(xs).astype(BF16)
        else:
            s = idx - N_QG - N_KG
            vt_ref[0, s, 0:LANES, :] = xs.T.astype(BF16)
            vt_ref[0, s, LANES:VT_ROWS, :] = jnp.ones((VT_ROWS - LANES, tt), BF16)

    gw = 2 * LANES
    n_groups = w_ref.shape[1] // gw
    pending = None
    for gi in range(n_groups + 1):
        z = None
        if gi < n_groups:
            z = jnp.dot(h, w_ref[:, gi * gw:(gi + 1) * gw], preferred_element_type=F32)
        if pending is not None:
            pg, pz = pending
            for half in range(gw // LANES):
                finish_slab(pg * (gw // LANES) + half, pz[:, half * LANES:(half + 1) * LANES])
        pending = (gi, z)


def _proj(x, ctx, mod, modc, g, w_perm, cos_t, s1_t, s2_t, qg, kg, ones_bd, tt, tq):
    b, s, d = x.shape
    nctx = ctx.shape[1]
    nct = nctx // tt
    nk = nctx + s
    nj = nk // tt
    ng = max(gsz for gsz in (1, 2, 3) if nj % gsz == 0)
    rows = ng * tt
    n_in = w_perm.shape[1]
    nlat = nj - nct
    kern = functools.partial(_proj_kernel, nlat, nct, tq, ng)

    def x_spec(t):
        return pl.BlockSpec((1, tt, d), lambda bi, j: (bi, jnp.minimum(j * ng + t, nlat - 1), 0))

    def ctx_spec(t):
        return pl.BlockSpec((1, tt, d), lambda bi, j: (bi, jnp.maximum(j * ng + t - nlat, 0), 0))

    tab_spec = pl.BlockSpec((rows, LANES), lambda bi, j: (j, 0))
    return pl.pallas_call(
        kern,
        grid=(b, nj // ng),
        in_specs=[x_spec(t) for t in range(ng)] + [ctx_spec(t) for t in range(ng)]
        + [pl.BlockSpec((1, N_MOD, d), lambda bi, j: (bi, 0, 0)),
           _const_spec((1, N_MOD, d)),
           _const_spec((1, d)),
           _const_spec((d, n_in)),
           tab_spec, tab_spec, tab_spec,
           _const_spec((1, LANES)),
           _const_spec((1, LANES)),
           _const_spec((LANES, LANES))],
        out_specs=[pl.BlockSpec((1, N_QG, LANES, 2 * rows), lambda bi, j: (bi, 0, 0, j)),
                   pl.BlockSpec((1, N_KG, rows, LANES), lambda bi, j: (bi, 0, j, 0)),
                   pl.BlockSpec((1, N_KG, VT_ROWS, rows), lambda bi, j: (bi, 0, 0, j))],
        out_shape=[jax.ShapeDtypeStruct((b, N_QG, LANES, 2 * nk), BF16),
                   jax.ShapeDtypeStruct((b, N_KG, nk, LANES), BF16),
                   jax.ShapeDtypeStruct((b, N_KG, VT_ROWS, nk), BF16)],
        compiler_params=pltpu.CompilerParams(
            dimension_semantics=("parallel", "arbitrary"),
            vmem_limit_bytes=_vmem_limit(56 * 1024 * 1024)),
        name="proj",
    )(*([x] * ng), *([ctx] * ng), mod, modc, g, w_perm, cos_t, s1_t, s2_t, qg, kg, ones_bd)


def _attn_kernel(tq, tk, nsub, ncast, qt_ref, k_ref, vt_ref, lam_ref, subln_ref, *rest):
    w_refs, o_ref, wb_refs, bufs = rest[:ncast], rest[ncast], rest[ncast + 1:2 * ncast + 1], rest[2 * ncast + 1:]
    for w_ref, wb_ref in zip(w_refs, wb_refs):
        wb_ref[...] = w_ref[...].astype(BF16)
    nk = k_ref.shape[2]
    nc = nk // tk
    tq2 = 2 * tq
    row = lax.broadcasted_iota(jnp.int32, (LANES, tq), 0)
    lo_rows = row < HEAD_DIM
    lv = lam_ref[...]
    lam = (jnp.exp(jnp.sum(lv[0:1] * lv[1:2], axis=1, keepdims=True))
           - jnp.exp(jnp.sum(lv[2:3] * lv[3:4], axis=1, keepdims=True)) + LAM_INIT_L0)

    n_a = A_W // LANES

    s_bufs, p_bufs = bufs[:3], bufs[3:]

    def item(w):
        return divmod(w, N_QG)

    def score_chunk(w, c, m8):
        qi, u = item(w)
        kv = min(u, n_a)
        s_c = jnp.dot(k_ref[0, kv, c * tk:(c + 1) * tk, :], qt_ref[0, u, :, qi * tq2:(qi + 1) * tq2],
                      preferred_element_type=F32)
        s_bufs[w % 3][c * tk:(c + 1) * tk, :] = s_c
        mc = jnp.max(s_c.reshape(tk // ACC_ROWS, ACC_ROWS, tq2), axis=0)
        return mc if m8 is None else jnp.maximum(m8, mc)

    def exp_chunk(w, c, m):
        p_bufs[w % 2][c * tk:(c + 1) * tk, :] = jnp.exp2((s_bufs[w % 3][c * tk:(c + 1) * tk, :] - m).astype(BF16))

    def value_chunk(w, c, acc):
        kv = min(item(w)[1], n_a)
        pv = jnp.dot(vt_ref[0, kv, :, c * tk:(c + 1) * tk], p_bufs[w % 2][c * tk:(c + 1) * tk, :],
                     preferred_element_type=F32)
        return pv if acc is None else acc + pv

    def finish(w, acc):
        qi, u = item(w)
        ot = acc[0:LANES] * (1.0 / acc[LANES:LANES + 1])
        o1, o2 = ot[:, :tq], ot[:, tq:]
        if u < n_a:
            ot = o1 - lam * o2
            ot = ot * lax.rsqrt(jnp.mean(ot * ot, axis=0, keepdims=True) + EPS)
            o_ref[0, u, qi * tq:(qi + 1) * tq, :] = (ot.T * (subln_ref[...] * (1.0 - LAM_INIT_L0))).astype(BF16)
        else:
            o_ref[0, u, qi * tq:(qi + 1) * tq, :] = jnp.where(lo_rows, o1, o2).T.astype(BF16)

    n_items = nsub * N_QG
    m8s = {}
    ms = {}
    accs = {}
    for t in range(n_items + 2):
        if 0 <= t - 1 < n_items:
            ms[t - 1] = jnp.max(m8s.pop(t - 1), axis=0, keepdims=True)
        for c in range(nc):
            pace = m8s.get(t) if t < n_items else accs.get(t - 2)
            pace = None if pace is None else pace[0:1]
            if 0 <= t - 2 < n_items:
                accs[t - 2] = value_chunk(t - 2, c, accs.get(t - 2))
            if t < n_items:
                m8s[t] = score_chunk(t, c, m8s.get(t))
            if 0 <= t - 1 < n_items:
                m = ms[t - 1]
                if pace is not None:
                    m = m + jnp.minimum(jnp.abs(pace), 0.0)
                exp_chunk(t - 1, c, m)
        if 0 <= t - 2 < n_items:
            finish(t - 2, accs.pop(t - 2))


def _cast_rows(n_rows, n_steps):
    best = None
    for r in range(BF16_SUBLANES, n_rows + 1, BF16_SUBLANES):
        if n_rows % r == 0 and n_steps % (n_rows // r) == 0:
            best = r if best is None else min(best, r)
    return best


def _attn(qt, k, vt, lam_vecs, subln, s, tq, tk, nsub, weights):
    b = qt.shape[0]
    nk = k.shape[2]
    tqs = tq * nsub
    ni = s // tqs
    plans = [_cast_rows(w.shape[0], b * ni) for w in weights]
    riders = [(w, r) for w, r in zip(weights, plans) if r is not None]
    kern = functools.partial(_attn_kernel, tq, tk, nsub, len(riders))

    def w_spec(w, r):
        rep = (b * ni) // (w.shape[0] // r)
        return pl.BlockSpec((r, w.shape[1]), lambda bi, i: ((bi * ni + i) // rep, 0))

    outs = pl.pallas_call(
        kern,
        grid=(b, ni),
        in_specs=[pl.BlockSpec((1, N_QG, LANES, 2 * tqs), lambda bi, i: (bi, 0, 0, i)),
                  pl.BlockSpec((1, N_KG, nk, LANES), lambda bi, i: (bi, 0, 0, 0)),
                  pl.BlockSpec((1, N_KG, VT_ROWS, nk), lambda bi, i: (bi, 0, 0, 0)),
                  _const_spec((4, HEAD_DIM)),
                  _const_spec((1, LANES))] + [w_spec(w, r) for w, r in riders],
        out_specs=[pl.BlockSpec((1, N_QG, tqs, LANES), lambda bi, i: (bi, 0, i, 0))]
        + [w_spec(w, r) for w, r in riders],
        out_shape=[jax.ShapeDtypeStruct((b, N_QG, s, LANES), BF16)]
        + [jax.ShapeDtypeStruct(w.shape, BF16) for w, _ in riders],
        scratch_shapes=[pltpu.VMEM((nk, 2 * tq), F32)] * 3 + [pltpu.VMEM((nk, 2 * tq), BF16)] * 2,
        compiler_params=pltpu.CompilerParams(
            dimension_semantics=("arbitrary", "arbitrary"),
            vmem_limit_bytes=_vmem_limit(48 * 1024 * 1024)),
        name="attn",
    )(qt, k, vt, lam_vecs, subln, *[w for w, _ in riders])
    cast = iter(outs[1:])
    return outs[0], [next(cast) if r is not None else w.astype(BF16) for w, r in zip(weights, plans)]


def _swiglu(x_all, mod_ref, gffn_ref, wgu_ref, wd_ref):
    shift = mod_ref[0, 3:4, :]
    gain = gffn_ref[...] * (1.0 + mod_ref[0, 4:5, :])
    gate = mod_ref[0, 5:6, :]
    n = x_all.shape[0]
    nh = n // FFN_SPLIT
    outs = []
    for r0 in range(0, n, nh):
        x1 = x_all[r0:r0 + nh]
        h = _rms(x1, gain) + shift
        gu = jnp.dot(h.astype(BF16), wgu_ref[...], preferred_element_type=F32)
        dff = gu.shape[1] // 2
        g = gu[:, :dff]
        u = gu[:, dff:]
        a = (g * jax.nn.sigmoid(g)) * u
        dn = jnp.dot(a.astype(BF16), wd_ref[...], preferred_element_type=F32)
        outs.append(x1 + gate * dn)
    return outs[0] if len(outs) == 1 else jnp.concatenate(outs, axis=0)


def _mixffn_kernel(o_ref, x_ref, mod_ref, wout_ref, gffn_ref, wgu_ref, wd_ref, out_ref, res_scr):
    o = jnp.concatenate([o_ref[0, u] for u in range(N_QG)], axis=1)
    y = jnp.dot(o, wout_ref[...], preferred_element_type=F32)
    x1 = x_ref[0] + mod_ref[0, 2:3, :] * y
    res = _swiglu(x1, mod_ref, gffn_ref, wgu_ref, wd_ref)
    n = res_scr.shape[1] // RADIX
    for sl in range(res_scr.shape[0]):
        res_scr[sl] = res[:, sl * LANES:(sl + 1) * LANES]
        for r in range(RADIX):
            out_ref[0, r, :, sl * LANES:(sl + 1) * LANES] = res_scr[sl, pl.ds(r, n, stride=RADIX), :]


def _ffn_final_kernel(x_ref, y_ref, mod_ref, gffn_ref, wgu_ref, wd_ref, gfin_ref, out_ref, x_scr):
    n = x_scr.shape[1] // RADIX
    gate = mod_ref[0, 2:3, :]
    for sl in range(x_scr.shape[0]):
        cols = slice(sl * LANES, (sl + 1) * LANES)
        for r in range(RADIX):
            x_scr[sl, pl.ds(r, n, stride=RADIX), :] = (x_ref[0, r, :, cols]
                                                       + gate[:, cols] * y_ref[0, r, :, cols].astype(F32))
    x1 = jnp.concatenate([x_scr[sl] for sl in range(x_scr.shape[0])], axis=1)
    x2 = _swiglu(x1, mod_ref, gffn_ref, wgu_ref, wd_ref)
    out_ref[0] = _rms(x2, gfin_ref[...])


def _mixffn(o, x, mod, wout, gffn, wgu, wd, tm):
    b, s, d = x.shape
    tok = lambda bi, i: (bi, i, 0)
    return pl.pallas_call(
        _mixffn_kernel,
        grid=(b, s // tm),
        in_specs=[pl.BlockSpec((1, N_QG, tm, LANES), lambda bi, i: (bi, 0, i, 0)),
                  pl.BlockSpec((1, tm, d), tok),
                  pl.BlockSpec((1, N_MOD, d), lambda bi, i: (bi, 0, 0)),
                  _const_spec(wout.shape), _const_spec((1, d)), _const_spec(wgu.shape), _const_spec(wd.shape)],
        out_specs=pl.BlockSpec((1, RADIX, tm // RADIX, d), lambda bi, i: (bi, 0, i, 0)),
        out_shape=jax.ShapeDtypeStruct((b, RADIX, s // RADIX, d), F32),
        scratch_shapes=[pltpu.VMEM((d // LANES, tm, LANES), F32)],
        compiler_params=pltpu.CompilerParams(
            dimension_semantics=("parallel", "arbitrary"),
            vmem_limit_bytes=_vmem_limit(56 * 1024 * 1024)),
        name="mixffn",
    )(o, x, mod, wout, gffn, wgu, wd)


def _ffn_final(xr, yr, mod, gffn, wgu, wd, gfin, tm):
    b, _, nq, d = xr.shape
    s = nq * RADIX
    res = pl.BlockSpec((1, RADIX, tm // RADIX, d), lambda bi, i: (bi, 0, i, 0))
    return pl.pallas_call(
        _ffn_final_kernel,
        grid=(b, s // tm),
        in_specs=[res, res,
                  pl.BlockSpec((1, N_MOD, d), lambda bi, i: (bi, 0, 0)),
                  _const_spec((1, d)), _const_spec(wgu.shape), _const_spec(wd.shape), _const_spec((1, d))],
        out_specs=pl.BlockSpec((1, tm, d), lambda bi, i: (bi, i, 0)),
        out_shape=jax.ShapeDtypeStruct((b, s, d), F32),
        scratch_shapes=[pltpu.VMEM((d // LANES, tm, LANES), F32)],
        compiler_params=pltpu.CompilerParams(
            dimension_semantics=("parallel", "arbitrary"),
            vmem_limit_bytes=_vmem_limit(56 * 1024 * 1024)),
        name="ffn",
    )(xr, yr, mod, gffn, wgu, wd, gfin)


_QUARTER = ((1, 0), (0, 1), (-1, 0), (0, -1))


def _fproj_kernel(ng, *refs):
    x_refs = refs[:RADIX * ng]
    mod_ref, g_ref, w_ref, u_ref = refs[RADIX * ng:]
    d = x_refs[0].shape[2]
    ta = x_refs[0].shape[1]
    shift = mod_ref[0, 0:1, :]
    gain = g_ref[...] * (1.0 + mod_ref[0, 1:2, :])
    for t in range(ng):
        h = jnp.concatenate([(_rms(xr[0], gain) + shift).astype(BF16)
                             for xr in x_refs[RADIX * t:RADIX * (t + 1)]], axis=0)
        z = jnp.dot(h, w_ref[...], preferred_element_type=F32)
        zc = [z[m * ta:(m + 1) * ta, 0:d] for m in range(RADIX)]
        zs = [z[m * ta:(m + 1) * ta, d:2 * d] for m in range(RADIX)]
        for r in range(RADIX):
            ua = None
            ub = None
            for m in range(RADIX):
                cp, sp = _QUARTER[(r * m) % 4]
                ta_, sa = (zc[m], cp) if cp else (zs[m], -sp)
                tb_, sb = (zs[m], cp) if cp else (zc[m], sp)
                ua = sa * ta_ if ua is None else (ua + ta_ if sa > 0 else ua - ta_)
                ub = sb * tb_ if ub is None else (ub + tb_ if sb > 0 else ub - tb_)
            u_ref[0, r, 0, t * ta:(t + 1) * ta, :] = ua.astype(BF16)
            u_ref[0, r, 1, t * ta:(t + 1) * ta, :] = ub.astype(BF16)


def _fproj(x, mod, g, wcs, ta):
    b, s, d = x.shape
    nq = s // RADIX
    nqq = nq // RADIX
    na = nqq // ta
    nrb = RADIX * na
    ng = max(n for n in (1, 2, 4) if nrb % n == 0)

    def x_spec(t, m):
        def index(bi, j):
            rb = j * ng + t
            return (bi, (rb // na) * (RADIX * na) + m * na + rb % na, 0)
        return pl.BlockSpec((1, ta, d), index)

    return pl.pallas_call(
        functools.partial(_fproj_kernel, ng),
        grid=(b, nrb // ng),
        in_specs=[x_spec(t, m) for t in range(ng) for m in range(RADIX)]
        + [pl.BlockSpec((1, N_MOD, d), lambda bi, j: (bi, 0, 0)),
           _const_spec((1, d)), _const_spec(wcs.shape)],
        out_specs=pl.BlockSpec((1, RADIX, 2, ng * ta, d), lambda bi, j: (bi, 0, 0, j, 0)),
        out_shape=jax.ShapeDtypeStruct((b, RADIX, 2, nq, d), BF16),
        compiler_params=pltpu.CompilerParams(
            dimension_semantics=("parallel", "arbitrary"),
            vmem_limit_bytes=_vmem_limit(48 * 1024 * 1024)),
        name="fproj",
    )(*([x] * (RADIX * ng)), mod, g, wcs)


def _fseq_kernel(u_ref, t_ref, y_ref):
    nq, d = u_ref.shape[3], u_ref.shape[4]
    u = u_ref[0, 0].reshape(2 * nq, d)
    y_ref[0] = jnp.dot(t_ref[0], u, preferred_element_type=F32).astype(BF16)


def _seq_dft_tables(n):
    nq = n // RADIX
    nqq = nq // RADIX
    u = np.arange(nq)
    q = (RADIX * (u % nqq) + u // nqq)[None, None, :]
    p = np.arange(nq)[None, :, None]
    r = np.arange(RADIX)[:, None, None]
    ang = 2.0 * np.pi * (((RADIX * p + r) * q) % n) / n
    t = np.concatenate([np.cos(ang), -np.sin(ang)], axis=2) / np.sqrt(n)
    return jnp.asarray(t, F32)


def _fseq(u, t, tmr):
    b, _, _, nq, d = u.shape
    nblk = nq // tmr
    return pl.pallas_call(
        _fseq_kernel,
        grid=(b, RADIX, nblk),
        in_specs=[pl.BlockSpec((1, 1, 2, nq, d), lambda bi, r, i: (bi, r, 0, 0, 0)),
                  pl.BlockSpec((1, tmr, 2 * nq), lambda bi, r, i: (r, i, 0))],
        out_specs=pl.BlockSpec((1, tmr, d), lambda bi, r, i: (bi, r * nblk + i, 0)),
        out_shape=jax.ShapeDtypeStruct((b, RADIX * nq, d), BF16),
        compiler_params=pltpu.CompilerParams(
            dimension_semantics=("parallel", "arbitrary", "arbitrary"),
            vmem_limit_bytes=_vmem_limit(40 * 1024 * 1024)),
        name="fseq",
    )(u, t)


def _rope_lane_tables(n_ctx, n_tok):
    rows_count = n_tok // GRID_W
    row = np.repeat(np.arange(rows_count), GRID_W).astype(np.float32)
    col = np.tile(np.arange(GRID_W), rows_count).astype(np.float32)
    inv = np.float32(ROPE_THETA) ** (-np.arange(ROPE_PAIRS, dtype=np.float32) / np.float32(ROPE_PAIRS))
    lane = np.arange(LANES)
    dd = lane % HEAD_DIM
    axis = dd // (2 * ROPE_PAIRS)
    half = (dd % (2 * ROPE_PAIRS)) // ROPE_PAIRS
    pair = dd % ROPE_PAIRS
    pos = np.where((axis == 0)[None, :], row[:, None], col[:, None])
    ang = (pos * inv.astype(np.float32)[pair][None, :]).astype(np.float32).astype(np.float64)
    cosv = np.cos(ang)
    sinv = np.sin(ang)
    first = (half == 0)[None, :]
    s1 = np.where(first, -sinv, 0.0)
    s2 = np.where(first, 0.0, sinv)
    pad1 = np.ones((n_ctx, LANES))
    pad0 = np.zeros((n_ctx, LANES))
    return tuple(jnp.asarray(np.concatenate([tab, pad], 0), F32)
                 for pad, tab in ((pad1, cosv), (pad0, s1), (pad0, s2)))


def _permute_w_in(w_in):
    d = w_in.shape[0]
    off_ak = A_W
    off_av = 2 * A_W
    off_bq = 3 * A_W
    off_bk = off_bq + BQ_W
    off_bv = off_bk + BKV_W
    grp = GQA_HEADS // GQA_KV_HEADS
    bq = w_in[:, off_bq:off_bk].reshape(d, GQA_KV_HEADS, grp, HEAD_DIM).swapaxes(1, 2).reshape(d, BQ_W)
    return jnp.concatenate([w_in[:, 0:A_W], bq, w_in[:, off_ak:off_av], w_in[:, off_bk:off_bv],
                            w_in[:, off_av:off_bq], w_in[:, off_bv:off_bv + BKV_W]], axis=1)


def _permute_w_out(w_out):
    d = w_out.shape[1]
    grp = GQA_HEADS // GQA_KV_HEADS
    wb = w_out[A_W:].reshape(GQA_KV_HEADS, grp, HEAD_DIM, d).swapaxes(0, 1).reshape(BQ_W, d)
    return jnp.concatenate([w_out[:A_W], wb], axis=0)


def kernel(x, c, ctx, c_ctx, l0_ada_w, l0_ada_b, l0_norm_mix, l0_w_in, l0_lambda_q1, l0_lambda_k1, l0_lambda_q2, l0_lambda_k2, l0_subln, l0_q_norm, l0_k_norm, l0_w_out, l0_norm_ffn, l0_w_gate_up, l0_w_down, l1_ada_w, l1_ada_b, l1_norm_mix, l1_w_out, l1_norm_ffn, l1_w_gate_up, l1_w_down, final_norm):
    b, s, d = x.shape
    n_ctx = ctx.shape[1]
    assert d == D_MODEL and s % GRID_W == 0

    tt = min(256, n_ctx, s)
    tq = min(128, tt)
    nsub = max(n for n in (1, 2, 4) if s % (n * tq) == 0)
    tk = min(256, n_ctx)
    tm = min(1024, s)
    tmr = min(512, s // RADIX)
    ta = min(128, s // RADIX ** 2)
    assert n_ctx % tt == 0 and s % tt == 0 and tt % tq == 0 and (n_ctx + s) % tk == 0
    assert s % tm == 0 and (s // RADIX) % tmr == 0 and (s // RADIX ** 2) % ta == 0

    n_rows = -(-(b + 1) // SUBLANES) * SUBLANES
    cond = jnp.concatenate([c, c_ctx[None, :], jnp.zeros((n_rows - b - 1, d), F32)], axis=0)
    m0, m1, wcs = _ada_fold(cond, l0_ada_w, l0_ada_b, l1_ada_w, l1_ada_b, l1_w_out)
    mod0 = m0[:b].reshape(b, N_MOD, d)
    mod0c = m0[b:b + 1].reshape(1, N_MOD, d)
    mod1 = m1[:b].reshape(b, N_MOD, d)

    w_perm = _permute_w_in(l0_w_in)
    cos_t, s1_t, s2_t = _rope_lane_tables(n_ctx, s)
    qg = jnp.tile(l0_q_norm, LANES // HEAD_DIM).reshape(1, LANES)
    kg = jnp.tile(l0_k_norm, LANES // HEAD_DIM).reshape(1, LANES)
    li = np.arange(LANES)
    ones_bd = jnp.asarray((li[:, None] // HEAD_DIM) == (li[None, :] // HEAD_DIM), BF16)
    qt, k, vt = _proj(x, ctx, mod0, mod0c, l0_norm_mix.reshape(1, d), w_perm.astype(BF16),
                      cos_t, s1_t, s2_t, qg, kg, ones_bd, tt, tq)
    lam_vecs = jnp.stack([l0_lambda_q1, l0_lambda_k1, l0_lambda_q2, l0_lambda_k2]).astype(F32)
    o, (w_out0, w_gu0, w_dn0, w_gu1, w_dn1) = _attn(
        qt, k, vt, lam_vecs, l0_subln.reshape(1, LANES), s, tq, tk, nsub,
        [_permute_w_out(l0_w_out), l0_w_gate_up, l0_w_down, l1_w_gate_up, l1_w_down])
    xr = _mixffn(o, x, mod0, w_out0, l0_norm_ffn.reshape(1, d), w_gu0, w_dn0, tm)
    xr = xr.reshape(b, s, d)

    u = _fproj(xr, mod1, l1_norm_mix.reshape(1, d), wcs, ta)
    yr = _fseq(u, _seq_dft_tables(s).astype(BF16), tmr)
    res4 = (b, RADIX, s // RADIX, d)
    return _ffn_final(xr.reshape(res4), yr.reshape(res4), mod1, l1_norm_ffn.reshape(1, d),
                      w_gu1, w_dn1, final_norm.reshape(1, d), tm)
```

```python
import functools
import math

import numpy as np
import jax
import jax.numpy as jnp
from jax import lax
from jax.experimental import pallas as pl
from jax.experimental.pallas import tpu as pltpu

F32 = jnp.float32
BF16 = jnp.bfloat16

LANES = 128
SUBLANES = 8
BF16_SUBLANES = 16
V7X_VMEM_BYTES = 64 * 1024 * 1024

D_MODEL = 1024
GRID_W = 64
DIFF_HEADS = 4
DIFF_HEAD_DIM = 64
GQA_HEADS = 8
GQA_KV_HEADS = 2
GQA_HEAD_DIM = 64
ROPE_THETA = 10000.0
ROPE_PAIRS = GQA_HEAD_DIM // 4
FOURIER_GROUPS = 8
FOURIER_GROUP_DIM = D_MODEL // FOURIER_GROUPS
EPS = 1e-6
N_MOD = 6
HEAD_DIM = 64
A_W = DIFF_HEADS * 2 * DIFF_HEAD_DIM
BQ_W = GQA_HEADS * GQA_HEAD_DIM
BKV_W = GQA_KV_HEADS * GQA_HEAD_DIM
Q_W = A_W + BQ_W
K_W = A_W + BKV_W
N_QG = Q_W // LANES
N_KG = K_W // LANES
LAM_INIT_L0 = 0.8 - 0.6 * math.exp(-0.3 * 0)
Q_SCALE = HEAD_DIM ** -0.5 * math.log2(math.e)
RADIX = 4
ACC_ROWS = 32
FFN_SPLIT = 4
VT_ROWS = LANES + BF16_SUBLANES


def _vmem_limit(nbytes):
    return int(min(nbytes, V7X_VMEM_BYTES - 4 * 1024 * 1024))


def _rms(x, g):
    ms = jnp.mean(x * x, axis=-1, keepdims=True)
    return x * lax.rsqrt(ms + EPS) * g


def _const_spec(shape):
    nd = len(shape)
    return pl.BlockSpec(shape, lambda *_: (0,) * nd, pipeline_mode=pl.Buffered(1))


def _fold_kernel(cd_ref, sd_ref, w_ref, o_ref):
    w = w_ref[...]
    d = w.shape[1]
    w_hi = w.astype(BF16)
    w_lo = (w - w_hi.astype(F32)).astype(BF16)

    def dot3(m):
        m_hi = m.astype(BF16)
        m_lo = (m - m_hi.astype(F32)).astype(BF16)
        return (jnp.dot(m_hi, w_hi, preferred_element_type=F32) + jnp.dot(m_hi, w_lo, preferred_element_type=F32)
                + jnp.dot(m_lo, w_hi, preferred_element_type=F32))

    o_ref[:, 0:d] = dot3(cd_ref[...]).astype(BF16)
    o_ref[:, d:2 * d] = dot3(sd_ref[...]).astype(BF16)


def _ada_kernel(c_ref, w0_ref, b0_ref, w1_ref, b1_ref, cd_ref, sd_ref, wo_ref, o0_ref, o1_ref, wcs_ref):
    @pl.when(pl.program_id(0) == 0)
    def _():
        o0_ref[...] = jnp.broadcast_to(b0_ref[...], o0_ref.shape)
        o1_ref[...] = jnp.broadcast_to(b1_ref[...], o1_ref.shape)

    cv = c_ref[...]
    a = (cv * jax.nn.sigmoid(cv)).astype(BF16)
    o0_ref[...] += jnp.dot(a, w0_ref[...].astype(BF16), preferred_element_type=F32)
    o1_ref[...] += jnp.dot(a, w1_ref[...].astype(BF16), preferred_element_type=F32)
    _fold_kernel(cd_ref, sd_ref, wo_ref, wcs_ref)


def _ada_fold(cond, w0, b0, w1, b1, w_out):
    r, d = cond.shape
    n = w0.shape[1]
    gd = FOURIER_GROUP_DIM
    ngr = w_out.shape[0] // gd
    bk = d // ngr
    idx = np.arange(gd)
    ang = 2.0 * np.pi * ((idx[:, None] * idx[None, :]) % gd) / gd
    cd = jnp.asarray(np.cos(ang) / np.sqrt(gd), F32)
    sd = jnp.asarray(np.sin(ang) / np.sqrt(gd), F32)
    return pl.pallas_call(
        _ada_kernel,
        grid=(ngr,),
        in_specs=[pl.BlockSpec((r, bk), lambda j: (0, j)),
                  pl.BlockSpec((bk, n), lambda j: (j, 0)),
                  pl.BlockSpec((1, n), lambda j: (0, 0)),
                  pl.BlockSpec((bk, n), lambda j: (j, 0)),
                  pl.BlockSpec((1, n), lambda j: (0, 0)),
                  pl.BlockSpec((gd, gd), lambda j: (0, 0)),
                  pl.BlockSpec((gd, gd), lambda j: (0, 0)),
                  pl.BlockSpec((gd, w_out.shape[1]), lambda j: (j, 0))],
        out_specs=[pl.BlockSpec((r, n), lambda j: (0, 0)),
                   pl.BlockSpec((r, n), lambda j: (0, 0)),
                   pl.BlockSpec((gd, 2 * w_out.shape[1]), lambda j: (j, 0))],
        out_shape=[jax.ShapeDtypeStruct((r, n), F32), jax.ShapeDtypeStruct((r, n), F32),
                   jax.ShapeDtypeStruct((w_out.shape[0], 2 * w_out.shape[1]), BF16)],
        compiler_params=pltpu.CompilerParams(dimension_semantics=("arbitrary",)),
        name="ada",
    )(cond, w0, b0.reshape(1, n), w1, b1.reshape(1, n), cd, sd, w_out)


def _proj_kernel(nlat, nct, tq, ng, *refs):
    x_refs, ctx_refs = refs[:ng], refs[ng:2 * ng]
    (mod_ref, modc_ref, g_ref, w_ref, cos_ref, s1_ref, s2_ref,
     qg_ref, kg_ref, ones_ref, qt_ref, k_ref, vt_ref) = refs[2 * ng:]
    j = pl.program_id(1)
    gain_x = g_ref[...] * (1.0 + mod_ref[0, 1:2, :])
    gain_c = g_ref[...] * (1.0 + modc_ref[0, 1:2, :])
    hs = []
    for t in range(ng):
        if ng - 1 - t >= nct:
            xin, gain, shift = x_refs[t][0], gain_x, mod_ref[0, 0:1, :]
        else:
            is_ctx = j * ng + t >= nlat
            xin = jnp.where(is_ctx, ctx_refs[t][0], x_refs[t][0])
            gain = jnp.where(is_ctx, gain_c, gain_x)
            shift = jnp.where(is_ctx, modc_ref[0, 0:1, :], mod_ref[0, 0:1, :])
        ms = jnp.mean(xin * xin, axis=-1, keepdims=True)
        hs.append((xin * lax.rsqrt(ms + EPS) * gain + shift).astype(BF16))
    h = jnp.concatenate(hs, axis=0)

    tt = h.shape[0]
    cosv = cos_ref[...]
    s1 = s1_ref[...]
    s2 = s2_ref[...]
    row = lax.broadcasted_iota(jnp.int32, (LANES, tt), 0)
    lo_rows = row < HEAD_DIM

    def rope(xs):
        return xs * cosv + pltpu.roll(xs, LANES - ROPE_PAIRS, 1) * s1 + pltpu.roll(xs, ROPE_PAIRS, 1) * s2

    def headnorm(xs, gain):
        sq = xs * xs
        hi = sq.astype(BF16)
        lo = (sq - hi.astype(F32)).astype(BF16)
        ss = (jnp.dot(hi, ones_ref[...], preferred_element_type=F32)
              + jnp.dot(lo, ones_ref[...], preferred_element_type=F32))
        return xs * lax.rsqrt(ss * (1.0 / HEAD_DIM) + EPS) * gain

    def finish_slab(idx, xs):
        if idx < N_QG:
            s = idx
            if s >= A_W // LANES:
                xs = headnorm(xs, qg_ref[...])
            qst = (rope(xs) * Q_SCALE).T
            q_lo = jnp.where(lo_rows, qst, 0.0)
            q_hi = qst - q_lo
            for a in range(tt // tq):
                qt_ref[0, s, :, 2 * a * tq:(2 * a + 1) * tq] = q_lo[:, a * tq:(a + 1) * tq].astype(BF16)
                qt_ref[0, s, :, (2 * a + 1) * tq:(2 * a + 2) * tq] = q_hi[:, a * tq:(a + 1) * tq].astype(BF16)
        elif idx < N_QG + N_KG:
            s = idx - N_QG
            if s >= A_W // LANES:
                xs = headnorm(xs, kg_ref[...])
            k_ref[0, s] = rope(xs).astype(BF16)
        else:
            s = idx - N_QG - N_KG
            vt_ref[0, s, 0:LANES, :] = xs.T.astype(BF16)
            vt_ref[0, s, LANES:VT_ROWS, :] = jnp.ones((VT_ROWS - LANES, tt), BF16)

    gw = 2 * LANES
    n_groups = w_ref.shape[1] // gw
    pending = None
    for gi in range(n_groups + 1):
        z = None
        if gi < n_groups:
            z = jnp.dot(h, w_ref[:, gi * gw:(gi + 1) * gw], preferred_element_type=F32)
        if pending is not None:
            pg, pz = pending
            for half in range(gw // LANES):
                finish_slab(pg * (gw // LANES) + half, pz[:, half * LANES:(half + 1) * LANES])
        pending = (gi, z)


def _proj(x, ctx, mod, modc, g, w_perm, cos_t, s1_t, s2_t, qg, kg, ones_bd, tt, tq):
    b, s, d = x.shape
    nctx = ctx.shape[1]
    nct = nctx // tt
    nk = nctx + s
    nj = nk // tt
    ng = max(gsz for gsz in (1, 2, 3) if nj % gsz == 0)
    rows = ng * tt
    n_in = w_perm.shape[1]
    nlat = nj - nct
    kern = functools.partial(_proj_kernel, nlat, nct, tq, ng)

    def x_spec(t):
        return pl.BlockSpec((1, tt, d), lambda bi, j: (bi, jnp.minimum(j * ng + t, nlat - 1), 0))

    def ctx_spec(t):
        return pl.BlockSpec((1, tt, d), lambda bi, j: (bi, jnp.maximum(j * ng + t - nlat, 0), 0))

    tab_spec = pl.BlockSpec((rows, LANES), lambda bi, j: (j, 0))
    return pl.pallas_call(
        kern,
        grid=(b, nj // ng),
        in_specs=[x_spec(t) for t in range(ng)] + [ctx_spec(t) for t in range(ng)]
        + [pl.BlockSpec((1, N_MOD, d), lambda bi, j: (bi, 0, 0)),
           _const_spec((1, N_MOD, d)),
           _const_spec((1, d)),
           _const_spec((d, n_in)),
           tab_spec, tab_spec, tab_spec,
           _const_spec((1, LANES)),
           _const_spec((1, LANES)),
           _const_spec((LANES, LANES))],
        out_specs=[pl.BlockSpec((1, N_QG, LANES, 2 * rows), lambda bi, j: (bi, 0, 0, j)),
                   pl.BlockSpec((1, N_KG, rows, LANES), lambda bi, j: (bi, 0, j, 0)),
                   pl.BlockSpec((1, N_KG, VT_ROWS, rows), lambda bi, j: (bi, 0, 0, j))],
        out_shape=[jax.ShapeDtypeStruct((b, N_QG, LANES, 2 * nk), BF16),
                   jax.ShapeDtypeStruct((b, N_KG, nk, LANES), BF16),
                   jax.ShapeDtypeStruct((b, N_KG, VT_ROWS, nk), BF16)],
        compiler_params=pltpu.CompilerParams(
            dimension_semantics=("parallel", "arbitrary"),
            vmem_limit_bytes=_vmem_limit(56 * 1024 * 1024)),
        name="proj",
    )(*([x] * ng), *([ctx] * ng), mod, modc, g, w_perm, cos_t, s1_t, s2_t, qg, kg, ones_bd)


def _attn_kernel(tq, tk, nsub, ncast, qt_ref, k_ref, vt_ref, lam_ref, subln_ref, *rest):
    w_refs, o_ref, wb_refs, bufs = rest[:ncast], rest[ncast], rest[ncast + 1:2 * ncast + 1], rest[2 * ncast + 1:]
    for w_ref, wb_ref in zip(w_refs, wb_refs):
        wb_ref[...] = w_ref[...].astype(BF16)
    nk = k_ref.shape[2]
    nc = nk // tk
    tq2 = 2 * tq
    row = lax.broadcasted_iota(jnp.int32, (LANES, tq), 0)
    lo_rows = row < HEAD_DIM
    lv = lam_ref[...]
    lam = (jnp.exp(jnp.sum(lv[0:1] * lv[1:2], axis=1, keepdims=True))
           - jnp.exp(jnp.sum(lv[2:3] * lv[3:4], axis=1, keepdims=True)) + LAM_INIT_L0)

    n_a = A_W // LANES

    s_bufs, p_bufs = bufs[:3], bufs[3:]

    def item(w):
        return divmod(w, N_QG)

    def score_chunk(w, c, m8):
        qi, u = item(w)
        kv = min(u, n_a)
        s_c = jnp.dot(k_ref[0, kv, c * tk:(c + 1) * tk, :], qt_ref[0, u, :, qi * tq2:(qi + 1) * tq2],
                      preferred_element_type=F32)
        s_bufs[w % 3][c * tk:(c + 1) * tk, :] = s_c
        mc = jnp.max(s_c.reshape(tk // ACC_ROWS, ACC_ROWS, tq2), axis=0)
        return mc if m8 is None else jnp.maximum(m8, mc)

    def exp_chunk(w, c, m):
        p_bufs[w % 2][c * tk:(c + 1) * tk, :] = jnp.exp2((s_bufs[w % 3][c * tk:(c + 1) * tk, :] - m).astype(BF16))

    def value_chunk(w, c, acc):
        kv = min(item(w)[1], n_a)
        pv = jnp.dot(vt_ref[0, kv, :, c * tk:(c + 1) * tk], p_bufs[w % 2][c * tk:(c + 1) * tk, :],
                     preferred_element_type=F32)
        return pv if acc is None else acc + pv

    def finish(w, acc):
        qi, u = item(w)
        ot = acc[0:LANES] * (1.0 / acc[LANES:LANES + 1])
        o1, o2 = ot[:, :tq], ot[:, tq:]
        if u < n_a:
            ot = o1 - lam * o2
            ot = ot * lax.rsqrt(jnp.mean(ot * ot, axis=0, keepdims=True) + EPS)
            o_ref[0, u, qi * tq:(qi + 1) * tq, :] = (ot.T * (subln_ref[...] * (1.0 - LAM_INIT_L0))).astype(BF16)
        else:
            o_ref[0, u, qi * tq:(qi + 1) * tq, :] = jnp.where(lo_rows, o1, o2).T.astype(BF16)

    n_items = nsub * N_QG
    m8s = {}
    ms = {}
    accs = {}
    for t in range(n_items + 2):
        if 0 <= t - 1 < n_items:
            ms[t - 1] = jnp.max(m8s.pop(t - 1), axis=0, keepdims=True)
        for c in range(nc):
            pace = m8s.get(t) if t < n_items else accs.get(t - 2)
            pace = None if pace is None else pace[0:1]
            if 0 <= t - 2 < n_items:
                accs[t - 2] = value_chunk(t - 2, c, accs.get(t - 2))
            if t < n_items:
                m8s[t] = score_chunk(t, c, m8s.get(t))
            if 0 <= t - 1 < n_items:
                m = ms[t - 1]
                if pace is not None:
                    m = m + jnp.minimum(jnp.abs(pace), 0.0)
                exp_chunk(t - 1, c, m)
        if 0 <= t - 2 < n_items:
            finish(t - 2, accs.pop(t - 2))


def _cast_rows(n_rows, n_steps):
    best = None
    for r in range(BF16_SUBLANES, n_rows + 1, BF16_SUBLANES):
        if n_rows % r == 0 and n_steps % (n_rows // r) == 0:
            best = r if best is None else min(best, r)
    return best


def _attn(qt, k, vt, lam_vecs, subln, s, tq, tk, nsub, weights):
    b = qt.shape[0]
    nk = k.shape[2]
    tqs = tq * nsub
    ni = s // tqs
    plans = [_cast_rows(w.shape[0], b * ni) for w in weights]
    riders = [(w, r) for w, r in zip(weights, plans) if r is not None]
    kern = functools.partial(_attn_kernel, tq, tk, nsub, len(riders))

    def w_spec(w, r):
        rep = (b * ni) // (w.shape[0] // r)
        return pl.BlockSpec((r, w.shape[1]), lambda bi, i: ((bi * ni + i) // rep, 0))

    outs = pl.pallas_call(
        kern,
        grid=(b, ni),
        in_specs=[pl.BlockSpec((1, N_QG, LANES, 2 * tqs), lambda bi, i: (bi, 0, 0, i)),
                  pl.BlockSpec((1, N_KG, nk, LANES), lambda bi, i: (bi, 0, 0, 0)),
                  pl.BlockSpec((1, N_KG, VT_ROWS, nk), lambda bi, i: (bi, 0, 0, 0)),
                  _const_spec((4, HEAD_DIM)),
                  _const_spec((1, LANES))] + [w_spec(w, r) for w, r in riders],
        out_specs=[pl.BlockSpec((1, N_QG, tqs, LANES), lambda bi, i: (bi, 0, i, 0))]
        + [w_spec(w, r) for w, r in riders],
        out_shape=[jax.ShapeDtypeStruct((b, N_QG, s, LANES), BF16)]
        + [jax.ShapeDtypeStruct(w.shape, BF16) for w, _ in riders],
        scratch_shapes=[pltpu.VMEM((nk, 2 * tq), F32)] * 3 + [pltpu.VMEM((nk, 2 * tq), BF16)] * 2,
        compiler_params=pltpu.CompilerParams(
            dimension_semantics=("arbitrary", "arbitrary"),
            vmem_limit_bytes=_vmem_limit(48 * 1024 * 1024)),
        name="attn",
    )(qt, k, vt, lam_vecs, subln, *[w for w, _ in riders])
    cast = iter(outs[1:])
    return outs[0], [next(cast) if r is not None else w.astype(BF16) for w, r in zip(weights, plans)]


def _swiglu(x_all, mod_ref, gffn_ref, wgu_ref, wd_ref):
    shift = mod_ref[0, 3:4, :]
    gain = gffn_ref[...] * (1.0 + mod_ref[0, 4:5, :])
    gate = mod_ref[0, 5:6, :]
    n = x_all.shape[0]
    nh = n // FFN_SPLIT
    outs = []
    for r0 in range(0, n, nh):
        x1 = x_all[r0:r0 + nh]
        h = _rms(x1, gain) + shift
        gu = jnp.dot(h.astype(BF16), wgu_ref[...], preferred_element_type=F32)
        dff = gu.shape[1] // 2
        g = gu[:, :dff]
        u = gu[:, dff:]
        a = (g * jax.nn.sigmoid(g)) * u
        dn = jnp.dot(a.astype(BF16), wd_ref[...], preferred_element_type=F32)
        outs.append(x1 + gate * dn)
    return outs[0] if len(outs) == 1 else jnp.concatenate(outs, axis=0)


def _mixffn_kernel(o_ref, x_ref, mod_ref, wout_ref, gffn_ref, wgu_ref, wd_ref, out_ref, res_scr):
    o = jnp.concatenate([o_ref[0, u] for u in range(N_QG)], axis=1)
    y = jnp.dot(o, wout_ref[...], preferred_element_type=F32)
    x1 = x_ref[0] + mod_ref[0, 2:3, :] * y
    res = _swiglu(x1, mod_ref, gffn_ref, wgu_ref, wd_ref)
    n = res_scr.shape[1] // RADIX
    for sl in range(res_scr.shape[0]):
        res_scr[sl] = res[:, sl * LANES:(sl + 1) * LANES]
        for r in range(RADIX):
            out_ref[0, r, :, sl * LANES:(sl + 1) * LANES] = res_scr[sl, pl.ds(r, n, stride=RADIX), :]


def _ffn_final_kernel(x_ref, y_ref, mod_ref, gffn_ref, wgu_ref, wd_ref, gfin_ref, out_ref, x_scr):
    n = x_scr.shape[1] // RADIX
    gate = mod_ref[0, 2:3, :]
    for sl in range(x_scr.shape[0]):
        cols = slice(sl * LANES, (sl + 1) * LANES)
        for r in range(RADIX):
            x_scr[sl, pl.ds(r, n, stride=RADIX), :] = (x_ref[0, r, :, cols]
                                                       + gate[:, cols] * y_ref[0, r, :, cols].astype(F32))
    x1 = jnp.concatenate([x_scr[sl] for sl in range(x_scr.shape[0])], axis=1)
    x2 = _swiglu(x1, mod_ref, gffn_ref, wgu_ref, wd_ref)
    out_ref[0] = _rms(x2, gfin_ref[...])


def _mixffn(o, x, mod, wout, gffn, wgu, wd, tm):
    b, s, d = x.shape
    tok = lambda bi, i: (bi, i, 0)
    return pl.pallas_call(
        _mixffn_kernel,
        grid=(b, s // tm),
        in_specs=[pl.BlockSpec((1, N_QG, tm, LANES), lambda bi, i: (bi, 0, i, 0)),
                  pl.BlockSpec((1, tm, d), tok),
                  pl.BlockSpec((1, N_MOD, d), lambda bi, i: (bi, 0, 0)),
                  _const_spec(wout.shape), _const_spec((1, d)), _const_spec(wgu.shape), _const_spec(wd.shape)],
        out_specs=pl.BlockSpec((1, RADIX, tm // RADIX, d), lambda bi, i: (bi, 0, i, 0)),
        out_shape=jax.ShapeDtypeStruct((b, RADIX, s // RADIX, d), F32),
        scratch_shapes=[pltpu.VMEM((d // LANES, tm, LANES), F32)],
        compiler_params=pltpu.CompilerParams(
            dimension_semantics=("parallel", "arbitrary"),
            vmem_limit_bytes=_vmem_limit(56 * 1024 * 1024)),
        name="mixffn",
    )(o, x, mod, wout, gffn, wgu, wd)


def _ffn_final(xr, yr, mod, gffn, wgu, wd, gfin, tm):
    b, _, nq, d = xr.shape
    s = nq * RADIX
    res = pl.BlockSpec((1, RADIX, tm // RADIX, d), lambda bi, i: (bi, 0, i, 0))
    return pl.pallas_call(
        _ffn_final_kernel,
        grid=(b, s // tm),
        in_specs=[res, res,
                  pl.BlockSpec((1, N_MOD, d), lambda bi, i: (bi, 0, 0)),
                  _const_spec((1, d)), _const_spec(wgu.shape), _const_spec(wd.shape), _const_spec((1, d))],
        out_specs=pl.BlockSpec((1, tm, d), lambda bi, i: (bi, i, 0)),
        out_shape=jax.ShapeDtypeStruct((b, s, d), F32),
        scratch_shapes=[pltpu.VMEM((d // LANES, tm, LANES), F32)],
        compiler_params=pltpu.CompilerParams(
            dimension_semantics=("parallel", "arbitrary"),
            vmem_limit_bytes=_vmem_limit(56 * 1024 * 1024)),
        name="ffn",
    )(xr, yr, mod, gffn, wgu, wd, gfin)


_QUARTER = ((1, 0), (0, 1), (-1, 0), (0, -1))


def _fproj_kernel(ng, *refs):
    x_refs = refs[:RADIX * ng]
    mod_ref, g_ref, w_ref, u_ref = refs[RADIX * ng:]
    d = x_refs[0].shape[2]
    ta = x_refs[0].shape[1]
    shift = mod_ref[0, 0:1, :]
    gain = g_ref[...] * (1.0 + mod_ref[0, 1:2, :])
    for t in range(ng):
        h = jnp.concatenate([(_rms(xr[0], gain) + shift).astype(BF16)
                             for xr in x_refs[RADIX * t:RADIX * (t + 1)]], axis=0)
        z = jnp.dot(h, w_ref[...], preferred_element_type=F32)
        zc = [z[m * ta:(m + 1) * ta, 0:d] for m in range(RADIX)]
        zs = [z[m * ta:(m + 1) * ta, d:2 * d] for m in range(RADIX)]
        for r in range(RADIX):
            ua = None
            ub = None
            for m in range(RADIX):
                cp, sp = _QUARTER[(r * m) % 4]
                ta_, sa = (zc[m], cp) if cp else (zs[m], -sp)
                tb_, sb = (zs[m], cp) if cp else (zc[m], sp)
                ua = sa * ta_ if ua is None else (ua + ta_ if sa > 0 else ua - ta_)
                ub = sb * tb_ if ub is None else (ub + tb_ if sb > 0 else ub - tb_)
            u_ref[0, r, 0, t * ta:(t + 1) * ta, :] = ua.astype(BF16)
            u_ref[0, r, 1, t * ta:(t + 1) * ta, :] = ub.astype(BF16)


def _fproj(x, mod, g, wcs, ta):
    b, s, d = x.shape
    nq = s // RADIX
    nqq = nq // RADIX
    na = nqq // ta
    nrb = RADIX * na
    ng = max(n for n in (1, 2, 4) if nrb % n == 0)

    def x_spec(t, m):
        def index(bi, j):
            rb = j * ng + t
            return (bi, (rb // na) * (RADIX * na) + m * na + rb % na, 0)
        return pl.BlockSpec((1, ta, d), index)

    return pl.pallas_call(
        functools.partial(_fproj_kernel, ng),
        grid=(b, nrb // ng),
        in_specs=[x_spec(t, m) for t in range(ng) for m in range(RADIX)]
        + [pl.BlockSpec((1, N_MOD, d), lambda bi, j: (bi, 0, 0)),
           _const_spec((1, d)), _const_spec(wcs.shape)],
        out_specs=pl.BlockSpec((1, RADIX, 2, ng * ta, d), lambda bi, j: (bi, 0, 0, j, 0)),
        out_shape=jax.ShapeDtypeStruct((b, RADIX, 2, nq, d), BF16),
        compiler_params=pltpu.CompilerParams(
            dimension_semantics=("parallel", "arbitrary"),
            vmem_limit_bytes=_vmem_limit(48 * 1024 * 1024)),
        name="fproj",
    )(*([x] * (RADIX * ng)), mod, g, wcs)


def _fseq_kernel(u_ref, t_ref, y_ref):
    nq, d = u_ref.shape[3], u_ref.shape[4]
    tmr = y_ref.shape[1]
    u = u_ref[0, 0].reshape(2 * nq, d)
    t = t_ref[pl.program_id(1), pl.ds(pl.multiple_of(pl.program_id(2) * tmr, tmr), tmr), :]
    y_ref[0] = jnp.dot(t, u, preferred_element_type=F32).astype(BF16)


def _seq_dft_tables(n):
    nq = n // RADIX
    nqq = nq // RADIX
    u = np.arange(nq)
    q = (RADIX * (u % nqq) + u // nqq)[None, None, :]
    p = np.arange(nq)[None, :, None]
    r = np.arange(RADIX)[:, None, None]
    ang = 2.0 * np.pi * (((RADIX * p + r) * q) % n) / n
    t = np.concatenate([np.cos(ang), -np.sin(ang)], axis=2) / np.sqrt(n)
    return jnp.asarray(t, F32)


def _fseq(u, t, tmr):
    b, _, _, nq, d = u.shape
    nblk = nq // tmr
    return pl.pallas_call(
        _fseq_kernel,
        grid=(b, RADIX, nblk),
        in_specs=[pl.BlockSpec((1, 1, 2, nq, d), lambda bi, r, i: (bi, r, 0, 0, 0)),
                  _const_spec(t.shape)],
        out_specs=pl.BlockSpec((1, tmr, d), lambda bi, r, i: (bi, r * nblk + i, 0)),
        out_shape=jax.ShapeDtypeStruct((b, RADIX * nq, d), BF16),
        compiler_params=pltpu.CompilerParams(
            dimension_semantics=("parallel", "arbitrary", "arbitrary"),
            vmem_limit_bytes=_vmem_limit(40 * 1024 * 1024)),
        name="fseq",
    )(u, t)


def _rope_lane_tables(n_ctx, n_tok):
    rows_count = n_tok // GRID_W
    row = np.repeat(np.arange(rows_count), GRID_W).astype(np.float32)
    col = np.tile(np.arange(GRID_W), rows_count).astype(np.float32)
    inv = np.float32(ROPE_THETA) ** (-np.arange(ROPE_PAIRS, dtype=np.float32) / np.float32(ROPE_PAIRS))
    lane = np.arange(LANES)
    dd = lane % HEAD_DIM
    axis = dd // (2 * ROPE_PAIRS)
    half = (dd % (2 * ROPE_PAIRS)) // ROPE_PAIRS
    pair = dd % ROPE_PAIRS
    pos = np.where((axis == 0)[None, :], row[:, None], col[:, None])
    ang = (pos * inv.astype(np.float32)[pair][None, :]).astype(np.float32).astype(np.float64)
    cosv = np.cos(ang)
    sinv = np.sin(ang)
    first = (half == 0)[None, :]
    s1 = np.where(first, -sinv, 0.0)
    s2 = np.where(first, 0.0, sinv)
    pad1 = np.ones((n_ctx, LANES))
    pad0 = np.zeros((n_ctx, LANES))
    return tuple(jnp.asarray(np.concatenate([tab, pad], 0), F32)
                 for pad, tab in ((pad1, cosv), (pad0, s1), (pad0, s2)))


def _permute_w_in(w_in):
    d = w_in.shape[0]
    off_ak = A_W
    off_av = 2 * A_W
    off_bq = 3 * A_W
    off_bk = off_bq + BQ_W
    off_bv = off_bk + BKV_W
    grp = GQA_HEADS // GQA_KV_HEADS
    bq = w_in[:, off_bq:off_bk].reshape(d, GQA_KV_HEADS, grp, HEAD_DIM).swapaxes(1, 2).reshape(d, BQ_W)
    return jnp.concatenate([w_in[:, 0:A_W], bq, w_in[:, off_ak:off_av], w_in[:, off_bk:off_bv],
                            w_in[:, off_av:off_bq], w_in[:, off_bv:off_bv + BKV_W]], axis=1)


def _permute_w_out(w_out):
    d = w_out.shape[1]
    grp = GQA_HEADS // GQA_KV_HEADS
    wb = w_out[A_W:].reshape(GQA_KV_HEADS, grp, HEAD_DIM, d).swapaxes(0, 1).reshape(BQ_W, d)
    return jnp.concatenate([w_out[:A_W], wb], axis=0)


def kernel(x, c, ctx, c_ctx, l0_ada_w, l0_ada_b, l0_norm_mix, l0_w_in, l0_lambda_q1, l0_lambda_k1, l0_lambda_q2, l0_lambda_k2, l0_subln, l0_q_norm, l0_k_norm, l0_w_out, l0_norm_ffn, l0_w_gate_up, l0_w_down, l1_ada_w, l1_ada_b, l1_norm_mix, l1_w_out, l1_norm_ffn, l1_w_gate_up, l1_w_down, final_norm):
    b, s, d = x.shape
    n_ctx = ctx.shape[1]
    assert d == D_MODEL and s % GRID_W == 0

    tt = min(256, n_ctx, s)
    tq = min(128, tt)
    nsub = max(n for n in (1, 2, 4) if s % (n * tq) == 0)
    tk = min(256, n_ctx)
    tm = min(1024, s)
    tmr = min(512, s // RADIX)
    ta = min(128, s // RADIX ** 2)
    assert n_ctx % tt == 0 and s % tt == 0 and tt % tq == 0 and (n_ctx + s) % tk == 0
    assert s % tm == 0 and (s // RADIX) % tmr == 0 and (s // RADIX ** 2) % ta == 0

    n_rows = -(-(b + 1) // SUBLANES) * SUBLANES
    cond = jnp.concatenate([c, c_ctx[None, :], jnp.zeros((n_rows - b - 1, d), F32)], axis=0)
    m0, m1, wcs = _ada_fold(cond, l0_ada_w, l0_ada_b, l1_ada_w, l1_ada_b, l1_w_out)
    mod0 = m0[:b].reshape(b, N_MOD, d)
    mod0c = m0[b:b + 1].reshape(1, N_MOD, d)
    mod1 = m1[:b].reshape(b, N_MOD, d)

    w_perm = _permute_w_in(l0_w_in)
    cos_t, s1_t, s2_t = _rope_lane_tables(n_ctx, s)
    qg = jnp.tile(l0_q_norm, LANES // HEAD_DIM).reshape(1, LANES)
    kg = jnp.tile(l0_k_norm, LANES // HEAD_DIM).reshape(1, LANES)
    li = np.arange(LANES)
    ones_bd = jnp.asarray((li[:, None] // HEAD_DIM) == (li[None, :] // HEAD_DIM), BF16)
    qt, k, vt = _proj(x, ctx, mod0, mod0c, l0_norm_mix.reshape(1, d), w_perm.astype(BF16),
                      cos_t, s1_t, s2_t, qg, kg, ones_bd, tt, tq)
    lam_vecs = jnp.stack([l0_lambda_q1, l0_lambda_k1, l0_lambda_q2, l0_lambda_k2]).astype(F32)
    o, (w_out0, w_gu0, w_dn0, w_gu1, w_dn1) = _attn(
        qt, k, vt, lam_vecs, l0_subln.reshape(1, LANES), s, tq, tk, nsub,
        [_permute_w_out(l0_w_out), l0_w_gate_up, l0_w_down, l1_w_gate_up, l1_w_down])
    xr = _mixffn(o, x, mod0, w_out0, l0_norm_ffn.reshape(1, d), w_gu0, w_dn0, tm)
    xr = xr.reshape(b, s, d)

    u = _fproj(xr, mod1, l1_norm_mix.reshape(1, d), wcs, ta)
    yr = _fseq(u, _seq_dft_tables(s).astype(BF16), tmr)
    res4 = (b, RADIX, s // RADIX, d)
    return _ffn_final(xr.reshape(res4), yr.reshape(res4), mod1, l1_norm_ffn.reshape(1, d),
                      w_gu1, w_dn1, final_norm.reshape(1, d), tm)
```
